```python
import math
import jax, jax.numpy as jnp
from jax import lax
import numpy as np

D_MODEL = 1024
BATCH = 32
SEQ = 256
DEPTH = 1
DEC_BATCH = 8
DEC_SEQ = 4096
PAST_LEN = 256

GRID_W = 64
EPS = 1e-6
N_MLSTM_HEADS = 4
MLSTM_HEAD_DIM = D_MODEL // 8
MLSTM_WIDTH = N_MLSTM_HEADS * MLSTM_HEAD_DIM
MLSTM_CHUNK = 128
FORGET_BIAS = 3.0
N_GATES = 4 * N_MLSTM_HEADS
N_DIFF_HEADS = 4
DIFF_QK_DIM = D_MODEL // 16
DIFF_V_DIM = 2 * DIFF_QK_DIM
DIFF_WIDTH = N_DIFF_HEADS * DIFF_V_DIM
DIFF_QK_WIDTH = N_DIFF_HEADS * 2 * DIFF_QK_DIM
Q_BLOCK = 128
ROPE_THETA = 10000.0
MIX_WIDTH = MLSTM_WIDTH + DIFF_WIDTH
OFF_MQ = 0
OFF_MK = OFF_MQ + MLSTM_WIDTH
OFF_MV = OFF_MK + MLSTM_WIDTH
OFF_MO = OFF_MV + MLSTM_WIDTH
OFF_GATE = OFF_MO + MLSTM_WIDTH
OFF_DQ = OFF_GATE + N_GATES
OFF_DK = OFF_DQ + DIFF_QK_WIDTH
OFF_DV = OFF_DK + DIFF_QK_WIDTH
IN_WIDTH = OFF_DV + DIFF_WIDTH
N_GROUPS = 4
EXPERTS_PER_GROUP = 4
N_EXPERTS = N_GROUPS * EXPERTS_PER_GROUP
TOP_K_IN_GROUP = 2
EXPERT_HIDDEN = D_MODEL // 4

kernel_name = 'hymba_mlstm_diffattn_hmoe_diffusion_step'


def rmsnorm(x, g):
    x32 = x.astype(jnp.float32)
    y = x32 * lax.rsqrt(jnp.mean(x32 * x32, axis=-1, keepdims=True) + EPS)
    return (y * g.astype(jnp.float32)).astype(x.dtype)


def head_rmsnorm(x):
    x32 = x.astype(jnp.float32)
    return (x32 * lax.rsqrt(jnp.mean(x32 * x32, axis=-1, keepdims=True) + EPS)).astype(x.dtype)


def ada_mod(cvec, w_ada, b_ada):
    m = jax.nn.silu(cvec) @ w_ada + b_ada
    return jnp.split(m[:, None, :], 6, axis=-1)


def project(h, w_in, b_gate):
    B, S, _ = h.shape
    p = h @ w_in
    q_m = p[..., OFF_MQ:OFF_MK].reshape(B, S, N_MLSTM_HEADS, MLSTM_HEAD_DIM)
    k_m = p[..., OFF_MK:OFF_MV].reshape(B, S, N_MLSTM_HEADS, MLSTM_HEAD_DIM) * (MLSTM_HEAD_DIM ** -0.5)
    v_m = p[..., OFF_MV:OFF_MO].reshape(B, S, N_MLSTM_HEADS, MLSTM_HEAD_DIM)
    o_m = p[..., OFF_MO:OFF_GATE]
    gates = (p[..., OFF_GATE:OFF_DQ].astype(jnp.float32) + b_gate.astype(jnp.float32)).reshape(B, S, 4, N_MLSTM_HEADS)
    q_d = p[..., OFF_DQ:OFF_DK].reshape(B, S, N_DIFF_HEADS, 2, DIFF_QK_DIM)
    k_d = p[..., OFF_DK:OFF_DV].reshape(B, S, N_DIFF_HEADS, 2, DIFF_QK_DIM)
    v_d = p[..., OFF_DV:IN_WIDTH].reshape(B, S, N_DIFF_HEADS, DIFF_V_DIM)
    return q_m, k_m, v_m, o_m, gates, q_d, k_d, v_d


def rope_1d(x, pos):
    half = x.shape[-1] // 2
    freqs = ROPE_THETA ** (-jnp.arange(half, dtype=jnp.float32) / half)
    ang = pos.astype(jnp.float32)[:, None] * freqs[None, :]
    cos = jnp.cos(ang)[None, :, None, None, :]
    sin = jnp.sin(ang)[None, :, None, None, :]
    x32 = x.astype(jnp.float32)
    x1, x2 = x32[..., :half], x32[..., half:]
    return jnp.concatenate([x1 * cos - x2 * sin, x1 * sin + x2 * cos], axis=-1).astype(x.dtype)


def axial_rope(x):
    S = x.shape[1]
    n_rows = S // GRID_W
    row = jnp.repeat(jnp.arange(n_rows), GRID_W)
    col = jnp.tile(jnp.arange(GRID_W), n_rows)
    half = x.shape[-1] // 2
    return jnp.concatenate([rope_1d(x[..., :half], row), rope_1d(x[..., half:], col)], axis=-1)


def diff_lambda_value(lp, lam_init):
    lp = lp.astype(jnp.float32)
    return jnp.exp(jnp.sum(lp[0] * lp[1])) - jnp.exp(jnp.sum(lp[2] * lp[3])) + lam_init


def diff_attention_block(q, k, v, lam):
    s = jnp.einsum('bqhmd,bkhmd->bhmqk', q, k).astype(jnp.float32) * (DIFF_QK_DIM ** -0.5)
    p = jax.nn.softmax(s, axis=-1)
    a = p[:, :, 0] - lam * p[:, :, 1]
    return jnp.einsum('bhqk,bkhd->bqhd', a.astype(v.dtype), v)


def blocked_diff_attention(q, k, v, lam):
    B, S, H, _, d = q.shape
    nb = S // Q_BLOCK
    qb = q.reshape(B, nb, Q_BLOCK, H, 2, d).swapaxes(0, 1)
    out = lax.map(lambda qi: diff_attention_block(qi, k, v, lam), qb)
    return out.swapaxes(0, 1).reshape(B, S, H, v.shape[-1])


def mlstm_chunk_scan(q, k, v, i_pre, f_pre, C0, n0, m0):
    B, S, H, d = q.shape
    nc = S // MLSTM_CHUNK
    f32 = jnp.float32

    def chunks(t):
        return t.reshape((B, nc, MLSTM_CHUNK) + t.shape[2:]).swapaxes(0, 1)

    xs = (chunks(q), chunks(k), chunks(v), chunks(i_pre.astype(f32)),
          chunks(jax.nn.log_sigmoid(f_pre.astype(f32))))
    causal = jnp.tril(jnp.ones((MLSTM_CHUNK, MLSTM_CHUNK), dtype=bool))

    def step(carry, inp):
        C, n, m = carry
        qc, kc, vc, ic, lfc = inp
        qc, kc, vc = qc.astype(f32), kc.astype(f32), vc.astype(f32)
        b = jnp.cumsum(lfc, axis=1).swapaxes(1, 2)
        ig = ic.swapaxes(1, 2)
        dmat = jnp.where(causal, b[..., :, None] - b[..., None, :] + ig[..., None, :], -jnp.inf)
        inter = b + m[..., None]
        m_t = jnp.maximum(inter, jnp.max(dmat, axis=-1))
        w_intra = jnp.exp(dmat - m_t[..., None])
        w_state = jnp.exp(inter - m_t)
        s = jnp.einsum('blhd,bshd->bhls', qc, kc) * w_intra
        num = jnp.einsum('bhls,bshd->bhld', s, vc) + w_state[..., None] * jnp.einsum('bhvk,blhk->bhlv', C, qc)
        den = jnp.sum(s, axis=-1) + w_state * jnp.einsum('bhk,blhk->bhl', n, qc)
        h = num / jnp.maximum(jnp.abs(den), jnp.exp(-m_t))[..., None]
        b_last = b[..., -1]
        g = b_last[..., None] - b + ig
        m_new = jnp.maximum(b_last + m, jnp.max(g, axis=-1))
        w_k = jnp.exp(g - m_new[..., None])
        decay = jnp.exp(b_last + m - m_new)
        C_new = decay[..., None, None] * C + jnp.einsum('bhs,bshv,bshk->bhvk', w_k, vc, kc)
        n_new = decay[..., None] * n + jnp.einsum('bhs,bshk->bhk', w_k, kc)
        return (C_new, n_new, m_new), h.swapaxes(1, 2)

    (C, n, m), hs = lax.scan(step, (C0.astype(f32), n0.astype(f32), m0.astype(f32)), xs)
    return hs.swapaxes(0, 1).reshape(B, S, H, d).astype(q.dtype), C, n, m


def mlstm_bidirectional(q, k, v, gates, C0, n0, m0):
    rev = lambda t: jnp.flip(t, axis=1)
    h_f, C_f, n_f, m_f = mlstm_chunk_scan(q, k, v, gates[:, :, 0], gates[:, :, 1], C0[:, 0], n0[:, 0], m0[:, 0])
    h_b, C_b, n_b, m_b = mlstm_chunk_scan(rev(q), rev(k), rev(v), rev(gates[:, :, 2]), rev(gates[:, :, 3]),
                                          C0[:, 1], n0[:, 1], m0[:, 1])
    h = h_f + rev(h_b)
    return h, jnp.stack([C_f, C_b], axis=1), jnp.stack([n_f, n_b], axis=1), jnp.stack([m_f, m_b], axis=1)


def mix_output(h_m, o_m, h_d, lam_init, g_m, g_d, w_out):
    B, S = h_m.shape[:2]
    hm = head_rmsnorm(h_m).reshape(B, S, MLSTM_WIDTH) * g_m * jax.nn.sigmoid(o_m)
    hd = head_rmsnorm(h_d).reshape(B, S, DIFF_WIDTH) * g_d * (1.0 - lam_init)
    return jnp.concatenate([hm, hd], axis=-1) @ w_out


def hierarchical_moe(h, w_rg, b_rg, w_re, b_re, w_gu, w_dn):
    B, S, D = h.shape
    x = h.reshape(B * S, D)
    f32 = jnp.float32
    lg = (x @ w_rg).astype(f32) + b_rg.astype(f32)
    pg = jax.nn.softmax(lg, axis=-1)
    grp = jnp.argmax(lg, axis=-1)
    p_grp = jnp.max(pg, axis=-1, keepdims=True)
    le = ((x @ w_re).astype(f32) + b_re.astype(f32)).reshape(B * S, N_GROUPS, EXPERTS_PER_GROUP)
    le_sel = jnp.einsum('nge,ng->ne', le, jax.nn.one_hot(grp, N_GROUPS, dtype=f32))
    pe = jax.nn.softmax(le_sel, axis=-1)
    top_p, top_i = lax.top_k(pe, TOP_K_IN_GROUP)
    top_p = top_p / jnp.sum(top_p, axis=-1, keepdims=True)
    expert_idx = grp[:, None] * EXPERTS_PER_GROUP + top_i
    comb = jnp.sum(jax.nn.one_hot(expert_idx, N_EXPERTS, dtype=f32) * (top_p * p_grp)[..., None], axis=1).astype(x.dtype)
    y = jnp.zeros_like(x)
    for e in range(N_EXPERTS):
        a, u = jnp.split(x @ w_gu[e], 2, axis=-1)
        y = y + comb[:, e:e + 1] * ((jax.nn.silu(a) * u) @ w_dn[e])
    return y.reshape(B, S, D)


def setup_inputs(seed: int = 0) -> dict:
    key = jax.random.key(seed)
    ks = jax.random.split(key, 32)
    f32 = jnp.float32
    D = D_MODEL

    def nrm(k, shape, s):
        return s * jax.random.normal(k, shape, f32)

    gate_noise = nrm(ks[14], (DEPTH, 4, N_MLSTM_HEADS), 0.3)
    gate_base = jnp.array([0.0, FORGET_BIAS, 0.0, FORGET_BIAS], f32)[None, :, None]
    return {
        'x_prompt': nrm(ks[0], (BATCH, SEQ, D), 1.0),
        'x_sample': nrm(ks[1], (DEC_BATCH, DEC_SEQ, D), 1.0),
        'cache_k': nrm(ks[2], (DEC_BATCH, DEPTH, PAST_LEN, N_DIFF_HEADS, 2, DIFF_QK_DIM), 1.0),
        'cache_v': nrm(ks[3], (DEC_BATCH, DEPTH, PAST_LEN, N_DIFF_HEADS, DIFF_V_DIM), 1.0),
        'state_C': nrm(ks[4], (DEC_BATCH, DEPTH, 2, N_MLSTM_HEADS, MLSTM_HEAD_DIM, MLSTM_HEAD_DIM), 0.1),
        'state_n': nrm(ks[5], (DEC_BATCH, DEPTH, 2, N_MLSTM_HEADS, MLSTM_HEAD_DIM), 0.3),
        'state_m': nrm(ks[6], (DEC_BATCH, DEPTH, 2, N_MLSTM_HEADS), 1.0),
        'c': nrm(ks[7], (DEC_BATCH, D), 1.0),
        'c_ctx': nrm(ks[8], (D,), 1.0),
        'w_ada': nrm(ks[9], (DEPTH, D, 6 * D), 0.5 * D ** -0.5),
        'b_ada': nrm(ks[10], (DEPTH, 6 * D), 0.01),
        'norm_mix': 1.0 + nrm(ks[11], (DEPTH, D), 0.01),
        'norm_ffn': 1.0 + nrm(ks[12], (DEPTH, D), 0.01),
        'w_in': nrm(ks[13], (DEPTH, D, IN_WIDTH), D ** -0.5),
        'b_gate': (gate_noise + gate_base).reshape(DEPTH, N_GATES),
        'mlstm_norm': 1.0 + nrm(ks[15], (DEPTH, MLSTM_WIDTH), 0.01),
        'diff_norm': 1.0 + nrm(ks[16], (DEPTH, DIFF_WIDTH), 0.01),
        'diff_lambda': nrm(ks[17], (DEPTH, 4, DIFF_QK_DIM), 0.1),
        'w_out': nrm(ks[18], (DEPTH, MIX_WIDTH, D), MIX_WIDTH ** -0.5),
        'w_route_group': nrm(ks[19], (DEPTH, D, N_GROUPS), D ** -0.5),
        'b_route_group': nrm(ks[20], (DEPTH, N_GROUPS), 0.01),
        'w_route_expert': nrm(ks[21], (DEPTH, D, N_EXPERTS), D ** -0.5),
        'b_route_expert': nrm(ks[22], (DEPTH, N_EXPERTS), 0.01),
        'w_gate_up': nrm(ks[23], (DEPTH, N_EXPERTS, D, 2 * EXPERT_HIDDEN), D ** -0.5),
        'w_down': nrm(ks[24], (DEPTH, N_EXPERTS, EXPERT_HIDDEN, D), EXPERT_HIDDEN ** -0.5),
        'final_norm': 1.0 + nrm(ks[25], (D,), 0.01),
    }


def reference(x_prompt, x_sample, cache_k, cache_v, state_C, state_n, state_m, c, c_ctx,
              w_ada, b_ada, norm_mix, norm_ffn, w_in, b_gate, mlstm_norm, diff_norm, diff_lambda, w_out,
              w_route_group, b_route_group, w_route_expert, b_route_expert, w_gate_up, w_down, final_norm):
    xp = x_prompt
    xs = x_sample
    Bp = xp.shape[0]
    new_k, new_v, new_C, new_n, new_m = [], [], [], [], []
    for l in range(DEPTH):
        lam_init = 0.8 - 0.6 * math.exp(-0.3 * l)
        lam = diff_lambda_value(diff_lambda[l], lam_init)
        moe_w = (w_route_group[l], b_route_group[l], w_route_expert[l], b_route_expert[l], w_gate_up[l], w_down[l])

        sh1, sc1, g1, sh2, sc2, g2 = ada_mod(c_ctx[None, :], w_ada[l], b_ada[l])
        h = rmsnorm(xp, norm_mix[l]) * (1.0 + sc1) + sh1
        q_m, k_m, v_m, o_m, gates, q_d, k_d, v_d = project(h, w_in[l], b_gate[l])
        zC = jnp.zeros((Bp, 2, N_MLSTM_HEADS, MLSTM_HEAD_DIM, MLSTM_HEAD_DIM), jnp.float32)
        zn = jnp.zeros((Bp, 2, N_MLSTM_HEADS, MLSTM_HEAD_DIM), jnp.float32)
        zm = jnp.zeros((Bp, 2, N_MLSTM_HEADS), jnp.float32)
        h_m, C_ctx, n_ctx, m_ctx = mlstm_bidirectional(q_m, k_m, v_m, gates, zC, zn, zm)
        h_d = blocked_diff_attention(q_d, k_d, v_d, lam)
        xp = xp + g1 * mix_output(h_m, o_m, h_d, lam_init, mlstm_norm[l], diff_norm[l], w_out[l])
        h = rmsnorm(xp, norm_ffn[l]) * (1.0 + sc2) + sh2
        xp = xp + g2 * hierarchical_moe(h, *moe_w)
        new_k.append(k_d)
        new_v.append(v_d)
        new_C.append(C_ctx)
        new_n.append(n_ctx)
        new_m.append(m_ctx)

        sh1, sc1, g1, sh2, sc2, g2 = ada_mod(c, w_ada[l], b_ada[l])
        h = rmsnorm(xs, norm_mix[l]) * (1.0 + sc1) + sh1
        q_m, k_m, v_m, o_m, gates, q_d, k_d, v_d = project(h, w_in[l], b_gate[l])
        h_m, _, _, _ = mlstm_bidirectional(q_m, k_m, v_m, gates, state_C[:, l], state_n[:, l], state_m[:, l])
        q_r = axial_rope(q_d)
        k_all = jnp.concatenate([axial_rope(k_d), cache_k[:, l].astype(k_d.dtype)], axis=1)
        v_all = jnp.concatenate([v_d, cache_v[:, l].astype(v_d.dtype)], axis=1)
        h_d = blocked_diff_attention(q_r, k_all, v_all, lam)
        xs = xs + g1 * mix_output(h_m, o_m, h_d, lam_init, mlstm_norm[l], diff_norm[l], w_out[l])
        h = rmsnorm(xs, norm_ffn[l]) * (1.0 + sc2) + sh2
        xs = xs + g2 * hierarchical_moe(h, *moe_w)

    y_prompt = rmsnorm(xp, final_norm)
    y_sample = rmsnorm(xs, final_norm)
    return (y_prompt, y_sample, jnp.stack(new_k, axis=1), jnp.stack(new_v, axis=1),
            jnp.stack(new_C, axis=1), jnp.stack(new_n, axis=1), jnp.stack(new_m, axis=1))
```

```python
import functools
import math

import jax
import jax.numpy as jnp
from jax import lax
from jax.experimental import pallas as pl
from jax.experimental.pallas import tpu as pltpu

F32 = jnp.float32
BF16 = jnp.bfloat16
HIGHEST = lax.Precision.HIGHEST

EPS = 1e-6
GRID_W = 64
ROPE_THETA = 10000.0
N_HEADS = 4
HEAD_DIM = 128
QK_DIM = 64
WIDTH = N_HEADS * HEAD_DIM
CHUNK = 128
N_GATES = 4 * N_HEADS
N_GROUPS = 4
EXPERTS_PER_GROUP = 4
N_EXPERTS = 16
LANES = 128
VMEM_LIMIT = 56 * 1024 * 1024


def _params(sem):
    return pltpu.CompilerParams(dimension_semantics=sem, vmem_limit_bytes=VMEM_LIMIT)


def _log_sigmoid(x):
    return jnp.minimum(x, 0.0) - jnp.log1p(jnp.exp(-jnp.abs(x)))


def _sigmoid(x):
    return 1.0 / (1.0 + jnp.exp(-x))


def _ada_kernel(c_ref, w_ref, b_ref, o_ref):
    c = c_ref[...]
    s = c * _sigmoid(c)
    o_ref[...] = jnp.dot(s, w_ref[...], preferred_element_type=F32, precision=HIGHEST) + b_ref[...]


def _ada(cvec, w_ada, b_ada):
    rows, d = cvec.shape
    n = w_ada.shape[1]
    tn = 1024
    return pl.pallas_call(
        _ada_kernel,
        grid=(n // tn,),
        in_specs=[pl.BlockSpec((rows, d), lambda j: (0, 0)),
                  pl.BlockSpec((d, tn), lambda j: (0, j)),
                  pl.BlockSpec((1, tn), lambda j: (0, j))],
        out_specs=pl.BlockSpec((rows, tn), lambda j: (0, j)),
        out_shape=jax.ShapeDtypeStruct((rows, n), F32),
        compiler_params=_params(("arbitrary",)),
        name="ada",
    )(cvec, w_ada, b_ada.reshape(1, n))


def _proj_kernel(*refs, rope, emit_cache):
    x_ref, nrm_ref, sc_ref, sh_ref, w_ref, bg_ref = refs[:6]
    refs = refs[6:]
    if rope:
        cos_ref, sa_ref, sb_ref = refs[:3]
        refs = refs[3:]
    m_ref, q_ref, k_ref, v_ref, g_ref = refs[:5]
    refs = refs[5:]
    if emit_cache:
        newk_ref, newv_ref = refs

    x = x_ref[...]
    ms = jnp.mean(x * x, axis=-1, keepdims=True)
    y = x * lax.rsqrt(ms + EPS) * nrm_ref[...]
    h = (y * (1.0 + sc_ref[0]) + sh_ref[0]).astype(BF16)

    def section(j):
        return jnp.dot(h, w_ref[:, j * WIDTH:(j + 1) * WIDTH], preferred_element_type=F32)

    def rotate(p):
        cos, sa, sb = cos_ref[...], sa_ref[...], sb_ref[...]
        outs = []
        for j in range(N_HEADS):
            xb = p[:, j * LANES:(j + 1) * LANES]
            outs.append(xb * cos + pltpu.roll(xb, LANES - 16, 1) * sa + pltpu.roll(xb, 16, 1) * sb)
        return jnp.concatenate(outs, axis=1)

    for j in range(4):
        p = section(j)
        if j == 1:
            p = p * (HEAD_DIM ** -0.5)
        m_ref[:, j * WIDTH:(j + 1) * WIDTH] = p.astype(BF16)
    pq = section(4)
    if rope:
        pq = rotate(pq)
    q_ref[...] = (pq * (QK_DIM ** -0.5)).astype(BF16)
    pk = section(5)
    if emit_cache:
        newk_ref[...] = pk
    if rope:
        pk = rotate(pk)
    k_ref[...] = pk.astype(BF16)
    pv = section(6)
    if emit_cache:
        newv_ref[...] = pv
    v_ref[...] = pv.astype(BF16)
    pg = jnp.dot(h, w_ref[:, 7 * WIDTH:7 * WIDTH + LANES], preferred_element_type=F32)
    g_ref[...] = pg[:, :N_GATES] + bg_ref[...]


def _proj(x2d, nrm, sc, sh, row0, tiles_per_row, w_r, b_gate, rope_tabs, emit_cache, tm):
    t, d = x2d.shape
    nt = t // tm
    rope = rope_tabs is not None
    row_map = lambda i: (row0 + i // tiles_per_row, 0, 0)
    in_specs = [pl.BlockSpec((tm, d), lambda i: (i, 0)),
                pl.BlockSpec((1, d), lambda i: (0, 0)),
                pl.BlockSpec((1, 1, d), row_map),
                pl.BlockSpec((1, 1, d), row_map),
                pl.BlockSpec(w_r.shape, lambda i: (0, 0)),
                pl.BlockSpec((1, N_GATES), lambda i: (0, 0))]
    args = [x2d, nrm, sc, sh, w_r, b_gate]
    if rope:
        tps = rope_tabs[0].shape[0] // tm
        for tab in rope_tabs:
            in_specs.append(pl.BlockSpec((tm, LANES), lambda i: (i % tps, 0)))
            args.append(tab)
    out_shape = [jax.ShapeDtypeStruct((t, 4 * WIDTH), BF16),
                 jax.ShapeDtypeStruct((t, WIDTH), BF16),
                 jax.ShapeDtypeStruct((t, WIDTH), BF16),
                 jax.ShapeDtypeStruct((t, WIDTH), BF16),
                 jax.ShapeDtypeStruct((t, N_GATES), F32)]
    out_specs = [pl.BlockSpec((tm, 4 * WIDTH), lambda i: (i, 0)),
                 pl.BlockSpec((tm, WIDTH), lambda i: (i, 0)),
                 pl.BlockSpec((tm, WIDTH), lambda i: (i, 0)),
                 pl.BlockSpec((tm, WIDTH), lambda i: (i, 0)),
                 pl.BlockSpec((tm, N_GATES), lambda i: (i, 0))]
    if emit_cache:
        out_shape += [jax.ShapeDtypeStruct((t, WIDTH), F32)] * 2
        out_specs += [pl.BlockSpec((tm, WIDTH), lambda i: (i, 0))] * 2
    return pl.pallas_call(
        functools.partial(_proj_kernel, rope=rope, emit_cache=emit_cache),
        grid=(nt,),
        in_specs=in_specs,
        out_specs=out_specs,
        out_shape=out_shape,
        compiler_params=_params(("parallel",)),
        name="proj_rope" if rope else "proj_ctx",
    )(*args)


def _rope_tables(seq):
    t = jnp.arange(seq)
    row = (t // GRID_W).astype(F32)[:, None]
    col = (t % GRID_W).astype(F32)[:, None]
    lane = jnp.arange(LANES)
    freqs = ROPE_THETA ** (-(lane % 16).astype(F32) / 16.0)
    pos = jnp.where(((lane % 64) < 32)[None, :], row, col)
    ang = pos * freqs[None, :]
    cos, sin = jnp.cos(ang), jnp.sin(ang)
    first = ((lane % 32) < 16)[None, :]
    return cos, jnp.where(first, -sin, 0.0), jnp.where(first, 0.0, sin)


def _mlstm_kernel(q_ref, k_ref, v_ref, g_ref, gt_ref, c0_ref, n0_ref, m0_ref,
                  hf_ref, hb_ref, cf_ref, nf_ref, mf_ref, *, nc, hp):
    L = CHUNK
    cf_ref[...] = c0_ref[...]
    nf_ref[...] = n0_ref[...]
    mf_ref[...] = m0_ref[...]
    rr = lax.broadcasted_iota(jnp.int32, (L, L), 0)
    cc = lax.broadcasted_iota(jnp.int32, (L, L), 1)
    lower = cc <= rr
    upper = cc >= rr
    lower_f = lower.astype(F32)
    upper_f = upper.astype(F32)

    def body(c, carry):
        for d in range(2):
            ci = c if d == 0 else nc - 1 - c
            off = pl.multiple_of(ci * L, L)
            mask = lower if d == 0 else upper
            tri = lower_f if d == 0 else upper_f
            tri_t = upper_f if d == 0 else lower_f
            g = g_ref[0, 0, pl.ds(off, L), :]
            gt = gt_ref[0, 0, :, pl.ds(off, L)]
            bsum = jnp.dot(tri, _log_sigmoid(g), preferred_element_type=F32, precision=HIGHEST)
            bsum_t = jnp.dot(_log_sigmoid(gt), tri_t, preferred_element_type=F32, precision=HIGHEST)
            last = L - 1 if d == 0 else 0
            for hh in range(hp):
                col_i = (2 * d) * hp + hh
                col_f = (2 * d + 1) * hp + hh
                i_col = g[:, col_i:col_i + 1]
                b_col = bsum[:, col_f:col_f + 1]
                i_row = gt[col_i:col_i + 1, :]
                b_row = bsum_t[col_f:col_f + 1, :]
                b_last = b_row[:, last:last + 1]
                lanes = slice(hh * HEAD_DIM, (hh + 1) * HEAD_DIM)
                q = q_ref[0, pl.ds(off, L), lanes]
                k = k_ref[0, pl.ds(off, L), lanes]
                v = v_ref[0, pl.ds(off, L), lanes]
                cmat = cf_ref[0, d, hh]
                n = nf_ref[0, d, hh]
                m = mf_ref[0, d, hh][:, 0:1]

                dmat = jnp.where(mask, b_col - b_row + i_row, -jnp.inf)
                inter = b_col + m
                m_t = jnp.maximum(inter, jnp.max(dmat, axis=-1, keepdims=True))
                w_intra = jnp.exp(dmat - m_t)
                w_state = jnp.exp(inter - m_t)
                s = lax.dot_general(q, k, (((1,), (1,)), ((), ())), preferred_element_type=F32) * w_intra
                qc = lax.dot_general(q, cmat.astype(BF16), (((1,), (1,)), ((), ())),
                                     preferred_element_type=F32)
                num = jnp.dot(s.astype(BF16), v, preferred_element_type=F32) + w_state * qc
                qn = jnp.sum(q.astype(F32) * n, axis=-1, keepdims=True)
                den = jnp.sum(s, axis=-1, keepdims=True) + w_state * qn
                h = num / jnp.maximum(jnp.abs(den), jnp.exp(-m_t))

                g_col = b_last - b_col + i_col
                g_row = b_last - b_row + i_row
                m_new = jnp.maximum(b_last + m, jnp.max(g_row, axis=-1, keepdims=True))
                w_k = jnp.exp(g_col - m_new)
                decay = jnp.exp(b_last + m - m_new)
                vw = (v.astype(F32) * w_k).astype(BF16)
                c_new = decay * cmat + lax.dot_general(vw, k, (((0,), (0,)), ((), ())),
                                                       preferred_element_type=F32)
                n_new = decay * n + jnp.sum(k.astype(F32) * w_k, axis=0, keepdims=True)

                cf_ref[0, d, hh] = c_new
                nf_ref[0, d, hh] = n_new
                mf_ref[0, d, hh] = jnp.broadcast_to(m_new, (1, HEAD_DIM))
                out_ref = hf_ref if d == 0 else hb_ref
                out_ref[0, pl.ds(off, L), lanes] = h.astype(BF16)
        return carry

    lax.fori_loop(0, nc, body, 0)


def _mlstm(qkvo, gates, c0, n0, m0, hp):
    b, s, _ = qkvo.shape
    nc = s // CHUNK
    hg = N_HEADS // hp
    gw = 4 * hp
    g5 = gates.reshape(b, s, 4, hg, hp).transpose(0, 3, 1, 2, 4).reshape(b, hg, s, gw)
    g5t = g5.transpose(0, 1, 3, 2)
    n0r = n0.reshape(b, 2, N_HEADS, 1, HEAD_DIM)
    m0r = jnp.broadcast_to(m0[..., None, None], (b, 2, N_HEADS, 1, HEAD_DIM))
    blk = hp * HEAD_DIM
    nblk = WIDTH // blk
    in_specs = [pl.BlockSpec((1, s, blk), lambda i, j: (i, 0, j)),
                pl.BlockSpec((1, s, blk), lambda i, j: (i, 0, nblk + j)),
                pl.BlockSpec((1, s, blk), lambda i, j: (i, 0, 2 * nblk + j)),
                pl.BlockSpec((1, 1, s, gw), lambda i, j: (i, j, 0, 0)),
                pl.BlockSpec((1, 1, gw, s), lambda i, j: (i, j, 0, 0)),
                pl.BlockSpec((1, 2, hp, HEAD_DIM, HEAD_DIM), lambda i, j: (i, 0, j, 0, 0)),
                pl.BlockSpec((1, 2, hp, 1, HEAD_DIM), lambda i, j: (i, 0, j, 0, 0)),
                pl.BlockSpec((1, 2, hp, 1, HEAD_DIM), lambda i, j: (i, 0, j, 0, 0))]
    out_specs = [pl.BlockSpec((1, s, blk), lambda i, j: (i, 0, j)),
                 pl.BlockSpec((1, s, blk), lambda i, j: (i, 0, j)),
                 pl.BlockSpec((1, 2, hp, HEAD_DIM, HEAD_DIM), lambda i, j: (i, 0, j, 0, 0)),
                 pl.BlockSpec((1, 2, hp, 1, HEAD_DIM), lambda i, j: (i, 0, j, 0, 0)),
                 pl.BlockSpec((1, 2, hp, 1, HEAD_DIM), lambda i, j: (i, 0, j, 0, 0))]
    out_shape = [jax.ShapeDtypeStruct((b, s, WIDTH), BF16),
                 jax.ShapeDtypeStruct((b, s, WIDTH), BF16),
                 jax.ShapeDtypeStruct((b, 2, N_HEADS, HEAD_DIM, HEAD_DIM), F32),
                 jax.ShapeDtypeStruct((b, 2, N_HEADS, 1, HEAD_DIM), F32),
                 jax.ShapeDtypeStruct((b, 2, N_HEADS, 1, HEAD_DIM), F32)]
    hf, hb, cf, nf, mf = pl.pallas_call(
        functools.partial(_mlstm_kernel, nc=nc, hp=hp),
        grid=(b, hg),
        in_specs=in_specs,
        out_specs=out_specs,
        out_shape=out_shape,
        compiler_params=_params(("parallel", "parallel")),
        name="mlstm",
    )(qkvo, qkvo, qkvo, g5, g5t, c0, n0r, m0r)
    return hf, hb, cf, nf[:, :, :, 0, :], mf[:, :, :, 0, 0]


def _attn_kernel(lp_ref, q_ref, k_ref, v_ref, o_ref, *, chunks, tq, lam_init):
    lp = lp_ref[...]
    lam = (jnp.exp(jnp.sum(lp[0:1] * lp[1:2], axis=-1, keepdims=True))
           - jnp.exp(jnp.sum(lp[2:3] * lp[3:4], axis=-1, keepdims=True)) + lam_init)
    q = q_ref[0]
    lane = lax.broadcasted_iota(jnp.int32, q.shape, 1)
    zero = jnp.zeros_like(q)
    qq = jnp.concatenate([jnp.where(lane < QK_DIM, q, zero), jnp.where(lane >= QK_DIM, q, zero)], axis=0)
    m = jnp.full((2 * tq, 1), -jnp.inf, F32)
    l = jnp.zeros((2 * tq, 1), F32)
    acc = jnp.zeros((2 * tq, HEAD_DIM), F32)
    for c0, c1 in chunks:
        kc = k_ref[0, c0:c1, :]
        vc = v_ref[0, c0:c1, :]
        s = lax.dot_general(qq, kc, (((1,), (1,)), ((), ())), preferred_element_type=F32)
        m_new = jnp.maximum(m, jnp.max(s, axis=-1, keepdims=True))
        alpha = jnp.exp(m - m_new)
        e = jnp.exp(s - m_new)
        l = alpha * l + jnp.sum(e, axis=-1, keepdims=True)
        acc = alpha * acc + jnp.dot(e.astype(BF16), vc, preferred_element_type=F32)
        m = m_new
    o = acc / l
    o_ref[0] = (o[:tq] - lam * o[tq:]).astype(BF16)


def _attn(lp, q, k, v, lam_init, tq, tk):
    b, sq, _ = q.shape
    t = k.shape[1]
    bounds = list(range(0, t, tk)) + [t]
    chunks = tuple(zip(bounds[:-1], bounds[1:]))
    return pl.pallas_call(
        functools.partial(_attn_kernel, chunks=chunks, tq=tq, lam_init=lam_init),
        grid=(b, N_HEADS, sq // tq),
        in_specs=[pl.BlockSpec(lp.shape, lambda i, h, j: (0, 0)),
                  pl.BlockSpec((1, tq, HEAD_DIM), lambda i, h, j: (i, j, h)),
                  pl.BlockSpec((1, t, HEAD_DIM), lambda i, h, j: (i, 0, h)),
                  pl.BlockSpec((1, t, HEAD_DIM), lambda i, h, j: (i, 0, h))],
        out_specs=pl.BlockSpec((1, tq, HEAD_DIM), lambda i, h, j: (i, j, h)),
        out_shape=jax.ShapeDtypeStruct((b, sq, WIDTH), BF16),
        compiler_params=_params(("parallel", "parallel", "parallel")),
        name="diff_attn",
    )(lp, q, k, v)


def _head_norm(x):
    outs = []
    for j in range(N_HEADS):
        xb = x[:, j * HEAD_DIM:(j + 1) * HEAD_DIM]
        outs.append(xb * lax.rsqrt(jnp.mean(xb * xb, axis=-1, keepdims=True) + EPS))
    return jnp.concatenate(outs, axis=1)


def _mix_kernel(hf_ref, hb_ref, om_ref, hd_ref, x_ref, g1_ref, sc2_ref, sh2_ref, gm_ref, gd_ref,
                wout_ref, nrm_ref, wr_ref, br_ref, x1_ref, h2_ref, comb_ref, *, diff_scale):
    hm = _head_norm(hf_ref[...].astype(F32) + hb_ref[...].astype(F32))
    hm = hm * gm_ref[...] * _sigmoid(om_ref[...].astype(F32))
    hd = _head_norm(hd_ref[...].astype(F32)) * gd_ref[...] * diff_scale
    mix = (jnp.dot(hm.astype(BF16), wout_ref[:WIDTH, :], preferred_element_type=F32)
           + jnp.dot(hd.astype(BF16), wout_ref[WIDTH:, :], preferred_element_type=F32))
    x1 = x_ref[...] + g1_ref[0] * mix
    x1_ref[...] = x1
    ms = jnp.mean(x1 * x1, axis=-1, keepdims=True)
    h2 = (x1 * lax.rsqrt(ms + EPS) * nrm_ref[...]) * (1.0 + sc2_ref[0]) + sh2_ref[0]
    h2_ref[...] = h2.astype(BF16)

    logits = jnp.dot(h2, wr_ref[...], preferred_element_type=F32, precision=HIGHEST) + br_ref[...]
    lane = lax.broadcasted_iota(jnp.int32, logits.shape, 1).astype(F32)
    big = float(LANES)
    is_grp = (lane >= N_EXPERTS) & (lane < N_EXPERTS + N_GROUPS)
    lg = jnp.where(is_grp, logits, -jnp.inf)
    mx = jnp.max(lg, axis=-1, keepdims=True)
    p_grp = 1.0 / jnp.sum(jnp.exp(lg - mx), axis=-1, keepdims=True)
    grp = jnp.min(jnp.where(lg == mx, lane, big), axis=-1, keepdims=True) - N_EXPERTS
    base = grp * EXPERTS_PER_GROUP
    sel = (lane >= base) & (lane < base + EXPERTS_PER_GROUP)
    le = jnp.where(sel, logits, -jnp.inf)
    ee = jnp.exp(le - jnp.max(le, axis=-1, keepdims=True))
    pe = ee / jnp.sum(ee, axis=-1, keepdims=True)
    top1 = jnp.max(pe, axis=-1, keepdims=True)
    idx1 = jnp.min(jnp.where(sel & (pe == top1), lane, big), axis=-1, keepdims=True)
    rest = sel & (lane != idx1)
    top2 = jnp.max(jnp.where(rest, pe, -1.0), axis=-1, keepdims=True)
    idx2 = jnp.min(jnp.where(rest & (pe == top2), lane, big), axis=-1, keepdims=True)
    denom = top1 + top2
    w1 = top1 / denom * p_grp
    w2 = top2 / denom * p_grp
    comb = jnp.where(lane == idx1, w1, 0.0) + jnp.where(lane == idx2, w2, 0.0)
    comb_ref[...] = comb[:, :N_EXPERTS]


def _mix(hf, hb, qkvo, hd, x2d, g1, sc2, sh2, row0, tiles_per_row, gm, gd, wout, nrm, wr, br, diff_scale, tm):
    t, d = x2d.shape
    row_map = lambda i: (row0 + i // tiles_per_row, 0, 0)
    tok = lambda i: (i, 0)
    full = lambda i: (0, 0)
    return pl.pallas_call(
        functools.partial(_mix_kernel, diff_scale=diff_scale),
        grid=(t // tm,),
        in_specs=[pl.BlockSpec((tm, WIDTH), tok),
                  pl.BlockSpec((tm, WIDTH), tok),
                  pl.BlockSpec((tm, WIDTH), lambda i: (i, 3)),
                  pl.BlockSpec((tm, WIDTH), tok),
                  pl.BlockSpec((tm, d), tok),
                  pl.BlockSpec((1, 1, d), row_map),
                  pl.BlockSpec((1, 1, d), row_map),
                  pl.BlockSpec((1, 1, d), row_map),
                  pl.BlockSpec((1, WIDTH), full),
                  pl.BlockSpec((1, WIDTH), full),
                  pl.BlockSpec(wout.shape, full),
                  pl.BlockSpec((1, d), full),
                  pl.BlockSpec(wr.shape, full),
                  pl.BlockSpec(br.shape, full)],
        out_specs=[pl.BlockSpec((tm, d), tok),
                   pl.BlockSpec((tm, d), tok),
                   pl.BlockSpec((tm, N_EXPERTS), tok)],
        out_shape=[jax.ShapeDtypeStruct((t, d), F32),
                   jax.ShapeDtypeStruct((t, d), BF16),
                   jax.ShapeDtypeStruct((t, N_EXPERTS), F32)],
        compiler_params=_params(("parallel",)),
        name="mix_router",
    )(hf, hb, qkvo, hd, x2d, g1, sc2, sh2, gm, gd, wout, nrm, wr, br)


def _moe_kernel(h2_ref, comb_ref, wgu_ref, wdn_ref, x1_ref, g2_ref, nrm_ref, y_ref, acc_ref, *, hidden):
    e = pl.program_id(1)

    @pl.when(e == 0)
    def _():
        acc_ref[...] = jnp.zeros_like(acc_ref)

    comb = comb_ref[...]
    lane = lax.broadcasted_iota(jnp.int32, comb.shape, 1)
    ce = jnp.sum(jnp.where(lane == e, comb, 0.0), axis=-1, keepdims=True)
    gu = jnp.dot(h2_ref[...], wgu_ref[0], preferred_element_type=F32)
    a, u = gu[:, :hidden], gu[:, hidden:]
    act = (a * _sigmoid(a) * u * ce).astype(BF16)
    acc_ref[...] += jnp.dot(act, wdn_ref[0], preferred_element_type=F32)

    @pl.when(e == pl.num_programs(1) - 1)
    def _():
        x2 = x1_ref[...] + g2_ref[0] * acc_ref[...]
        ms = jnp.mean(x2 * x2, axis=-1, keepdims=True)
        y_ref[...] = x2 * lax.rsqrt(ms + EPS) * nrm_ref[...]


def _moe(h2, comb, wgu, wdn, x1, g2, row0, tiles_per_row, nrm, tm):
    t, d = x1.shape
    ne, _, two_h = wgu.shape
    hidden = two_h // 2
    row_map = lambda i, e: (row0 + i // tiles_per_row, 0, 0)
    tok = lambda i, e: (i, 0)
    return pl.pallas_call(
        functools.partial(_moe_kernel, hidden=hidden),
        grid=(t // tm, ne),
        in_specs=[pl.BlockSpec((tm, d), tok),
                  pl.BlockSpec((tm, N_EXPERTS), tok),
                  pl.BlockSpec((1, d, two_h), lambda i, e: (e, 0, 0)),
                  pl.BlockSpec((1, hidden, d), lambda i, e: (e, 0, 0)),
                  pl.BlockSpec((tm, d), tok),
                  pl.BlockSpec((1, 1, d), row_map),
                  pl.BlockSpec((1, d), lambda i, e: (0, 0))],
        out_specs=pl.BlockSpec((tm, d), tok),
        out_shape=jax.ShapeDtypeStruct((t, d), F32),
        scratch_shapes=[pltpu.VMEM((tm, d), F32)],
        compiler_params=_params(("parallel", "arbitrary")),
        name="moe_experts",
    )(h2, comb, wgu, wdn, x1, g2, nrm)


def _layer(x, mods, row0, per_request, weights, states, cache, rope_tabs, emit_cache, tm, tq, tk, hp):
    b, s, d = x.shape
    sh1, sc1, g1, sh2, sc2, g2 = mods
    (nrm_mix, nrm_ffn, w_in_r, b_gate, gm, gd, lp, wout, wr, br, wgu, wdn, nrm_final, lam_init) = weights
    x2d = x.reshape(b * s, d)
    tiles_per_row = (s // tm) if per_request else (b * s // tm)
    outs = _proj(x2d, nrm_mix, sc1, sh1, row0, tiles_per_row, w_in_r, b_gate, rope_tabs, emit_cache, tm)
    qkvo, qd, kd, vd, gates = outs[:5]
    c0, n0, m0 = states
    hf, hb, cf, nf, mf = _mlstm(qkvo.reshape(b, s, 4 * WIDTH), gates.reshape(b, s, N_GATES), c0, n0, m0, hp)
    qd = qd.reshape(b, s, WIDTH)
    kd = kd.reshape(b, s, WIDTH)
    vd = vd.reshape(b, s, WIDTH)
    if cache is not None:
        ck, cv = cache
        kd = jnp.concatenate([kd, ck.reshape(b, -1, WIDTH).astype(BF16)], axis=1)
        vd = jnp.concatenate([vd, cv.reshape(b, -1, WIDTH).astype(BF16)], axis=1)
    hd = _attn(lp, qd, kd, vd, lam_init, tq, tk)
    x1, h2, comb = _mix(hf.reshape(b * s, WIDTH), hb.reshape(b * s, WIDTH), qkvo, hd.reshape(b * s, WIDTH), x2d,
                        g1, sc2, sh2, row0, tiles_per_row, gm, gd, wout, nrm_ffn, wr, br, 1.0 - lam_init, tm)
    tm_moe = 2 * tm
    y = _moe(h2, comb, wgu, wdn, x1, g2, row0, tiles_per_row // 2 if per_request else b * s // tm_moe,
             nrm_final, tm_moe)
    new_kv = outs[5:] if emit_cache else None
    return y.reshape(b, s, d), new_kv, (cf, nf, mf)


def kernel(x_prompt, x_sample, cache_k, cache_v, state_C, state_n, state_m, c, c_ctx, w_ada, b_ada, norm_mix, norm_ffn, w_in, b_gate, mlstm_norm, diff_norm, diff_lambda, w_out, w_route_group, b_route_group, w_route_expert, b_route_expert, w_gate_up, w_down, final_norm):
    depth = w_in.shape[0]
    assert depth == 1, "the final norm is fused into the last layer's expert kernel"
    bp, sp, d = x_prompt.shape
    bs, ss, _ = x_sample.shape
    rows = 16
    assert bs < rows
    cvec = jnp.zeros((rows, d), F32).at[:bs].set(c).at[bs].set(c_ctx)
    rope_tabs = _rope_tables(ss)
    xp, xs = x_prompt, x_sample
    new_k, new_v, new_c, new_n, new_m = [], [], [], [], []
    for l in range(depth):
        lam_init = 0.8 - 0.6 * math.exp(-0.3 * l)
        mod = _ada(cvec, w_ada[l], b_ada[l])
        mods = [mod[:, i * d:(i + 1) * d].reshape(rows, 1, d) for i in range(6)]
        wl = w_in[l]
        w_in_r = jnp.concatenate(
            [wl[:, :4 * WIDTH], wl[:, 4 * WIDTH + N_GATES:], wl[:, 4 * WIDTH:4 * WIDTH + N_GATES],
             jnp.zeros((d, LANES - N_GATES), F32)], axis=1).astype(BF16)
        wr = jnp.concatenate([w_route_expert[l], w_route_group[l],
                              jnp.zeros((d, LANES - N_EXPERTS - N_GROUPS), F32)], axis=1)
        br = jnp.concatenate([b_route_expert[l], b_route_group[l],
                              jnp.zeros((LANES - N_EXPERTS - N_GROUPS,), F32)]).reshape(1, LANES)
        weights = (norm_mix[l].reshape(1, d), norm_ffn[l].reshape(1, d), w_in_r, b_gate[l].reshape(1, N_GATES),
                   mlstm_norm[l].reshape(1, WIDTH), diff_norm[l].reshape(1, WIDTH), diff_lambda[l],
                   w_out[l].astype(BF16), wr, br, w_gate_up[l].astype(BF16), w_down[l].astype(BF16),
                   final_norm.reshape(1, d), lam_init)

        zero_states = (jnp.zeros((bp, 2, N_HEADS, HEAD_DIM, HEAD_DIM), F32),
                       jnp.zeros((bp, 2, N_HEADS, HEAD_DIM), F32),
                       jnp.zeros((bp, 2, N_HEADS), F32))
        xp, (nk, nv), (cf, nf, mf) = _layer(xp, mods, bs, False, weights, zero_states, None, None, True,
                                            tm=512, tq=256, tk=256, hp=2)
        new_k.append(nk.reshape(bp, sp, N_HEADS, 2, QK_DIM))
        new_v.append(nv.reshape(bp, sp, N_HEADS, HEAD_DIM))
        new_c.append(cf)
        new_n.append(nf)
        new_m.append(mf)

        states = (state_C[:, l], state_n[:, l], state_m[:, l])
        xs, _, _ = _layer(xs, mods, 0, True, weights, states, (cache_k[:, l], cache_v[:, l]), rope_tabs, False,
                          tm=512, tq=256, tk=1024, hp=2)

    return (xp, xs, jnp.stack(new_k, axis=1), jnp.stack(new_v, axis=1),
            jnp.stack(new_c, axis=1), jnp.stack(new_n, axis=1), jnp.stack(new_m, axis=1))
```

```python
import functools
import math

import jax
import jax.numpy as jnp
from jax import lax
from jax.experimental import pallas as pl
from jax.experimental.pallas import tpu as pltpu

F32 = jnp.float32
BF16 = jnp.bfloat16
HIGHEST = lax.Precision.HIGHEST

EPS = 1e-6
GRID_W = 64
ROPE_THETA = 10000.0
N_HEADS = 4
HEAD_DIM = 128
QK_DIM = 64
WIDTH = N_HEADS * HEAD_DIM
CHUNK = 128
N_GATES = 4 * N_HEADS
N_GROUPS = 4
EXPERTS_PER_GROUP = 4
N_EXPERTS = 16
LANES = 128
MOE_BLOCK = 256
VMEM_LIMIT = 56 * 1024 * 1024


def _params(sem):
    return pltpu.CompilerParams(dimension_semantics=sem, vmem_limit_bytes=VMEM_LIMIT)


def _log_sigmoid(x):
    return jnp.minimum(x, 0.0) - jnp.log1p(jnp.exp(-jnp.abs(x)))


def _sigmoid(x):
    return 1.0 / (1.0 + jnp.exp(-x))


def _ada_kernel(c_ref, w_ref, b_ref, o_ref):
    c = c_ref[...]
    s = c * _sigmoid(c)
    o_ref[...] = jnp.dot(s, w_ref[...], preferred_element_type=F32, precision=HIGHEST) + b_ref[...]


def _ada(cvec, w_ada, b_ada):
    rows, d = cvec.shape
    n = w_ada.shape[1]
    tn = 1024
    return pl.pallas_call(
        _ada_kernel,
        grid=(n // tn,),
        in_specs=[pl.BlockSpec((rows, d), lambda j: (0, 0)),
                  pl.BlockSpec((d, tn), lambda j: (0, j)),
                  pl.BlockSpec((1, tn), lambda j: (0, j))],
        out_specs=pl.BlockSpec((rows, tn), lambda j: (0, j)),
        out_shape=jax.ShapeDtypeStruct((rows, n), F32),
        compiler_params=_params(("arbitrary",)),
        name="ada",
    )(cvec, w_ada, b_ada.reshape(1, n))


def _proj_kernel(*refs, rope, emit_cache):
    x_ref, nrm_ref, sc_ref, sh_ref, w_ref, bg_ref = refs[:6]
    refs = refs[6:]
    if rope:
        cos_ref, sa_ref, sb_ref = refs[:3]
        refs = refs[3:]
    m_ref, q_ref, k_ref, v_ref, g_ref = refs[:5]
    refs = refs[5:]
    if emit_cache:
        newk_ref, newv_ref = refs

    x = x_ref[...]
    ms = jnp.mean(x * x, axis=-1, keepdims=True)
    y = x * lax.rsqrt(ms + EPS) * nrm_ref[...]
    h = (y * (1.0 + sc_ref[0]) + sh_ref[0]).astype(BF16)

    def section(j):
        return jnp.dot(h, w_ref[:, j * WIDTH:(j + 1) * WIDTH], preferred_element_type=F32)

    def rotate(p):
        cos, sa, sb = cos_ref[...], sa_ref[...], sb_ref[...]
        outs = []
        for j in range(N_HEADS):
            xb = p[:, j * LANES:(j + 1) * LANES]
            outs.append(xb * cos + pltpu.roll(xb, LANES - 16, 1) * sa + pltpu.roll(xb, 16, 1) * sb)
        return jnp.concatenate(outs, axis=1)

    for j in range(4):
        p = section(j)
        if j == 1:
            p = p * (HEAD_DIM ** -0.5)
        m_ref[:, j * WIDTH:(j + 1) * WIDTH] = p.astype(BF16)
    pq = section(4)
    if rope:
        pq = rotate(pq)
    q_ref[...] = (pq * (QK_DIM ** -0.5)).astype(BF16)
    pk = section(5)
    if emit_cache:
        newk_ref[...] = pk
    if rope:
        pk = rotate(pk)
    k_ref[...] = pk.astype(BF16)
    pv = section(6)
    if emit_cache:
        newv_ref[...] = pv
    v_ref[...] = pv.astype(BF16)
    pg = jnp.dot(h, w_ref[:, 7 * WIDTH:7 * WIDTH + LANES], preferred_element_type=F32)
    g_ref[...] = pg[:, :N_GATES] + bg_ref[...]


def _proj(x2d, nrm, sc, sh, row0, tiles_per_row, w_r, b_gate, rope_tabs, emit_cache, tm):
    t, d = x2d.shape
    nt = t // tm
    rope = rope_tabs is not None
    row_map = lambda i: (row0 + i // tiles_per_row, 0, 0)
    in_specs = [pl.BlockSpec((tm, d), lambda i: (i, 0)),
                pl.BlockSpec((1, d), lambda i: (0, 0)),
                pl.BlockSpec((1, 1, d), row_map),
                pl.BlockSpec((1, 1, d), row_map),
                pl.BlockSpec(w_r.shape, lambda i: (0, 0)),
                pl.BlockSpec((1, N_GATES), lambda i: (0, 0))]
    args = [x2d, nrm, sc, sh, w_r, b_gate]
    if rope:
        tps = rope_tabs[0].shape[0] // tm
        for tab in rope_tabs:
            in_specs.append(pl.BlockSpec((tm, LANES), lambda i: (i % tps, 0)))
            args.append(tab)
    out_shape = [jax.ShapeDtypeStruct((t, 4 * WIDTH), BF16),
                 jax.ShapeDtypeStruct((t, WIDTH), BF16),
                 jax.ShapeDtypeStruct((t, WIDTH), BF16),
                 jax.ShapeDtypeStruct((t, WIDTH), BF16),
                 jax.ShapeDtypeStruct((t, N_GATES), F32)]
    out_specs = [pl.BlockSpec((tm, 4 * WIDTH), lambda i: (i, 0)),
                 pl.BlockSpec((tm, WIDTH), lambda i: (i, 0)),
                 pl.BlockSpec((tm, WIDTH), lambda i: (i, 0)),
                 pl.BlockSpec((tm, WIDTH), lambda i: (i, 0)),
                 pl.BlockSpec((tm, N_GATES), lambda i: (i, 0))]
    if emit_cache:
        out_shape += [jax.ShapeDtypeStruct((t, WIDTH), F32)] * 2
        out_specs += [pl.BlockSpec((tm, WIDTH), lambda i: (i, 0))] * 2
    return pl.pallas_call(
        functools.partial(_proj_kernel, rope=rope, emit_cache=emit_cache),
        grid=(nt,),
        in_specs=in_specs,
        out_specs=out_specs,
        out_shape=out_shape,
        compiler_params=_params(("parallel",)),
        name="proj_rope" if rope else "proj_ctx",
    )(*args)


def _rope_tables(seq):
    t = jnp.arange(seq)
    row = (t // GRID_W).astype(F32)[:, None]
    col = (t % GRID_W).astype(F32)[:, None]
    lane = jnp.arange(LANES)
    freqs = ROPE_THETA ** (-(lane % 16).astype(F32) / 16.0)
    pos = jnp.where(((lane % 64) < 32)[None, :], row, col)
    ang = pos * freqs[None, :]
    cos, sin = jnp.cos(ang), jnp.sin(ang)
    first = ((lane % 32) < 16)[None, :]
    return cos, jnp.where(first, -sin, 0.0), jnp.where(first, 0.0, sin)


def _mlstm_kernel(q_ref, k_ref, v_ref, g_ref, gt_ref, c0_ref, n0_ref, m0_ref,
                  hf_ref, hb_ref, cf_ref, nf_ref, mf_ref, *, nc, hp):
    L = CHUNK
    cf_ref[...] = c0_ref[...]
    nf_ref[...] = n0_ref[...]
    mf_ref[...] = m0_ref[...]
    rr = lax.broadcasted_iota(jnp.int32, (L, L), 0)
    cc = lax.broadcasted_iota(jnp.int32, (L, L), 1)
    lower = cc <= rr
    upper = cc >= rr
    lower_f = lower.astype(F32)
    upper_f = upper.astype(F32)

    def body(c, carry):
        for d in range(2):
            ci = c if d == 0 else nc - 1 - c
            off = pl.multiple_of(ci * L, L)
            mask = lower if d == 0 else upper
            tri = lower_f if d == 0 else upper_f
            tri_t = upper_f if d == 0 else lower_f
            g = g_ref[0, 0, pl.ds(off, L), :]
            gt = gt_ref[0, 0, :, pl.ds(off, L)]
            bsum = jnp.dot(tri, _log_sigmoid(g), preferred_element_type=F32, precision=HIGHEST)
            bsum_t = jnp.dot(_log_sigmoid(gt), tri_t, preferred_element_type=F32, precision=HIGHEST)
            last = L - 1 if d == 0 else 0
            for hh in range(hp):
                col_i = (2 * d) * hp + hh
                col_f = (2 * d + 1) * hp + hh
                i_col = g[:, col_i:col_i + 1]
                b_col = bsum[:, col_f:col_f + 1]
                i_row = gt[col_i:col_i + 1, :]
                b_row = bsum_t[col_f:col_f + 1, :]
                b_last = b_row[:, last:last + 1]
                lanes = slice(hh * HEAD_DIM, (hh + 1) * HEAD_DIM)
                q = q_ref[0, pl.ds(off, L), lanes]
                k = k_ref[0, pl.ds(off, L), lanes]
                v = v_ref[0, pl.ds(off, L), lanes]
                cmat = cf_ref[0, d, hh]
                n = nf_ref[0, d, hh]
                m = mf_ref[0, d, hh][:, 0:1]

                dmat = jnp.where(mask, b_col - b_row + i_row, -jnp.inf)
                inter = b_col + m
                m_t = jnp.maximum(inter, jnp.max(dmat, axis=-1, keepdims=True))
                w_intra = jnp.exp(dmat - m_t)
                w_state = jnp.exp(inter - m_t)
                s = lax.dot_general(q, k, (((1,), (1,)), ((), ())), preferred_element_type=F32) * w_intra
                qc = lax.dot_general(q, cmat.astype(BF16), (((1,), (1,)), ((), ())),
                                     preferred_element_type=F32)
                num = jnp.dot(s.astype(BF16), v, preferred_element_type=F32) + w_state * qc
                qn = jnp.sum(q.astype(F32) * n, axis=-1, keepdims=True)
                den = jnp.sum(s, axis=-1, keepdims=True) + w_state * qn
                h = num / jnp.maximum(jnp.abs(den), jnp.exp(-m_t))

                g_col = b_last - b_col + i_col
                g_row = b_last - b_row + i_row
                m_new = jnp.maximum(b_last + m, jnp.max(g_row, axis=-1, keepdims=True))
                w_k = jnp.exp(g_col - m_new)
                decay = jnp.exp(b_last + m - m_new)
                vw = (v.astype(F32) * w_k).astype(BF16)
                c_new = decay * cmat + lax.dot_general(vw, k, (((0,), (0,)), ((), ())),
                                                       preferred_element_type=F32)
                n_new = decay * n + jnp.sum(k.astype(F32) * w_k, axis=0, keepdims=True)

                cf_ref[0, d, hh] = c_new
                nf_ref[0, d, hh] = n_new
                mf_ref[0, d, hh] = jnp.broadcast_to(m_new, (1, HEAD_DIM))
                out_ref = hf_ref if d == 0 else hb_ref
                out_ref[0, pl.ds(off, L), lanes] = h.astype(BF16)
        return carry

    lax.fori_loop(0, nc, body, 0)


def _mlstm(qkvo, gates, c0, n0, m0, hp):
    b, s, _ = qkvo.shape
    nc = s // CHUNK
    hg = N_HEADS // hp
    gw = 4 * hp
    g5 = gates.reshape(b, s, 4, hg, hp).transpose(0, 3, 1, 2, 4).reshape(b, hg, s, gw)
    g5t = g5.transpose(0, 1, 3, 2)
    n0r = n0.reshape(b, 2, N_HEADS, 1, HEAD_DIM)
    m0r = jnp.broadcast_to(m0[..., None, None], (b, 2, N_HEADS, 1, HEAD_DIM))
    blk = hp * HEAD_DIM
    nblk = WIDTH // blk
    in_specs = [pl.BlockSpec((1, s, blk), lambda i, j: (i, 0, j)),
                pl.BlockSpec((1, s, blk), lambda i, j: (i, 0, nblk + j)),
                pl.BlockSpec((1, s, blk), lambda i, j: (i, 0, 2 * nblk + j)),
                pl.BlockSpec((1, 1, s, gw), lambda i, j: (i, j, 0, 0)),
                pl.BlockSpec((1, 1, gw, s), lambda i, j: (i, j, 0, 0)),
                pl.BlockSpec((1, 2, hp, HEAD_DIM, HEAD_DIM), lambda i, j: (i, 0, j, 0, 0)),
                pl.BlockSpec((1, 2, hp, 1, HEAD_DIM), lambda i, j: (i, 0, j, 0, 0)),
                pl.BlockSpec((1, 2, hp, 1, HEAD_DIM), lambda i, j: (i, 0, j, 0, 0))]
    out_specs = [pl.BlockSpec((1, s, blk), lambda i, j: (i, 0, j)),
                 pl.BlockSpec((1, s, blk), lambda i, j: (i, 0, j)),
                 pl.BlockSpec((1, 2, hp, HEAD_DIM, HEAD_DIM), lambda i, j: (i, 0, j, 0, 0)),
                 pl.BlockSpec((1, 2, hp, 1, HEAD_DIM), lambda i, j: (i, 0, j, 0, 0)),
                 pl.BlockSpec((1, 2, hp, 1, HEAD_DIM), lambda i, j: (i, 0, j, 0, 0))]
    out_shape = [jax.ShapeDtypeStruct((b, s, WIDTH), BF16),
                 jax.ShapeDtypeStruct((b, s, WIDTH), BF16),
                 jax.ShapeDtypeStruct((b, 2, N_HEADS, HEAD_DIM, HEAD_DIM), F32),
                 jax.ShapeDtypeStruct((b, 2, N_HEADS, 1, HEAD_DIM), F32),
                 jax.ShapeDtypeStruct((b, 2, N_HEADS, 1, HEAD_DIM), F32)]
    hf, hb, cf, nf, mf = pl.pallas_call(
        functools.partial(_mlstm_kernel, nc=nc, hp=hp),
        grid=(b, hg),
        in_specs=in_specs,
        out_specs=out_specs,
        out_shape=out_shape,
        compiler_params=_params(("parallel", "parallel")),
        name="mlstm",
    )(qkvo, qkvo, qkvo, g5, g5t, c0, n0r, m0r)
    return hf, hb, cf, nf[:, :, :, 0, :], mf[:, :, :, 0, 0]


def _attn_kernel(lp_ref, q_ref, k_ref, v_ref, o_ref, *, chunks, tq, lam_init):
    lp = lp_ref[...]
    lam = (jnp.exp(jnp.sum(lp[0:1] * lp[1:2], axis=-1, keepdims=True))
           - jnp.exp(jnp.sum(lp[2:3] * lp[3:4], axis=-1, keepdims=True)) + lam_init)
    q = q_ref[0]
    lane = lax.broadcasted_iota(jnp.int32, q.shape, 1)
    zero = jnp.zeros_like(q)
    qq = jnp.concatenate([jnp.where(lane < QK_DIM, q, zero), jnp.where(lane >= QK_DIM, q, zero)], axis=0)
    m = jnp.full((2 * tq, 1), -jnp.inf, F32)
    l = jnp.zeros((2 * tq, 1), F32)
    acc = jnp.zeros((2 * tq, HEAD_DIM), F32)
    for c0, c1 in chunks:
        kc = k_ref[0, c0:c1, :]
        vc = v_ref[0, c0:c1, :]
        s = lax.dot_general(qq, kc, (((1,), (1,)), ((), ())), preferred_element_type=F32)
        m_new = jnp.maximum(m, jnp.max(s, axis=-1, keepdims=True))
        alpha = jnp.exp(m - m_new)
        e = jnp.exp(s - m_new)
        l = alpha * l + jnp.sum(e, axis=-1, keepdims=True)
        acc = alpha * acc + jnp.dot(e.astype(BF16), vc, preferred_element_type=F32)
        m = m_new
    o = acc / l
    o_ref[0] = (o[:tq] - lam * o[tq:]).astype(BF16)


def _attn(lp, q, k, v, lam_init, tq, tk):
    b, sq, _ = q.shape
    t = k.shape[1]
    bounds = list(range(0, t, tk)) + [t]
    chunks = tuple(zip(bounds[:-1], bounds[1:]))
    return pl.pallas_call(
        functools.partial(_attn_kernel, chunks=chunks, tq=tq, lam_init=lam_init),
        grid=(b, N_HEADS, sq // tq),
        in_specs=[pl.BlockSpec(lp.shape, lambda i, h, j: (0, 0)),
                  pl.BlockSpec((1, tq, HEAD_DIM), lambda i, h, j: (i, j, h)),
                  pl.BlockSpec((1, t, HEAD_DIM), lambda i, h, j: (i, 0, h)),
                  pl.BlockSpec((1, t, HEAD_DIM), lambda i, h, j: (i, 0, h))],
        out_specs=pl.BlockSpec((1, tq, HEAD_DIM), lambda i, h, j: (i, j, h)),
        out_shape=jax.ShapeDtypeStruct((b, sq, WIDTH), BF16),
        compiler_params=_params(("parallel", "parallel", "parallel")),
        name="diff_attn",
    )(lp, q, k, v)


def _head_norm(x):
    outs = []
    for j in range(N_HEADS):
        xb = x[:, j * HEAD_DIM:(j + 1) * HEAD_DIM]
        outs.append(xb * lax.rsqrt(jnp.mean(xb * xb, axis=-1, keepdims=True) + EPS))
    return jnp.concatenate(outs, axis=1)


def _mix_kernel(hf_ref, hb_ref, om_ref, hd_ref, x_ref, g1_ref, sc2_ref, sh2_ref, gm_ref, gd_ref,
                wout_ref, nrm_ref, wr_ref, br_ref, x1_ref, h2_ref, chl_ref, grpt_ref, *, diff_scale):
    hm = _head_norm(hf_ref[...].astype(F32) + hb_ref[...].astype(F32))
    hm = hm * gm_ref[...] * _sigmoid(om_ref[...].astype(F32))
    hd = _head_norm(hd_ref[...].astype(F32)) * gd_ref[...] * diff_scale
    mix = (jnp.dot(hm.astype(BF16), wout_ref[:WIDTH, :], preferred_element_type=F32)
           + jnp.dot(hd.astype(BF16), wout_ref[WIDTH:, :], preferred_element_type=F32))
    x1 = x_ref[...] + g1_ref[0] * mix
    x1_ref[...] = x1
    ms = jnp.mean(x1 * x1, axis=-1, keepdims=True)
    h2 = (x1 * lax.rsqrt(ms + EPS) * nrm_ref[...]) * (1.0 + sc2_ref[0]) + sh2_ref[0]
    h2_ref[...] = h2.astype(BF16)

    logits = jnp.dot(h2, wr_ref[...], preferred_element_type=F32, precision=HIGHEST) + br_ref[...]
    lane = lax.broadcasted_iota(jnp.int32, logits.shape, 1).astype(F32)
    big = float(LANES)
    is_grp = (lane >= N_EXPERTS) & (lane < N_EXPERTS + N_GROUPS)
    lg = jnp.where(is_grp, logits, -jnp.inf)
    mx = jnp.max(lg, axis=-1, keepdims=True)
    p_grp = 1.0 / jnp.sum(jnp.exp(lg - mx), axis=-1, keepdims=True)
    grp = jnp.min(jnp.where(lg == mx, lane, big), axis=-1, keepdims=True) - N_EXPERTS
    base = grp * EXPERTS_PER_GROUP
    sel = (lane >= base) & (lane < base + EXPERTS_PER_GROUP)
    le = jnp.where(sel, logits, -jnp.inf)
    ee = jnp.exp(le - jnp.max(le, axis=-1, keepdims=True))
    pe = ee / jnp.sum(ee, axis=-1, keepdims=True)
    top1 = jnp.max(pe, axis=-1, keepdims=True)
    idx1 = jnp.min(jnp.where(sel & (pe == top1), lane, big), axis=-1, keepdims=True)
    rest = sel & (lane != idx1)
    top2 = jnp.max(jnp.where(rest, pe, -1.0), axis=-1, keepdims=True)
    idx2 = jnp.min(jnp.where(rest & (pe == top2), lane, big), axis=-1, keepdims=True)
    denom = top1 + top2
    w1 = top1 / denom * p_grp
    w2 = top2 / denom * p_grp
    chl = jnp.zeros_like(logits)
    for j in range(EXPERTS_PER_GROUP):
        cj = jnp.where(idx1 == base + j, w1, 0.0) + jnp.where(idx2 == base + j, w2, 0.0)
        hi = cj.astype(BF16).astype(F32)
        chl = chl + jnp.where(lane == j, hi, 0.0) + jnp.where(lane == EXPERTS_PER_GROUP + j, cj - hi, 0.0)
    chl_ref[...] = chl.astype(BF16)
    grpt_ref[...] = jnp.transpose(jnp.broadcast_to(grp, logits.shape))[:8, :]


def _mix(hf, hb, qkvo, hd, x2d, g1, sc2, sh2, row0, tiles_per_row, gm, gd, wout, nrm, wr, br, diff_scale, tm):
    t, d = x2d.shape
    row_map = lambda i: (row0 + i // tiles_per_row, 0, 0)
    tok = lambda i: (i, 0)
    full = lambda i: (0, 0)
    return pl.pallas_call(
        functools.partial(_mix_kernel, diff_scale=diff_scale),
        grid=(t // tm,),
        in_specs=[pl.BlockSpec((tm, WIDTH), tok),
                  pl.BlockSpec((tm, WIDTH), tok),
                  pl.BlockSpec((tm, WIDTH), lambda i: (i, 3)),
                  pl.BlockSpec((tm, WIDTH), tok),
                  pl.BlockSpec((tm, d), tok),
                  pl.BlockSpec((1, 1, d), row_map),
                  pl.BlockSpec((1, 1, d), row_map),
                  pl.BlockSpec((1, 1, d), row_map),
                  pl.BlockSpec((1, WIDTH), full),
                  pl.BlockSpec((1, WIDTH), full),
                  pl.BlockSpec(wout.shape, full),
                  pl.BlockSpec((1, d), full),
                  pl.BlockSpec(wr.shape, full),
                  pl.BlockSpec(br.shape, full)],
        out_specs=[pl.BlockSpec((tm, d), tok),
                   pl.BlockSpec((tm, d), tok),
                   pl.BlockSpec((tm, LANES), tok),
                   pl.BlockSpec((8, tm), lambda i: (0, i))],
        out_shape=[jax.ShapeDtypeStruct((t, d), F32),
                   jax.ShapeDtypeStruct((t, d), BF16),
                   jax.ShapeDtypeStruct((t, LANES), BF16),
                   jax.ShapeDtypeStruct((8, t), F32)],
        compiler_params=_params(("parallel",)),
        name="mix_router",
    )(hf, hb, qkvo, hd, x2d, g1, sc2, sh2, gm, gd, wout, nrm, wr, br)


def _moe_kernel(h2_ref, chl_ref, grpt_ref, wgu_ref, wdn_ref, x1_ref, g2_ref, nrm_ref, y_ref,
                acc_ref, rank_ref, *, hidden, blk):
    g = pl.program_id(1)
    tm = h2_ref.shape[0]

    @pl.when(g == 0)
    def _():
        acc_ref[...] = jnp.zeros_like(acc_ref)
        gid = lax.broadcasted_iota(jnp.int32, (8, tm), 0).astype(F32)
        member = grpt_ref[...] == gid
        src = lax.broadcasted_iota(jnp.int32, (tm, tm), 0)
        dst = lax.broadcasted_iota(jnp.int32, (tm, tm), 1)
        before = jnp.where(src < dst, 1.0, 0.0).astype(BF16)
        rank = jnp.dot(jnp.where(member, 1.0, 0.0).astype(BF16), before, preferred_element_type=F32)
        rank_ref[...] = jnp.where(member, rank, -1.0)

    rank_g = rank_ref[pl.ds(g, 1), :]
    n_g = jnp.max(rank_g) + 1.0

    for j in range(tm // blk):
        @pl.when(n_g > j * blk)
        def _():
            slot = lax.broadcasted_iota(jnp.int32, (blk, tm), 0).astype(F32) + float(j * blk)
            pick = jnp.where(rank_g == slot, 1.0, 0.0).astype(BF16)
            xg = jnp.dot(pick, h2_ref[...], preferred_element_type=F32).astype(BF16)
            cg = jnp.dot(pick, chl_ref[...], preferred_element_type=F32)
            out = jnp.zeros((blk, acc_ref.shape[1]), F32)
            for e in range(EXPERTS_PER_GROUP):
                gu = jnp.dot(xg, wgu_ref[e], preferred_element_type=F32)
                a, u = gu[:, :hidden], gu[:, hidden:]
                ce = cg[:, e:e + 1] + cg[:, EXPERTS_PER_GROUP + e:EXPERTS_PER_GROUP + e + 1]
                act = (a * _sigmoid(a) * u * ce).astype(BF16)
                out = out + jnp.dot(act, wdn_ref[e], preferred_element_type=F32)
            acc_ref[...] += lax.dot_general(pick, out.astype(BF16), (((0,), (0,)), ((), ())),
                                            preferred_element_type=F32)

    @pl.when(g == pl.num_programs(1) - 1)
    def _():
        x2 = x1_ref[...] + g2_ref[0] * acc_ref[...]
        ms = jnp.mean(x2 * x2, axis=-1, keepdims=True)
        y_ref[...] = x2 * lax.rsqrt(ms + EPS) * nrm_ref[...]


def _moe(h2, chl, grpt, wgu, wdn, x1, g2, row0, tiles_per_row, nrm, tm, blk):
    t, d = x1.shape
    _, _, two_h = wgu.shape
    hidden = two_h // 2
    row_map = lambda i, g: (row0 + i // tiles_per_row, 0, 0)
    tok = lambda i, g: (i, 0)
    return pl.pallas_call(
        functools.partial(_moe_kernel, hidden=hidden, blk=blk),
        grid=(t // tm, N_GROUPS),
        in_specs=[pl.BlockSpec((tm, d), tok),
                  pl.BlockSpec((tm, LANES), tok),
                  pl.BlockSpec((8, tm), lambda i, g: (0, i)),
                  pl.BlockSpec((EXPERTS_PER_GROUP, d, two_h), lambda i, g: (g, 0, 0)),
                  pl.BlockSpec((EXPERTS_PER_GROUP, hidden, d), lambda i, g: (g, 0, 0)),
                  pl.BlockSpec((tm, d), tok),
                  pl.BlockSpec((1, 1, d), row_map),
                  pl.BlockSpec((1, d), lambda i, g: (0, 0))],
        out_specs=pl.BlockSpec((tm, d), tok),
        out_shape=jax.ShapeDtypeStruct((t, d), F32),
        scratch_shapes=[pltpu.VMEM((tm, d), F32), pltpu.VMEM((8, tm), F32)],
        compiler_params=_params(("parallel", "arbitrary")),
        name="moe_experts",
    )(h2, chl, grpt, wgu, wdn, x1, g2, nrm)


def _layer(x, mods, row0, per_request, weights, states, cache, rope_tabs, emit_cache, tm, tq, tk, hp):
    b, s, d = x.shape
    sh1, sc1, g1, sh2, sc2, g2 = mods
    (nrm_mix, nrm_ffn, w_in_r, b_gate, gm, gd, lp, wout, wr, br, wgu, wdn, nrm_final, lam_init) = weights
    x2d = x.reshape(b * s, d)
    tiles_per_row = (s // tm) if per_request else (b * s // tm)
    outs = _proj(x2d, nrm_mix, sc1, sh1, row0, tiles_per_row, w_in_r, b_gate, rope_tabs, emit_cache, tm)
    qkvo, qd, kd, vd, gates = outs[:5]
    c0, n0, m0 = states
    hf, hb, cf, nf, mf = _mlstm(qkvo.reshape(b, s, 4 * WIDTH), gates.reshape(b, s, N_GATES), c0, n0, m0, hp)
    qd = qd.reshape(b, s, WIDTH)
    kd = kd.reshape(b, s, WIDTH)
    vd = vd.reshape(b, s, WIDTH)
    if cache is not None:
        ck, cv = cache
        kd = jnp.concatenate([kd, ck.reshape(b, -1, WIDTH).astype(BF16)], axis=1)
        vd = jnp.concatenate([vd, cv.reshape(b, -1, WIDTH).astype(BF16)], axis=1)
    hd = _attn(lp, qd, kd, vd, lam_init, tq, tk)
    x1, h2, chl, grpt = _mix(hf.reshape(b * s, WIDTH), hb.reshape(b * s, WIDTH), qkvo, hd.reshape(b * s, WIDTH),
                             x2d, g1, sc2, sh2, row0, tiles_per_row, gm, gd, wout, nrm_ffn, wr, br,
                             1.0 - lam_init, tm)
    tm_moe = 2 * tm
    y = _moe(h2, chl, grpt, wgu, wdn, x1, g2, row0, tiles_per_row // 2 if per_request else b * s // tm_moe,
             nrm_final, tm_moe, MOE_BLOCK)
    new_kv = outs[5:] if emit_cache else None
    return y.reshape(b, s, d), new_kv, (cf, nf, mf)


def kernel(x_prompt, x_sample, cache_k, cache_v, state_C, state_n, state_m, c, c_ctx, w_ada, b_ada, norm_mix, norm_ffn, w_in, b_gate, mlstm_norm, diff_norm, diff_lambda, w_out, w_route_group, b_route_group, w_route_expert, b_route_expert, w_gate_up, w_down, final_norm):
    depth = w_in.shape[0]
    assert depth == 1, "the final norm is fused into the last layer's expert kernel"
    bp, sp, d = x_prompt.shape
    bs, ss, _ = x_sample.shape
    rows = 16
    assert bs < rows
    cvec = jnp.zeros((rows, d), F32).at[:bs].set(c).at[bs].set(c_ctx)
    rope_tabs = _rope_tables(ss)
    xp, xs = x_prompt, x_sample
    new_k, new_v, new_c, new_n, new_m = [], [], [], [], []
    for l in range(depth):
        lam_init = 0.8 - 0.6 * math.exp(-0.3 * l)
        mod = _ada(cvec, w_ada[l], b_ada[l])
        mods = [mod[:, i * d:(i + 1) * d].reshape(rows, 1, d) for i in range(6)]
        wl = w_in[l]
        w_in_r = jnp.concatenate(
            [wl[:, :4 * WIDTH], wl[:, 4 * WIDTH + N_GATES:], wl[:, 4 * WIDTH:4 * WIDTH + N_GATES],
             jnp.zeros((d, LANES - N_GATES), F32)], axis=1).astype(BF16)
        wr = jnp.concatenate([w_route_expert[l], w_route_group[l],
                              jnp.zeros((d, LANES - N_EXPERTS - N_GROUPS), F32)], axis=1)
        br = jnp.concatenate([b_route_expert[l], b_route_group[l],
                              jnp.zeros((LANES - N_EXPERTS - N_GROUPS,), F32)]).reshape(1, LANES)
        weights = (norm_mix[l].reshape(1, d), norm_ffn[l].reshape(1, d), w_in_r, b_gate[l].reshape(1, N_GATES),
                   mlstm_norm[l].reshape(1, WIDTH), diff_norm[l].reshape(1, WIDTH), diff_lambda[l],
                   w_out[l].astype(BF16), wr, br, w_gate_up[l].astype(BF16), w_down[l].astype(BF16),
                   final_norm.reshape(1, d), lam_init)

        zero_states = (jnp.zeros((bp, 2, N_HEADS, HEAD_DIM, HEAD_DIM), F32),
                       jnp.zeros((bp, 2, N_HEADS, HEAD_DIM), F32),
                       jnp.zeros((bp, 2, N_HEADS), F32))
        xp, (nk, nv), (cf, nf, mf) = _layer(xp, mods, bs, False, weights, zero_states, None, None, True,
                                            tm=512, tq=256, tk=256, hp=4)
        new_k.append(nk.reshape(bp, sp, N_HEADS, 2, QK_DIM))
        new_v.append(nv.reshape(bp, sp, N_HEADS, HEAD_DIM))
        new_c.append(cf)
        new_n.append(nf)
        new_m.append(mf)

        states = (state_C[:, l], state_n[:, l], state_m[:, l])
        xs, _, _ = _layer(xs, mods, 0, True, weights, states, (cache_k[:, l], cache_v[:, l]), rope_tabs, False,
                          tm=512, tq=256, tk=1024, hp=4)

    return (xp, xs, jnp.stack(new_k, axis=1), jnp.stack(new_v, axis=1),
            jnp.stack(new_c, axis=1), jnp.stack(new_n, axis=1), jnp.stack(new_m, axis=1))
```

```python
import functools
import math

import jax
import jax.numpy as jnp
from jax import lax
from jax.experimental import pallas as pl
from jax.experimental.pallas import tpu as pltpu

F32 = jnp.float32
BF16 = jnp.bfloat16
HIGHEST = lax.Precision.HIGHEST

EPS = 1e-6
GRID_W = 64
ROPE_THETA = 10000.0
N_HEADS = 4
HEAD_DIM = 128
QK_DIM = 64
WIDTH = N_HEADS * HEAD_DIM
CHUNK = 128
N_GATES = 4 * N_HEADS
N_GROUPS = 4
EXPERTS_PER_GROUP = 4
N_EXPERTS = 16
LANES = 128
MOE_BLOCK = 256
BF16_ROWS = 16
ATTN_SUB = 128
LOG2E = math.log2(math.e)
VMEM_LIMIT = 56 * 1024 * 1024


def _params(sem):
    return pltpu.CompilerParams(dimension_semantics=sem, vmem_limit_bytes=VMEM_LIMIT)


def _log_sigmoid(x):
    return jnp.minimum(x, 0.0) - jnp.log1p(jnp.exp(-jnp.abs(x)))


def _sigmoid(x):
    return 1.0 / (1.0 + jnp.exp(-x))


def _ada_kernel(c_ref, w_ref, b_ref, o_ref):
    c = c_ref[...]
    s = c * _sigmoid(c)
    o_ref[...] = jnp.dot(s, w_ref[...], preferred_element_type=F32, precision=HIGHEST) + b_ref[...]


def _ada(cvec, w_ada, b_ada):
    rows, d = cvec.shape
    n = w_ada.shape[1]
    tn = 1024
    return pl.pallas_call(
        _ada_kernel,
        grid=(n // tn,),
        in_specs=[pl.BlockSpec((rows, d), lambda j: (0, 0)),
                  pl.BlockSpec((d, tn), lambda j: (0, j)),
                  pl.BlockSpec((1, tn), lambda j: (0, j))],
        out_specs=pl.BlockSpec((rows, tn), lambda j: (0, j)),
        out_shape=jax.ShapeDtypeStruct((rows, n), F32),
        compiler_params=_params(("arbitrary",)),
        name="ada",
    )(cvec, w_ada, b_ada.reshape(1, n))


def _proj_kernel(*refs, rope, emit_cache):
    x_ref, nrm_ref, sc_ref, sh_ref, w_ref, bg_ref = refs[:6]
    refs = refs[6:]
    if rope:
        cos_ref, sa_ref, sb_ref = refs[:3]
        refs = refs[3:]
    m_ref, q_ref, k_ref, v_ref, g_ref = refs[:5]
    refs = refs[5:]
    if emit_cache:
        newk_ref, newv_ref = refs

    x = x_ref[...]
    ms = jnp.mean(x * x, axis=-1, keepdims=True)
    y = x * lax.rsqrt(ms + EPS) * nrm_ref[...]
    h = (y * (1.0 + sc_ref[0]) + sh_ref[0]).astype(BF16)

    def section(j):
        return jnp.dot(h, w_ref[:, j * WIDTH:(j + 1) * WIDTH], preferred_element_type=F32)

    def rotate(p):
        cos, sa, sb = cos_ref[...], sa_ref[...], sb_ref[...]
        outs = []
        for j in range(N_HEADS):
            xb = p[:, j * LANES:(j + 1) * LANES]
            outs.append(xb * cos + pltpu.roll(xb, LANES - 16, 1) * sa + pltpu.roll(xb, 16, 1) * sb)
        return jnp.concatenate(outs, axis=1)

    for j in range(4):
        p = section(j)
        if j == 1:
            p = p * (HEAD_DIM ** -0.5)
        m_ref[:, j * WIDTH:(j + 1) * WIDTH] = p.astype(BF16)
    pq = section(4)
    if rope:
        pq = rotate(pq)
    q_ref[...] = (pq * (LOG2E * QK_DIM ** -0.5)).astype(BF16)
    pk = section(5)
    if emit_cache:
        newk_ref[...] = pk
    if rope:
        pk = rotate(pk)
    k_ref[...] = pk.astype(BF16)
    pv = section(6)
    if emit_cache:
        newv_ref[...] = pv
    v_ref[...] = pv.astype(BF16)
    pg = jnp.dot(h, w_ref[:, 7 * WIDTH:7 * WIDTH + LANES], preferred_element_type=F32)
    g_ref[...] = pg[:, :N_GATES] + bg_ref[...]


def _proj(x2d, nrm, sc, sh, row0, tiles_per_row, w_r, b_gate, rope_tabs, emit_cache, tm):
    t, d = x2d.shape
    nt = t // tm
    rope = rope_tabs is not None
    row_map = lambda i: (row0 + i // tiles_per_row, 0, 0)
    in_specs = [pl.BlockSpec((tm, d), lambda i: (i, 0)),
                pl.BlockSpec((1, d), lambda i: (0, 0)),
                pl.BlockSpec((1, 1, d), row_map),
                pl.BlockSpec((1, 1, d), row_map),
                pl.BlockSpec(w_r.shape, lambda i: (0, 0)),
                pl.BlockSpec((1, N_GATES), lambda i: (0, 0))]
    args = [x2d, nrm, sc, sh, w_r, b_gate]
    if rope:
        tps = rope_tabs[0].shape[0] // tm
        for tab in rope_tabs:
            in_specs.append(pl.BlockSpec((tm, LANES), lambda i: (i % tps, 0)))
            args.append(tab)
    out_shape = [jax.ShapeDtypeStruct((t, 4 * WIDTH), BF16),
                 jax.ShapeDtypeStruct((t, WIDTH), BF16),
                 jax.ShapeDtypeStruct((t, WIDTH), BF16),
                 jax.ShapeDtypeStruct((t, WIDTH), BF16),
                 jax.ShapeDtypeStruct((t, N_GATES), F32)]
    out_specs = [pl.BlockSpec((tm, 4 * WIDTH), lambda i: (i, 0)),
                 pl.BlockSpec((tm, WIDTH), lambda i: (i, 0)),
                 pl.BlockSpec((tm, WIDTH), lambda i: (i, 0)),
                 pl.BlockSpec((tm, WIDTH), lambda i: (i, 0)),
                 pl.BlockSpec((tm, N_GATES), lambda i: (i, 0))]
    if emit_cache:
        out_shape += [jax.ShapeDtypeStruct((t, WIDTH), F32)] * 2
        out_specs += [pl.BlockSpec((tm, WIDTH), lambda i: (i, 0))] * 2
    return pl.pallas_call(
        functools.partial(_proj_kernel, rope=rope, emit_cache=emit_cache),
        grid=(nt,),
        in_specs=in_specs,
        out_specs=out_specs,
        out_shape=out_shape,
        compiler_params=_params(("parallel",)),
        name="proj_rope" if rope else "proj_ctx",
    )(*args)


def _rope_tables(seq):
    t = jnp.arange(seq)
    row = (t // GRID_W).astype(F32)[:, None]
    col = (t % GRID_W).astype(F32)[:, None]
    lane = jnp.arange(LANES)
    freqs = ROPE_THETA ** (-(lane % 16).astype(F32) / 16.0)
    pos = jnp.where(((lane % 64) < 32)[None, :], row, col)
    ang = pos * freqs[None, :]
    cos, sin = jnp.cos(ang), jnp.sin(ang)
    first = ((lane % 32) < 16)[None, :]
    return cos, jnp.where(first, -sin, 0.0), jnp.where(first, 0.0, sin)


def _mlstm_kernel_v1(q_ref, k_ref, v_ref, g_ref, gt_ref, c0_ref, n0_ref, m0_ref,
                     hf_ref, hb_ref, cf_ref, nf_ref, mf_ref, *, nc, hp):
    L = CHUNK
    cf_ref[...] = c0_ref[...]
    nf_ref[...] = n0_ref[...]
    mf_ref[...] = m0_ref[...]
    rr = lax.broadcasted_iota(jnp.int32, (L, L), 0)
    cc = lax.broadcasted_iota(jnp.int32, (L, L), 1)
    lower = cc <= rr
    upper = cc >= rr
    lower_f = lower.astype(F32)
    upper_f = upper.astype(F32)

    def body(c, carry):
        for d in range(2):
            ci = c if d == 0 else nc - 1 - c
            off = pl.multiple_of(ci * L, L)
            mask = lower if d == 0 else upper
            tri = lower_f if d == 0 else upper_f
            tri_t = upper_f if d == 0 else lower_f
            g = g_ref[0, 0, pl.ds(off, L), :]
            gt = gt_ref[0, 0, :, pl.ds(off, L)]
            bsum = jnp.dot(tri, _log_sigmoid(g), preferred_element_type=F32, precision=HIGHEST)
            bsum_t = jnp.dot(_log_sigmoid(gt), tri_t, preferred_element_type=F32, precision=HIGHEST)
            last = L - 1 if d == 0 else 0
            for hh in range(hp):
                col_i = (2 * d) * hp + hh
                col_f = (2 * d + 1) * hp + hh
                i_col = g[:, col_i:col_i + 1]
                b_col = bsum[:, col_f:col_f + 1]
                i_row = gt[col_i:col_i + 1, :]
                b_row = bsum_t[col_f:col_f + 1, :]
                b_last = b_row[:, last:last + 1]
                lanes = slice(hh * HEAD_DIM, (hh + 1) * HEAD_DIM)
                q = q_ref[0, pl.ds(off, L), lanes]
                k = k_ref[0, pl.ds(off, L), lanes]
                v = v_ref[0, pl.ds(off, L), lanes]
                cmat = cf_ref[0, d, hh]
                n = nf_ref[0, d, hh]
                m = mf_ref[0, d, hh][:, 0:1]

                dmat = jnp.where(mask, b_col - b_row + i_row, -jnp.inf)
                inter = b_col + m
                m_t = jnp.maximum(inter, jnp.max(dmat, axis=-1, keepdims=True))
                w_intra = jnp.exp(dmat - m_t)
                w_state = jnp.exp(inter - m_t)
                s = lax.dot_general(q, k, (((1,), (1,)), ((), ())), preferred_element_type=F32) * w_intra
                qc = lax.dot_general(q, cmat.astype(BF16), (((1,), (1,)), ((), ())),
                                     preferred_element_type=F32)
                num = jnp.dot(s.astype(BF16), v, preferred_element_type=F32) + w_state * qc
                qn = jnp.sum(q.astype(F32) * n, axis=-1, keepdims=True)
                den = jnp.sum(s, axis=-1, keepdims=True) + w_state * qn
                h = num / jnp.maximum(jnp.abs(den), jnp.exp(-m_t))

                g_col = b_last - b_col + i_col
                g_row = b_last - b_row + i_row
                m_new = jnp.maximum(b_last + m, jnp.max(g_row, axis=-1, keepdims=True))
                w_k = jnp.exp(g_col - m_new)
                decay = jnp.exp(b_last + m - m_new)
                vw = (v.astype(F32) * w_k).astype(BF16)
                c_new = decay * cmat + lax.dot_general(vw, k, (((0,), (0,)), ((), ())),
                                                       preferred_element_type=F32)
                n_new = decay * n + jnp.sum(k.astype(F32) * w_k, axis=0, keepdims=True)

                cf_ref[0, d, hh] = c_new
                nf_ref[0, d, hh] = n_new
                mf_ref[0, d, hh] = jnp.broadcast_to(m_new, (1, HEAD_DIM))
                out_ref = hf_ref if d == 0 else hb_ref
                out_ref[0, pl.ds(off, L), lanes] = h.astype(BF16)
        return carry

    lax.fori_loop(0, nc, body, 0)


def _mlstm_v1(qkvo, gates, c0, n0, m0, hp):
    b, s, _ = qkvo.shape
    nc = s // CHUNK
    hg = N_HEADS // hp
    gw = 4 * hp
    g5 = gates.reshape(b, s, 4, hg, hp).transpose(0, 3, 1, 2, 4).reshape(b, hg, s, gw)
    g5t = g5.transpose(0, 1, 3, 2)
    n0r = n0.reshape(b, 2, N_HEADS, 1, HEAD_DIM)
    m0r = jnp.broadcast_to(m0[..., None, None], (b, 2, N_HEADS, 1, HEAD_DIM))
    blk = hp * HEAD_DIM
    nblk = WIDTH // blk
    in_specs = [pl.BlockSpec((1, s, blk), lambda i, j: (i, 0, j)),
                pl.BlockSpec((1, s, blk), lambda i, j: (i, 0, nblk + j)),
                pl.BlockSpec((1, s, blk), lambda i, j: (i, 0, 2 * nblk + j)),
                pl.BlockSpec((1, 1, s, gw), lambda i, j: (i, j, 0, 0)),
                pl.BlockSpec((1, 1, gw, s), lambda i, j: (i, j, 0, 0)),
                pl.BlockSpec((1, 2, hp, HEAD_DIM, HEAD_DIM), lambda i, j: (i, 0, j, 0, 0)),
                pl.BlockSpec((1, 2, hp, 1, HEAD_DIM), lambda i, j: (i, 0, j, 0, 0)),
                pl.BlockSpec((1, 2, hp, 1, HEAD_DIM), lambda i, j: (i, 0, j, 0, 0))]
    out_specs = [pl.BlockSpec((1, s, blk), lambda i, j: (i, 0, j)),
                 pl.BlockSpec((1, s, blk), lambda i, j: (i, 0, j)),
                 pl.BlockSpec((1, 2, hp, HEAD_DIM, HEAD_DIM), lambda i, j: (i, 0, j, 0, 0)),
                 pl.BlockSpec((1, 2, hp, 1, HEAD_DIM), lambda i, j: (i, 0, j, 0, 0)),
                 pl.BlockSpec((1, 2, hp, 1, HEAD_DIM), lambda i, j: (i, 0, j, 0, 0))]
    out_shape = [jax.ShapeDtypeStruct((b, s, WIDTH), BF16),
                 jax.ShapeDtypeStruct((b, s, WIDTH), BF16),
                 jax.ShapeDtypeStruct((b, 2, N_HEADS, HEAD_DIM, HEAD_DIM), F32),
                 jax.ShapeDtypeStruct((b, 2, N_HEADS, 1, HEAD_DIM), F32),
                 jax.ShapeDtypeStruct((b, 2, N_HEADS, 1, HEAD_DIM), F32)]
    hf, hb, cf, nf, mf = pl.pallas_call(
        functools.partial(_mlstm_kernel_v1, nc=nc, hp=hp),
        grid=(b, hg),
        in_specs=in_specs,
        out_specs=out_specs,
        out_shape=out_shape,
        compiler_params=_params(("parallel", "parallel")),
        name="mlstm",
    )(qkvo, qkvo, qkvo, g5, g5t, c0, n0r, m0r)
    return hf, hb, cf, nf[:, :, :, 0, :], mf[:, :, :, 0, 0]


def _lane_scan(x, pos, op, forward):
    n = x.shape[-1]
    shift = 1
    while shift < n:
        if forward:
            y = pltpu.roll(x, shift, 1)
            ok = pos >= shift
        else:
            y = pltpu.roll(x, n - shift, 1)
            ok = pos < n - shift
        x = jnp.where(ok, op(x, y), x)
        shift *= 2
    return x


def _mlstm_kernel(q_ref, kt_ref, v_ref, g_ref, c0_ref, n0_ref, m0_ref,
                  hf_ref, hb_ref, cf_ref, nf_ref, mf_ref,
                  st_ref, r_ref, mp_ref, mn_ref, dc_ref, tab_ref, *, nc, hp):
    L = CHUNK
    rows = nc * hp
    pos = lax.broadcasted_iota(jnp.int32, (rows, L), 1)
    tab_ref[...] = jnp.zeros_like(tab_ref)
    for d in range(2):
        i_pre = g_ref[0, 0, (2 * d) * rows:(2 * d + 1) * rows, :]
        lf = _log_sigmoid(g_ref[0, 0, (2 * d + 1) * rows:(2 * d + 2) * rows, :])
        pre = _lane_scan(lf, pos, jnp.add, True)
        suf = _lane_scan(lf, pos, jnp.add, False)
        tot = pre + suf - lf
        b = pre if d == 0 else suf
        r = i_pre - b
        r_max = _lane_scan(r, pos, jnp.maximum, d == 0)
        gk = tot - b + i_pre
        g_max = jnp.maximum(_lane_scan(gk, pos, jnp.maximum, True), _lane_scan(gk, pos, jnp.maximum, False))
        m = m0_ref[0, d, :, 0, :]
        for c in (range(nc) if d == 0 else reversed(range(nc))):
            sl = slice(c * hp, (c + 1) * hp)
            mp_ref[d, sl, :] = m
            m = jnp.maximum(tot[sl] + m, g_max[sl])
            mn_ref[d, sl, :] = m
        mf_ref[0, d, :, 0, :] = m
        m_prev = mp_ref[d]
        m_new = mn_ref[d]
        m_row = jnp.maximum(m_prev, r_max)
        r_ref[d] = r
        dc_ref[d] = jnp.exp(tot + m_prev - m_new)
        w_k = jnp.exp(gk - m_new)
        m_t = b + m_row
        for c in range(nc):
            src = slice(c * hp, (c + 1) * hp)
            slot = c if d == 0 else nc - 1 - c
            for qi, val in enumerate((m_row, m_t, w_k)):
                base = (2 * qi + d) * hp
                tab_ref[slot, base:base + hp, :] = val[src]

    for d in range(2):
        for hh in range(hp):
            st_ref[d, hh, :, :HEAD_DIM] = jnp.transpose(c0_ref[0, d, hh])
            st_ref[d, hh, :, HEAD_DIM:] = jnp.transpose(jnp.broadcast_to(n0_ref[0, d, hh], (HEAD_DIM, HEAD_DIM)))

    rr = lax.broadcasted_iota(jnp.int32, (L, L), 0)
    cc = lax.broadcasted_iota(jnp.int32, (L, L), 1)
    masks = (cc <= rr, cc >= rr)
    ones = jnp.ones((L, HEAD_DIM), BF16)

    def body(c, carry):
        tab = jnp.transpose(tab_ref[c])
        for d in range(2):
            ci = c if d == 0 else nc - 1 - c
            off = pl.multiple_of(ci * L, L)
            for hh in range(hp):
                lanes = slice(hh * HEAD_DIM, (hh + 1) * HEAD_DIM)
                row = ci * hp + hh
                col = d * hp + hh
                m_col = jnp.broadcast_to(tab[:, col:col + 1], (L, HEAD_DIM))
                mt_col = jnp.broadcast_to(tab[:, 2 * hp + col:2 * hp + col + 1], (L, HEAD_DIM))
                wk_col = jnp.broadcast_to(tab[:, 4 * hp + col:4 * hp + col + 1], (L, HEAD_DIM))
                r_row = r_ref[d, pl.ds(row, 1), :]
                m_prev = mp_ref[d, pl.ds(row, 1), :]
                decay = dc_ref[d, pl.ds(row, 1), :]
                q = q_ref[0, pl.ds(off, L), lanes]
                v = v_ref[0, pl.ds(off, L), lanes]
                kt = kt_ref[0, lanes, pl.ds(off, L)]
                state = st_ref[d, hh]

                w_intra = jnp.exp(jnp.where(masks[d], r_row - m_col, -jnp.inf))
                s = (jnp.dot(q, kt, preferred_element_type=F32) * w_intra).astype(BF16)
                intra = jnp.dot(s, jnp.concatenate([v, ones], axis=1), preferred_element_type=F32)
                inter = jnp.dot(q, state.astype(BF16), preferred_element_type=F32)
                w_state = jnp.exp(m_prev - m_col)
                both = intra + jnp.concatenate([w_state, w_state], axis=1) * inter
                den = jnp.maximum(jnp.abs(both[:, HEAD_DIM:]), jnp.exp(-mt_col))
                h = both[:, :HEAD_DIM] / den

                vw = (v.astype(F32) * wk_col).astype(BF16)
                upd = jnp.dot(kt, jnp.concatenate([vw, wk_col.astype(BF16)], axis=1),
                              preferred_element_type=F32)
                st_ref[d, hh] = jnp.concatenate([decay, decay], axis=1) * state + upd
                out_ref = hf_ref if d == 0 else hb_ref
                out_ref[0, pl.ds(off, L), lanes] = h.astype(BF16)
        return carry

    lax.fori_loop(0, nc, body, 0)

    for d in range(2):
        for hh in range(hp):
            state = st_ref[d, hh]
            cf_ref[0, d, hh] = jnp.transpose(state[:, :HEAD_DIM])
            nf_ref[0, d, hh] = jnp.transpose(state[:, HEAD_DIM:])[0:1, :]


def _mlstm(qkvo, kt, gates, c0, n0, m0, hp):
    b, s, _ = qkvo.shape
    nc = s // CHUNK
    hg = N_HEADS // hp
    rows = nc * hp
    g6 = gates.reshape(b, nc, CHUNK, 4, hg, hp).transpose(0, 4, 3, 1, 5, 2).reshape(b, hg, 4 * rows, CHUNK)
    n0r = n0.reshape(b, 2, N_HEADS, 1, HEAD_DIM)
    m0r = jnp.broadcast_to(m0[..., None, None], (b, 2, N_HEADS, 1, HEAD_DIM))
    blk = hp * HEAD_DIM
    nblk = WIDTH // blk
    state_spec = lambda w: pl.BlockSpec((1, 2, hp, w, HEAD_DIM), lambda i, j: (i, 0, j, 0, 0))
    in_specs = [pl.BlockSpec((1, s, blk), lambda i, j: (i, 0, j)),
                pl.BlockSpec((1, blk, s), lambda i, j: (i, j, 0)),
                pl.BlockSpec((1, s, blk), lambda i, j: (i, 0, 2 * nblk + j)),
                pl.BlockSpec((1, 1, 4 * rows, CHUNK), lambda i, j: (i, j, 0, 0)),
                state_spec(HEAD_DIM), state_spec(1), state_spec(1)]
    out_specs = [pl.BlockSpec((1, s, blk), lambda i, j: (i, 0, j)),
                 pl.BlockSpec((1, s, blk), lambda i, j: (i, 0, j)),
                 state_spec(HEAD_DIM), state_spec(1), state_spec(1)]
    out_shape = [jax.ShapeDtypeStruct((b, s, WIDTH), BF16),
                 jax.ShapeDtypeStruct((b, s, WIDTH), BF16),
                 jax.ShapeDtypeStruct((b, 2, N_HEADS, HEAD_DIM, HEAD_DIM), F32),
                 jax.ShapeDtypeStruct((b, 2, N_HEADS, 1, HEAD_DIM), F32),
                 jax.ShapeDtypeStruct((b, 2, N_HEADS, 1, HEAD_DIM), F32)]
    scratch = [pltpu.VMEM((2, hp, HEAD_DIM, 2 * HEAD_DIM), F32),
               pltpu.VMEM((2, rows, CHUNK), F32),
               pltpu.VMEM((2, rows, CHUNK), F32),
               pltpu.VMEM((2, rows, CHUNK), F32),
               pltpu.VMEM((2, rows, CHUNK), F32),
               pltpu.VMEM((nc, LANES, CHUNK), F32)]
    hf, hb, cf, nf, mf = pl.pallas_call(
        functools.partial(_mlstm_kernel, nc=nc, hp=hp),
        grid=(b, hg),
        in_specs=in_specs,
        out_specs=out_specs,
        out_shape=out_shape,
        scratch_shapes=scratch,
        compiler_params=_params(("parallel", "parallel")),
        name="mlstm",
    )(qkvo, kt, qkvo, g6, c0, n0r, m0r)
    return hf, hb, cf, nf[:, :, :, 0, :], mf[:, :, :, 0, 0]


def _attn_kernel(lp_ref, q_ref, kt_ref, v_ref, o_ref, s_even_ref, s_odd_ref, *, tq, sub, rb, lam_init):
    lp = lp_ref[...]
    lam = (jnp.exp(jnp.sum(lp[0:1] * lp[1:2], axis=-1, keepdims=True))
           - jnp.exp(jnp.sum(lp[2:3] * lp[3:4], axis=-1, keepdims=True)) + lam_init)
    lane = lax.broadcasted_iota(jnp.int32, (sub, HEAD_DIM), 1)
    s_refs = (s_even_ref, s_odd_ref)
    n_sub = tq // sub

    def scores(st):
        q = q_ref[0, st * sub:(st + 1) * sub, :]
        zero = jnp.zeros_like(q)
        qq = jnp.concatenate([jnp.where(lane < QK_DIM, q, zero), jnp.where(lane >= QK_DIM, q, zero)], axis=0)
        s = jnp.dot(qq, kt_ref[0], preferred_element_type=F32)
        s_refs[st % 2][...] = s
        return jnp.max(s, axis=-1, keepdims=True)

    m_next = scores(0)
    for st in range(n_sub):
        m_all = m_next
        if st + 1 < n_sub:
            m_next = scores(st + 1)
        s_ref = s_refs[st % 2]
        a_rows, l_rows = [], []
        for r in range(sub // rb):
            s0 = s_ref[r * rb:(r + 1) * rb, :]
            s1 = s_ref[sub + r * rb:sub + (r + 1) * rb, :]
            e0 = jnp.exp2(s0 - m_all[r * rb:(r + 1) * rb])
            e1 = jnp.exp2(s1 - m_all[sub + r * rb:sub + (r + 1) * rb])
            l0 = jnp.sum(e0, axis=-1, keepdims=True)
            l1 = jnp.sum(e1, axis=-1, keepdims=True)
            ratio = (lam * l0 / l1).astype(BF16)
            a_rows.append(e0.astype(BF16) - ratio * e1.astype(BF16))
            l_rows.append(l0)
        a = jnp.concatenate(a_rows, axis=0)
        o = jnp.dot(a, v_ref[0], preferred_element_type=F32) / jnp.concatenate(l_rows, axis=0)
        o_ref[0, st * sub:(st + 1) * sub, :] = o.astype(BF16)


def _attn(lp, q, kt, v, lam_init, tq):
    b, sq, _ = q.shape
    t = v.shape[1]
    return pl.pallas_call(
        functools.partial(_attn_kernel, tq=tq, sub=ATTN_SUB, rb=BF16_ROWS, lam_init=lam_init),
        grid=(b, N_HEADS, sq // tq),
        scratch_shapes=[pltpu.VMEM((2 * ATTN_SUB, t), F32), pltpu.VMEM((2 * ATTN_SUB, t), F32)],
        in_specs=[pl.BlockSpec(lp.shape, lambda i, h, j: (0, 0)),
                  pl.BlockSpec((1, tq, HEAD_DIM), lambda i, h, j: (i, j, h)),
                  pl.BlockSpec((1, HEAD_DIM, t), lambda i, h, j: (i, h, 0)),
                  pl.BlockSpec((1, t, HEAD_DIM), lambda i, h, j: (i, 0, h))],
        out_specs=pl.BlockSpec((1, tq, HEAD_DIM), lambda i, h, j: (i, j, h)),
        out_shape=jax.ShapeDtypeStruct((b, sq, WIDTH), BF16),
        compiler_params=_params(("parallel", "parallel", "parallel")),
        name="diff_attn",
    )(lp, q, kt, v)


def _head_norm(x):
    outs = []
    for j in range(N_HEADS):
        xb = x[:, j * HEAD_DIM:(j + 1) * HEAD_DIM]
        outs.append(xb * lax.rsqrt(jnp.mean(xb * xb, axis=-1, keepdims=True) + EPS))
    return jnp.concatenate(outs, axis=1)


def _mix_kernel(hf_ref, hb_ref, om_ref, hd_ref, x_ref, g1_ref, sc2_ref, sh2_ref, gm_ref, gd_ref,
                wout_ref, nrm_ref, wr_ref, br_ref, x1_ref, h2_ref, chl_ref, grpt_ref, *, diff_scale):
    hm = _head_norm(hf_ref[...].astype(F32) + hb_ref[...].astype(F32))
    hm = hm * gm_ref[...] * _sigmoid(om_ref[...].astype(F32))
    hd = _head_norm(hd_ref[...].astype(F32)) * gd_ref[...] * diff_scale
    mix = (jnp.dot(hm.astype(BF16), wout_ref[:WIDTH, :], preferred_element_type=F32)
           + jnp.dot(hd.astype(BF16), wout_ref[WIDTH:, :], preferred_element_type=F32))
    x1 = x_ref[...] + g1_ref[0] * mix
    x1_ref[...] = x1
    ms = jnp.mean(x1 * x1, axis=-1, keepdims=True)
    h2 = (x1 * lax.rsqrt(ms + EPS) * nrm_ref[...]) * (1.0 + sc2_ref[0]) + sh2_ref[0]
    h2_ref[...] = h2.astype(BF16)

    logits = jnp.dot(h2, wr_ref[...], preferred_element_type=F32, precision=HIGHEST) + br_ref[...]
    lane = lax.broadcasted_iota(jnp.int32, logits.shape, 1).astype(F32)
    big = float(LANES)
    is_grp = (lane >= N_EXPERTS) & (lane < N_EXPERTS + N_GROUPS)
    lg = jnp.where(is_grp, logits, -jnp.inf)
    mx = jnp.max(lg, axis=-1, keepdims=True)
    p_grp = 1.0 / jnp.sum(jnp.exp(lg - mx), axis=-1, keepdims=True)
    grp = jnp.min(jnp.where(lg == mx, lane, big), axis=-1, keepdims=True) - N_EXPERTS
    base = grp * EXPERTS_PER_GROUP
    sel = (lane >= base) & (lane < base + EXPERTS_PER_GROUP)
    le = jnp.where(sel, logits, -jnp.inf)
    ee = jnp.exp(le - jnp.max(le, axis=-1, keepdims=True))
    pe = ee / jnp.sum(ee, axis=-1, keepdims=True)
    top1 = jnp.max(pe, axis=-1, keepdims=True)
    idx1 = jnp.min(jnp.where(sel & (pe == top1), lane, big), axis=-1, keepdims=True)
    rest = sel & (lane != idx1)
    top2 = jnp.max(jnp.where(rest, pe, -1.0), axis=-1, keepdims=True)
    idx2 = jnp.min(jnp.where(rest & (pe == top2), lane, big), axis=-1, keepdims=True)
    denom = top1 + top2
    w1 = top1 / denom * p_grp
    w2 = top2 / denom * p_grp
    chl = jnp.zeros_like(logits)
    for j in range(EXPERTS_PER_GROUP):
        cj = jnp.where(idx1 == base + j, w1, 0.0) + jnp.where(idx2 == base + j, w2, 0.0)
        hi = cj.astype(BF16).astype(F32)
        chl = chl + jnp.where(lane == j, hi, 0.0) + jnp.where(lane == EXPERTS_PER_GROUP + j, cj - hi, 0.0)
    chl_ref[...] = chl.astype(BF16)
    grpt_ref[...] = jnp.transpose(jnp.broadcast_to(grp, logits.shape))[:8, :]


def _mix(hf, hb, qkvo, hd, x2d, g1, sc2, sh2, row0, tiles_per_row, gm, gd, wout, nrm, wr, br, diff_scale, tm):
    t, d = x2d.shape
    row_map = lambda i: (row0 + i // tiles_per_row, 0, 0)
    tok = lambda i: (i, 0)
    full = lambda i: (0, 0)
    return pl.pallas_call(
        functools.partial(_mix_kernel, diff_scale=diff_scale),
        grid=(t // tm,),
        in_specs=[pl.BlockSpec((tm, WIDTH), tok),
                  pl.BlockSpec((tm, WIDTH), tok),
                  pl.BlockSpec((tm, WIDTH), lambda i: (i, 3)),
                  pl.BlockSpec((tm, WIDTH), tok),
                  pl.BlockSpec((tm, d), tok),
                  pl.BlockSpec((1, 1, d), row_map),
                  pl.BlockSpec((1, 1, d), row_map),
                  pl.BlockSpec((1, 1, d), row_map),
                  pl.BlockSpec((1, WIDTH), full),
                  pl.BlockSpec((1, WIDTH), full),
                  pl.BlockSpec(wout.shape, full),
                  pl.BlockSpec((1, d), full),
                  pl.BlockSpec(wr.shape, full),
                  pl.BlockSpec(br.shape, full)],
        out_specs=[pl.BlockSpec((tm, d), tok),
                   pl.BlockSpec((tm, d), tok),
                   pl.BlockSpec((tm, LANES), tok),
                   pl.BlockSpec((8, tm), lambda i: (0, i))],
        out_shape=[jax.ShapeDtypeStruct((t, d), F32),
                   jax.ShapeDtypeStruct((t, d), BF16),
                   jax.ShapeDtypeStruct((t, LANES), BF16),
                   jax.ShapeDtypeStruct((8, t), F32)],
        compiler_params=_params(("parallel",)),
        name="mix_router",
    )(hf, hb, qkvo, hd, x2d, g1, sc2, sh2, gm, gd, wout, nrm, wr, br)


def _moe_kernel(h2_ref, chl_ref, grpt_ref, wgu_ref, wdn_ref, x1_ref, g2_ref, nrm_ref, y_ref,
                acc_ref, rank_ref, *, hidden, blk):
    g = pl.program_id(1)
    tm = h2_ref.shape[0]

    @pl.when(g == 0)
    def _():
        acc_ref[...] = jnp.zeros_like(acc_ref)
        gid = lax.broadcasted_iota(jnp.int32, (8, tm), 0).astype(F32)
        member = grpt_ref[...] == gid
        src = lax.broadcasted_iota(jnp.int32, (tm, tm), 0)
        dst = lax.broadcasted_iota(jnp.int32, (tm, tm), 1)
        before = jnp.where(src < dst, 1.0, 0.0).astype(BF16)
        rank = jnp.dot(jnp.where(member, 1.0, 0.0).astype(BF16), before, preferred_element_type=F32)
        rank_ref[...] = jnp.where(member, rank, -1.0)

    rank_g = rank_ref[pl.ds(g, 1), :]
    n_g = jnp.max(rank_g) + 1.0

    for j in range(tm // blk):
        @pl.when(n_g > j * blk)
        def _():
            slot = lax.broadcasted_iota(jnp.int32, (blk, tm), 0).astype(F32) + float(j * blk)
            pick = jnp.where(rank_g == slot, 1.0, 0.0).astype(BF16)
            xg = jnp.dot(pick, h2_ref[...], preferred_element_type=F32).astype(BF16)
            cg = jnp.dot(pick, chl_ref[...], preferred_element_type=F32)
            out = jnp.zeros((blk, acc_ref.shape[1]), F32)
            for e in range(EXPERTS_PER_GROUP):
                gu = jnp.dot(xg, wgu_ref[e], preferred_element_type=F32)
                a, u = gu[:, :hidden], gu[:, hidden:]
                ce = cg[:, e:e + 1] + cg[:, EXPERTS_PER_GROUP + e:EXPERTS_PER_GROUP + e + 1]
                act = (a * _sigmoid(a) * u * ce).astype(BF16)
                out = out + jnp.dot(act, wdn_ref[e], preferred_element_type=F32)
            acc_ref[...] += lax.dot_general(pick, out.astype(BF16), (((0,), (0,)), ((), ())),
                                            preferred_element_type=F32)

    @pl.when(g == pl.num_programs(1) - 1)
    def _():
        x2 = x1_ref[...] + g2_ref[0] * acc_ref[...]
        ms = jnp.mean(x2 * x2, axis=-1, keepdims=True)
        y_ref[...] = x2 * lax.rsqrt(ms + EPS) * nrm_ref[...]


def _moe(h2, chl, grpt, wgu, wdn, x1, g2, row0, tiles_per_row, nrm, tm, blk):
    t, d = x1.shape
    _, _, two_h = wgu.shape
    hidden = two_h // 2
    row_map = lambda i, g: (row0 + i // tiles_per_row, 0, 0)
    tok = lambda i, g: (i, 0)
    return pl.pallas_call(
        functools.partial(_moe_kernel, hidden=hidden, blk=blk),
        grid=(t // tm, N_GROUPS),
        in_specs=[pl.BlockSpec((tm, d), tok),
                  pl.BlockSpec((tm, LANES), tok),
                  pl.BlockSpec((8, tm), lambda i, g: (0, i)),
                  pl.BlockSpec((EXPERTS_PER_GROUP, d, two_h), lambda i, g: (g, 0, 0)),
                  pl.BlockSpec((EXPERTS_PER_GROUP, hidden, d), lambda i, g: (g, 0, 0)),
                  pl.BlockSpec((tm, d), tok),
                  pl.BlockSpec((1, 1, d), row_map),
                  pl.BlockSpec((1, d), lambda i, g: (0, 0))],
        out_specs=pl.BlockSpec((tm, d), tok),
        out_shape=jax.ShapeDtypeStruct((t, d), F32),
        scratch_shapes=[pltpu.VMEM((tm, d), F32), pltpu.VMEM((8, tm), F32)],
        compiler_params=_params(("parallel", "arbitrary")),
        name="moe_experts",
    )(h2, chl, grpt, wgu, wdn, x1, g2, nrm)


def _layer(x, mods, row0, per_request, weights, states, cache, rope_tabs, emit_cache, tm, tq, hp):
    b, s, d = x.shape
    sh1, sc1, g1, sh2, sc2, g2 = mods
    (nrm_mix, nrm_ffn, w_in_r, b_gate, gm, gd, lp, wout, wr, br, wgu, wdn, nrm_final, lam_init) = weights
    x2d = x.reshape(b * s, d)
    tiles_per_row = (s // tm) if per_request else (b * s // tm)
    outs = _proj(x2d, nrm_mix, sc1, sh1, row0, tiles_per_row, w_in_r, b_gate, rope_tabs, emit_cache, tm)
    qkvo, qd, kd, vd, gates = outs[:5]
    c0, n0, m0 = states
    qkvo3 = qkvo.reshape(b, s, 4 * WIDTH)
    km_t = qkvo3[:, :, WIDTH:2 * WIDTH].transpose(0, 2, 1)
    hf, hb, cf, nf, mf = _mlstm(qkvo3, km_t, gates.reshape(b, s, N_GATES), c0, n0, m0, hp)
    qd = qd.reshape(b, s, WIDTH)
    kd = kd.reshape(b, s, WIDTH)
    vd = vd.reshape(b, s, WIDTH)
    if cache is not None:
        ck, cv = cache
        kd = jnp.concatenate([kd, ck.reshape(b, -1, WIDTH).astype(BF16)], axis=1)
        vd = jnp.concatenate([vd, cv.reshape(b, -1, WIDTH).astype(BF16)], axis=1)
    hd = _attn(lp, qd, kd.transpose(0, 2, 1), vd, lam_init, tq)
    x1, h2, chl, grpt = _mix(hf.reshape(b * s, WIDTH), hb.reshape(b * s, WIDTH), qkvo, hd.reshape(b * s, WIDTH),
                             x2d, g1, sc2, sh2, row0, tiles_per_row, gm, gd, wout, nrm_ffn, wr, br,
                             1.0 - lam_init, tm)
    tm_moe = 2 * tm
    y = _moe(h2, chl, grpt, wgu, wdn, x1, g2, row0, tiles_per_row // 2 if per_request else b * s // tm_moe,
             nrm_final, tm_moe, MOE_BLOCK)
    new_kv = outs[5:] if emit_cache else None
    return y.reshape(b, s, d), new_kv, (cf, nf, mf)


def kernel(x_prompt, x_sample, cache_k, cache_v, state_C, state_n, state_m, c, c_ctx, w_ada, b_ada, norm_mix, norm_ffn, w_in, b_gate, mlstm_norm, diff_norm, diff_lambda, w_out, w_route_group, b_route_group, w_route_expert, b_route_expert, w_gate_up, w_down, final_norm):
    depth = w_in.shape[0]
    assert depth == 1, "the final norm is fused into the last layer's expert kernel"
    bp, sp, d = x_prompt.shape
    bs, ss, _ = x_sample.shape
    rows = 16
    assert bs < rows
    cvec = jnp.zeros((rows, d), F32).at[:bs].set(c).at[bs].set(c_ctx)
    rope_tabs = _rope_tables(ss)
    xp, xs = x_prompt, x_sample
    new_k, new_v, new_c, new_n, new_m = [], [], [], [], []
    for l in range(depth):
        lam_init = 0.8 - 0.6 * math.exp(-0.3 * l)
        mod = _ada(cvec, w_ada[l], b_ada[l])
        mods = [mod[:, i * d:(i + 1) * d].reshape(rows, 1, d) for i in range(6)]
        wl = w_in[l]
        w_in_r = jnp.concatenate(
            [wl[:, :4 * WIDTH], wl[:, 4 * WIDTH + N_GATES:], wl[:, 4 * WIDTH:4 * WIDTH + N_GATES],
             jnp.zeros((d, LANES - N_GATES), F32)], axis=1).astype(BF16)
        wr = jnp.concatenate([w_route_expert[l], w_route_group[l],
                              jnp.zeros((d, LANES - N_EXPERTS - N_GROUPS), F32)], axis=1)
        br = jnp.concatenate([b_route_expert[l], b_route_group[l],
                              jnp.zeros((LANES - N_EXPERTS - N_GROUPS,), F32)]).reshape(1, LANES)
        weights = (norm_mix[l].reshape(1, d), norm_ffn[l].reshape(1, d), w_in_r, b_gate[l].reshape(1, N_GATES),
                   mlstm_norm[l].reshape(1, WIDTH), diff_norm[l].reshape(1, WIDTH), diff_lambda[l],
                   w_out[l].astype(BF16), wr, br, w_gate_up[l].astype(BF16), w_down[l].astype(BF16),
                   final_norm.reshape(1, d), lam_init)

        zero_states = (jnp.zeros((bp, 2, N_HEADS, HEAD_DIM, HEAD_DIM), F32),
                       jnp.zeros((bp, 2, N_HEADS, HEAD_DIM), F32),
                       jnp.zeros((bp, 2, N_HEADS), F32))
        xp, (nk, nv), (cf, nf, mf) = _layer(xp, mods, bs, False, weights, zero_states, None, None, True,
                                            tm=512, tq=256, hp=2)
        new_k.append(nk.reshape(bp, sp, N_HEADS, 2, QK_DIM))
        new_v.append(nv.reshape(bp, sp, N_HEADS, HEAD_DIM))
        new_c.append(cf)
        new_n.append(nf)
        new_m.append(mf)

        states = (state_C[:, l], state_n[:, l], state_m[:, l])
        xs, _, _ = _layer(xs, mods, 0, True, weights, states, (cache_k[:, l], cache_v[:, l]), rope_tabs, False,
                          tm=512, tq=512, hp=4)

    return (xp, xs, jnp.stack(new_k, axis=1), jnp.stack(new_v, axis=1),
            jnp.stack(new_c, axis=1), jnp.stack(new_n, axis=1), jnp.stack(new_m, axis=1))
```

```python
import functools
import math

import jax
import jax.numpy as jnp
from jax import lax
from jax.experimental import pallas as pl
from jax.experimental.pallas import tpu as pltpu

F32 = jnp.float32
BF16 = jnp.bfloat16
HIGHEST = lax.Precision.HIGHEST

EPS = 1e-6
GRID_W = 64
ROPE_THETA = 10000.0
N_HEADS = 4
HEAD_DIM = 128
QK_DIM = 64
WIDTH = N_HEADS * HEAD_DIM
CHUNK = 128
N_GATES = 4 * N_HEADS
N_GROUPS = 4
EXPERTS_PER_GROUP = 4
N_EXPERTS = 16
LANES = 128
SUBLANES = 8
MOE_BLOCK = 256
ATTN_SUB = 128
LOG2E = math.log2(math.e)
VMEM_LIMIT = 56 * 1024 * 1024


def _params(sem):
    return pltpu.CompilerParams(dimension_semantics=sem, vmem_limit_bytes=VMEM_LIMIT)


def _log_sigmoid(x):
    return jnp.minimum(x, 0.0) - jnp.log1p(jnp.exp(-jnp.abs(x)))


def _sigmoid(x):
    return 1.0 / (1.0 + jnp.exp(-x))


def _ada_kernel(c_ref, w_ref, b_ref, o_ref):
    c = c_ref[...]
    s = c * _sigmoid(c)
    o_ref[...] = jnp.dot(s, w_ref[...], preferred_element_type=F32, precision=HIGHEST) + b_ref[...]


def _ada(cvec, w_ada, b_ada):
    rows, d = cvec.shape
    n = w_ada.shape[1]
    tn = 1024
    return pl.pallas_call(
        _ada_kernel,
        grid=(n // tn,),
        in_specs=[pl.BlockSpec((rows, d), lambda j: (0, 0)),
                  pl.BlockSpec((d, tn), lambda j: (0, j)),
                  pl.BlockSpec((1, tn), lambda j: (0, j))],
        out_specs=pl.BlockSpec((rows, tn), lambda j: (0, j)),
        out_shape=jax.ShapeDtypeStruct((rows, n), F32),
        compiler_params=_params(("arbitrary",)),
        name="ada",
    )(cvec, w_ada, b_ada.reshape(1, n))


def _proj_kernel(*refs, rope, emit_cache):
    x_ref, nrm_ref, sc_ref, sh_ref, w_ref, wt_ref, bg_ref = refs[:7]
    refs = refs[7:]
    if rope:
        cos_ref, sa_ref, sb_ref = refs[:3]
        refs = refs[3:]
    m_ref, kmt_ref, q_ref, k_ref, vt_ref, gt_ref = refs[:6]
    refs = refs[6:]
    if emit_cache:
        newk_ref, newv_ref = refs

    x = x_ref[...]
    ms = jnp.mean(x * x, axis=-1, keepdims=True)
    y = x * lax.rsqrt(ms + EPS) * nrm_ref[...]
    h = (y * (1.0 + sc_ref[0]) + sh_ref[0]).astype(BF16)

    def section(j):
        return jnp.dot(h, w_ref[:, j * WIDTH:(j + 1) * WIDTH], preferred_element_type=F32)

    def rotate(p):
        cos, sa, sb = cos_ref[...], sa_ref[...], sb_ref[...]
        outs = []
        for j in range(N_HEADS):
            xb = p[:, j * LANES:(j + 1) * LANES]
            outs.append(xb * cos + pltpu.roll(xb, LANES - 16, 1) * sa + pltpu.roll(xb, 16, 1) * sb)
        return jnp.concatenate(outs, axis=1)

    def section_t(j, n):
        return lax.dot_general(wt_ref[j * WIDTH:j * WIDTH + n, :], h, (((1,), (1,)), ((), ())),
                               preferred_element_type=F32)

    for j in range(3):
        m_ref[:, j * WIDTH:(j + 1) * WIDTH] = section(j).astype(BF16)
    kmt_ref[...] = (section_t(0, WIDTH) * (HEAD_DIM ** -0.5)).astype(BF16)
    pq = section(3)
    if rope:
        pq = rotate(pq)
    q_ref[...] = (pq * (LOG2E * QK_DIM ** -0.5)).astype(BF16)
    pk = section(4)
    if emit_cache:
        newk_ref[...] = pk
    if rope:
        pk = rotate(pk)
    k_ref[...] = pk.astype(BF16)
    if emit_cache:
        newv_ref[...] = section(5)
    vt_ref[...] = section_t(1, WIDTH).astype(BF16)
    gt_ref[...] = section_t(2, LANES)[:N_GATES, :] + bg_ref[...]


def _proj(x2d, nrm, sc, sh, row0, tiles_per_row, w_r, w_t, b_gate, rope_tabs, emit_cache, tm):
    t, d = x2d.shape
    nt = t // tm
    rope = rope_tabs is not None
    row_map = lambda i: (row0 + i // tiles_per_row, 0, 0)
    tok = lambda i: (i, 0)
    feat = lambda i: (0, i)
    in_specs = [pl.BlockSpec((tm, d), tok),
                pl.BlockSpec((1, d), lambda i: (0, 0)),
                pl.BlockSpec((1, 1, d), row_map),
                pl.BlockSpec((1, 1, d), row_map),
                pl.BlockSpec(w_r.shape, lambda i: (0, 0)),
                pl.BlockSpec(w_t.shape, lambda i: (0, 0)),
                pl.BlockSpec((N_GATES, 1), lambda i: (0, 0))]
    args = [x2d, nrm, sc, sh, w_r, w_t, b_gate]
    if rope:
        tps = rope_tabs[0].shape[0] // tm
        for tab in rope_tabs:
            in_specs.append(pl.BlockSpec((tm, LANES), lambda i: (i % tps, 0)))
            args.append(tab)
    out_shape = [jax.ShapeDtypeStruct((t, 3 * WIDTH), BF16),
                 jax.ShapeDtypeStruct((WIDTH, t), BF16),
                 jax.ShapeDtypeStruct((t, WIDTH), BF16),
                 jax.ShapeDtypeStruct((t, WIDTH), BF16),
                 jax.ShapeDtypeStruct((WIDTH, t), BF16),
                 jax.ShapeDtypeStruct((N_GATES, t), F32)]
    out_specs = [pl.BlockSpec((tm, 3 * WIDTH), tok),
                 pl.BlockSpec((WIDTH, tm), feat),
                 pl.BlockSpec((tm, WIDTH), tok),
                 pl.BlockSpec((tm, WIDTH), tok),
                 pl.BlockSpec((WIDTH, tm), feat),
                 pl.BlockSpec((N_GATES, tm), feat)]
    if emit_cache:
        out_shape += [jax.ShapeDtypeStruct((t, WIDTH), F32)] * 2
        out_specs += [pl.BlockSpec((tm, WIDTH), tok)] * 2
    return pl.pallas_call(
        functools.partial(_proj_kernel, rope=rope, emit_cache=emit_cache),
        grid=(nt,),
        in_specs=in_specs,
        out_specs=out_specs,
        out_shape=out_shape,
        compiler_params=_params(("parallel",)),
        name="proj_rope" if rope else "proj_ctx",
    )(*args)


def _rope_tables(seq):
    t = jnp.arange(seq)
    row = (t // GRID_W).astype(F32)[:, None]
    col = (t % GRID_W).astype(F32)[:, None]
    lane = jnp.arange(LANES)
    freqs = ROPE_THETA ** (-(lane % 16).astype(F32) / 16.0)
    pos = jnp.where(((lane % 64) < 32)[None, :], row, col)
    ang = pos * freqs[None, :]
    cos, sin = jnp.cos(ang), jnp.sin(ang)
    first = ((lane % 32) < 16)[None, :]
    return cos, jnp.where(first, -sin, 0.0), jnp.where(first, 0.0, sin)


def _lane_scan(x, pos, op, forward):
    n = x.shape[-1]
    shift = 1
    while shift < n:
        if forward:
            y = pltpu.roll(x, shift, 1)
            ok = pos >= shift
        else:
            y = pltpu.roll(x, n - shift, 1)
            ok = pos < n - shift
        x = jnp.where(ok, op(x, y), x)
        shift *= 2
    return x


def _mlstm_kernel(q_ref, kt_ref, v_ref, g_ref, c0_ref, n0_ref, m0_ref,
                  hf_ref, hb_ref, cf_ref, nf_ref, mf_ref,
                  st_ref, r_ref, mp_ref, mn_ref, dc_ref, tab_ref, *, nc, hp):
    L = CHUNK
    rows = nc * hp
    pos = lax.broadcasted_iota(jnp.int32, (rows, L), 1)
    tab_ref[...] = jnp.zeros_like(tab_ref)
    for d in range(2):
        i_pre = g_ref[0, 0, (2 * d) * rows:(2 * d + 1) * rows, :]
        lf = _log_sigmoid(g_ref[0, 0, (2 * d + 1) * rows:(2 * d + 2) * rows, :])
        pre = _lane_scan(lf, pos, jnp.add, True)
        suf = _lane_scan(lf, pos, jnp.add, False)
        tot = pre + suf - lf
        b = pre if d == 0 else suf
        r = i_pre - b
        r_max = _lane_scan(r, pos, jnp.maximum, d == 0)
        gk = tot - b + i_pre
        g_max = jnp.maximum(_lane_scan(gk, pos, jnp.maximum, True), _lane_scan(gk, pos, jnp.maximum, False))
        m = m0_ref[0, d, :, 0, :]
        for c in (range(nc) if d == 0 else reversed(range(nc))):
            sl = slice(c * hp, (c + 1) * hp)
            mp_ref[d, sl, :] = m
            m = jnp.maximum(tot[sl] + m, g_max[sl])
            mn_ref[d, sl, :] = m
        mf_ref[0, d, :, 0, :] = m
        m_prev = mp_ref[d]
        m_new = mn_ref[d]
        m_row = jnp.maximum(m_prev, r_max)
        r_ref[d] = r
        dc_ref[d] = jnp.exp(tot + m_prev - m_new)
        w_k = jnp.exp(gk - m_new)
        m_t = b + m_row
        for c in range(nc):
            src = slice(c * hp, (c + 1) * hp)
            slot = c if d == 0 else nc - 1 - c
            for qi, val in enumerate((m_row, m_t, w_k)):
                base = (2 * qi + d) * hp
                tab_ref[slot, base:base + hp, :] = val[src]

    for d in range(2):
        for hh in range(hp):
            st_ref[d, hh, :, :HEAD_DIM] = jnp.transpose(c0_ref[0, d, hh])
            st_ref[d, hh, :, HEAD_DIM:] = jnp.transpose(jnp.broadcast_to(n0_ref[0, d, hh], (HEAD_DIM, HEAD_DIM)))

    rr = lax.broadcasted_iota(jnp.int32, (L, L), 0)
    cc = lax.broadcasted_iota(jnp.int32, (L, L), 1)
    masks = (cc <= rr, cc >= rr)
    ones = jnp.ones((L, HEAD_DIM), BF16)

    def body(c, carry):
        tab = jnp.transpose(tab_ref[c])
        for d in range(2):
            ci = c if d == 0 else nc - 1 - c
            off = pl.multiple_of(ci * L, L)
            for hh in range(hp):
                lanes = slice(hh * HEAD_DIM, (hh + 1) * HEAD_DIM)
                row = ci * hp + hh
                col = d * hp + hh
                m_col = jnp.broadcast_to(tab[:, col:col + 1], (L, HEAD_DIM))
                mt_col = jnp.broadcast_to(tab[:, 2 * hp + col:2 * hp + col + 1], (L, HEAD_DIM))
                wk_col = jnp.broadcast_to(tab[:, 4 * hp + col:4 * hp + col + 1], (L, HEAD_DIM))
                r_row = r_ref[d, pl.ds(row, 1), :]
                m_prev = mp_ref[d, pl.ds(row, 1), :]
                decay = dc_ref[d, pl.ds(row, 1), :]
                q = q_ref[0, pl.ds(off, L), lanes]
                v = v_ref[0, pl.ds(off, L), lanes]
                kt = kt_ref[lanes, pl.ds(off, L)]
                state = st_ref[d, hh]

                w_intra = jnp.exp(jnp.where(masks[d], r_row - m_col, -jnp.inf))
                s = (jnp.dot(q, kt, preferred_element_type=F32) * w_intra).astype(BF16)
                intra = jnp.dot(s, jnp.concatenate([v, ones], axis=1), preferred_element_type=F32)
                inter = jnp.dot(q, state.astype(BF16), preferred_element_type=F32)
                w_state = jnp.exp(m_prev - m_col)
                both = intra + jnp.concatenate([w_state, w_state], axis=1) * inter
                den = jnp.maximum(jnp.abs(both[:, HEAD_DIM:]), jnp.exp(-mt_col))
                h = both[:, :HEAD_DIM] / den

                vw = (v.astype(F32) * wk_col).astype(BF16)
                upd = jnp.dot(kt, jnp.concatenate([vw, wk_col.astype(BF16)], axis=1),
                              preferred_element_type=F32)
                st_ref[d, hh] = jnp.concatenate([decay, decay], axis=1) * state + upd
                out_ref = hf_ref if d == 0 else hb_ref
                out_ref[0, pl.ds(off, L), lanes] = h.astype(BF16)
        return carry

    lax.fori_loop(0, nc, body, 0)

    for d in range(2):
        for hh in range(hp):
            state = st_ref[d, hh]
            cf_ref[0, d, hh] = jnp.transpose(state[:, :HEAD_DIM])
            nf_ref[0, d, hh] = jnp.transpose(state[:, HEAD_DIM:])[0:1, :]


def _mlstm(qvo, kt, gt, c0, n0, m0, hp):
    b, s, _ = qvo.shape
    nc = s // CHUNK
    hg = N_HEADS // hp
    rows = nc * hp
    g6 = gt.reshape(4, hg, hp, b, nc, CHUNK).transpose(3, 1, 0, 4, 2, 5).reshape(b, hg, 4 * rows, CHUNK)
    n0r = n0.reshape(b, 2, N_HEADS, 1, HEAD_DIM)
    m0r = jnp.broadcast_to(m0[..., None, None], (b, 2, N_HEADS, 1, HEAD_DIM))
    blk = hp * HEAD_DIM
    nblk = WIDTH // blk
    state_spec = lambda w: pl.BlockSpec((1, 2, hp, w, HEAD_DIM), lambda i, j: (i, 0, j, 0, 0))
    in_specs = [pl.BlockSpec((1, s, blk), lambda i, j: (i, 0, j)),
                pl.BlockSpec((blk, s), lambda i, j: (j, i)),
                pl.BlockSpec((1, s, blk), lambda i, j: (i, 0, nblk + j)),
                pl.BlockSpec((1, 1, 4 * rows, CHUNK), lambda i, j: (i, j, 0, 0)),
                state_spec(HEAD_DIM), state_spec(1), state_spec(1)]
    out_specs = [pl.BlockSpec((1, s, blk), lambda i, j: (i, 0, j)),
                 pl.BlockSpec((1, s, blk), lambda i, j: (i, 0, j)),
                 state_spec(HEAD_DIM), state_spec(1), state_spec(1)]
    out_shape = [jax.ShapeDtypeStruct((b, s, WIDTH), BF16),
                 jax.ShapeDtypeStruct((b, s, WIDTH), BF16),
                 jax.ShapeDtypeStruct((b, 2, N_HEADS, HEAD_DIM, HEAD_DIM), F32),
                 jax.ShapeDtypeStruct((b, 2, N_HEADS, 1, HEAD_DIM), F32),
                 jax.ShapeDtypeStruct((b, 2, N_HEADS, 1, HEAD_DIM), F32)]
    scratch = [pltpu.VMEM((2, hp, HEAD_DIM, 2 * HEAD_DIM), F32),
               pltpu.VMEM((2, rows, CHUNK), F32),
               pltpu.VMEM((2, rows, CHUNK), F32),
               pltpu.VMEM((2, rows, CHUNK), F32),
               pltpu.VMEM((2, rows, CHUNK), F32),
               pltpu.VMEM((nc, LANES, CHUNK), F32)]
    hf, hb, cf, nf, mf = pl.pallas_call(
        functools.partial(_mlstm_kernel, nc=nc, hp=hp),
        grid=(b, hg),
        in_specs=in_specs,
        out_specs=out_specs,
        out_shape=out_shape,
        scratch_shapes=scratch,
        compiler_params=_params(("parallel", "parallel")),
        name="mlstm",
    )(qvo, kt, qvo, g6, c0, n0r, m0r)
    return hf, hb, cf, nf[:, :, :, 0, :], mf[:, :, :, 0, 0]


def _attn_kernel(*refs, tq, sub, n_new, n_cache, lam_init):
    if n_cache:
        lp_ref, q_ref, k_ref, kc_ref, vt_ref, vct_ref, o_ref, s_even_ref, s_odd_ref = refs
    else:
        lp_ref, q_ref, k_ref, vt_ref, o_ref, s_even_ref, s_odd_ref = refs
    lp = lp_ref[...]
    lam = (jnp.exp(jnp.sum(lp[0:1] * lp[1:2], axis=-1, keepdims=True))
           - jnp.exp(jnp.sum(lp[2:3] * lp[3:4], axis=-1, keepdims=True)) + lam_init)
    lane = lax.broadcasted_iota(jnp.int32, (sub, HEAD_DIM), 1)
    s_refs = (s_even_ref, s_odd_ref)
    n_sub = tq // sub
    nt = (((1,), (1,)), ((), ()))

    def scores(st):
        q = q_ref[0, st * sub:(st + 1) * sub, :]
        zero = jnp.zeros_like(q)
        qq = jnp.concatenate([jnp.where(lane < QK_DIM, q, zero), jnp.where(lane >= QK_DIM, q, zero)], axis=0)
        s = lax.dot_general(k_ref[0], qq, nt, preferred_element_type=F32)
        s_refs[st % 2][0:n_new, :] = s
        m = jnp.max(s, axis=0, keepdims=True)
        if n_cache:
            sc = lax.dot_general(kc_ref[0], qq, nt, preferred_element_type=F32)
            s_refs[st % 2][n_new:n_new + n_cache, :] = sc
            m = jnp.maximum(m, jnp.max(sc, axis=0, keepdims=True))
        return m

    m_next = scores(0)
    for st in range(n_sub):
        m = m_next
        if st + 1 < n_sub:
            m_next = scores(st + 1)
        e = jnp.exp2(s_refs[st % 2][...] - m)
        l = jnp.sum(e, axis=0, keepdims=True)
        eb = e.astype(BF16)
        o = jnp.dot(vt_ref[...], eb[0:n_new], preferred_element_type=F32)
        if n_cache:
            o = o + jnp.dot(vct_ref[0], eb[n_new:n_new + n_cache], preferred_element_type=F32)
        o = o / l
        out_t = o[:, :sub] - lam * o[:, sub:]
        o_ref[0, st * sub:(st + 1) * sub, :] = jnp.transpose(out_t).astype(BF16)


def _attn(lp, q, k, vt, cache, lam_init, tq):
    b, s, _ = q.shape
    n_cache = 0 if cache is None else cache[0].shape[1]
    in_specs = [pl.BlockSpec(lp.shape, lambda i, h, j: (0, 0)),
                pl.BlockSpec((1, tq, HEAD_DIM), lambda i, h, j: (i, j, h)),
                pl.BlockSpec((1, s, HEAD_DIM), lambda i, h, j: (i, 0, h))]
    args = [lp, q, k]
    if n_cache:
        in_specs.append(pl.BlockSpec((1, n_cache, HEAD_DIM), lambda i, h, j: (i, 0, h)))
        args.append(cache[0])
    in_specs.append(pl.BlockSpec((HEAD_DIM, s), lambda i, h, j: (h, i)))
    args.append(vt)
    if n_cache:
        in_specs.append(pl.BlockSpec((1, HEAD_DIM, n_cache), lambda i, h, j: (i, h, 0)))
        args.append(cache[1])
    keys = s + n_cache
    return pl.pallas_call(
        functools.partial(_attn_kernel, tq=tq, sub=ATTN_SUB, n_new=s, n_cache=n_cache, lam_init=lam_init),
        grid=(b, N_HEADS, s // tq),
        in_specs=in_specs,
        out_specs=pl.BlockSpec((1, tq, HEAD_DIM), lambda i, h, j: (i, j, h)),
        out_shape=jax.ShapeDtypeStruct((b, s, WIDTH), BF16),
        scratch_shapes=[pltpu.VMEM((keys, 2 * ATTN_SUB), F32), pltpu.VMEM((keys, 2 * ATTN_SUB), F32)],
        compiler_params=_params(("parallel", "parallel", "parallel")),
        name="diff_attn",
    )(*args)


def _head_norm(x):
    outs = []
    for j in range(N_HEADS):
        xb = x[:, j * HEAD_DIM:(j + 1) * HEAD_DIM]
        outs.append(xb * lax.rsqrt(jnp.mean(xb * xb, axis=-1, keepdims=True) + EPS))
    return jnp.concatenate(outs, axis=1)


def _mix_kernel(hf_ref, hb_ref, om_ref, hd_ref, x_ref, g1_ref, sc2_ref, sh2_ref, gm_ref, gd_ref,
                wout_ref, nrm_ref, wr_ref, br_ref, x1_ref, h2_ref, chl_ref, grpt_ref, *, diff_scale):
    hm = _head_norm(hf_ref[...].astype(F32) + hb_ref[...].astype(F32))
    hm = hm * gm_ref[...] * _sigmoid(om_ref[...].astype(F32))
    hd = _head_norm(hd_ref[...].astype(F32)) * gd_ref[...] * diff_scale
    mix = (jnp.dot(hm.astype(BF16), wout_ref[:WIDTH, :], preferred_element_type=F32)
           + jnp.dot(hd.astype(BF16), wout_ref[WIDTH:, :], preferred_element_type=F32))
    x1 = x_ref[...] + g1_ref[0] * mix
    x1_ref[...] = x1
    ms = jnp.mean(x1 * x1, axis=-1, keepdims=True)
    h2 = (x1 * lax.rsqrt(ms + EPS) * nrm_ref[...]) * (1.0 + sc2_ref[0]) + sh2_ref[0]
    h2_hi = h2.astype(BF16)
    h2_ref[...] = h2_hi

    h2_lo = (h2 - h2_hi.astype(F32)).astype(BF16)
    parts = (jnp.dot(h2_hi, wr_ref[...], preferred_element_type=F32)
             + jnp.dot(h2_lo, wr_ref[...], preferred_element_type=F32))
    logits = parts[:, :LANES] + parts[:, LANES:] + br_ref[...]
    lane = lax.broadcasted_iota(jnp.int32, logits.shape, 1).astype(F32)
    big = float(LANES)
    is_grp = (lane >= N_EXPERTS) & (lane < N_EXPERTS + N_GROUPS)
    lg = jnp.where(is_grp, logits, -jnp.inf)
    mx = jnp.max(lg, axis=-1, keepdims=True)
    p_grp = 1.0 / jnp.sum(jnp.exp(lg - mx), axis=-1, keepdims=True)
    grp = jnp.min(jnp.where(lg == mx, lane, big), axis=-1, keepdims=True) - N_EXPERTS
    base = grp * EXPERTS_PER_GROUP
    sel = (lane >= base) & (lane < base + EXPERTS_PER_GROUP)
    le = jnp.where(sel, logits, -jnp.inf)
    ee = jnp.exp(le - jnp.max(le, axis=-1, keepdims=True))
    pe = ee / jnp.sum(ee, axis=-1, keepdims=True)
    top1 = jnp.max(pe, axis=-1, keepdims=True)
    idx1 = jnp.min(jnp.where(sel & (pe == top1), lane, big), axis=-1, keepdims=True)
    rest = sel & (lane != idx1)
    top2 = jnp.max(jnp.where(rest, pe, -1.0), axis=-1, keepdims=True)
    idx2 = jnp.min(jnp.where(rest & (pe == top2), lane, big), axis=-1, keepdims=True)
    denom = top1 + top2
    w1 = top1 / denom * p_grp
    w2 = top2 / denom * p_grp
    chl = jnp.zeros_like(logits)
    for j in range(EXPERTS_PER_GROUP):
        cj = jnp.where(idx1 == base + j, w1, 0.0) + jnp.where(idx2 == base + j, w2, 0.0)
        hi = cj.astype(BF16).astype(F32)
        chl = chl + jnp.where(lane == j, hi, 0.0) + jnp.where(lane == EXPERTS_PER_GROUP + j, cj - hi, 0.0)
    chl_ref[...] = chl.astype(BF16)
    grpt_ref[...] = jnp.transpose(jnp.broadcast_to(grp, logits.shape))[:SUBLANES, :]


def _mix(hf, hb, qvo, hd, x2d, g1, sc2, sh2, row0, tiles_per_row, gm, gd, wout, nrm, wr, br, diff_scale, tm):
    t, d = x2d.shape
    row_map = lambda i: (row0 + i // tiles_per_row, 0, 0)
    tok = lambda i: (i, 0)
    full = lambda i: (0, 0)
    return pl.pallas_call(
        functools.partial(_mix_kernel, diff_scale=diff_scale),
        grid=(t // tm,),
        in_specs=[pl.BlockSpec((tm, WIDTH), tok),
                  pl.BlockSpec((tm, WIDTH), tok),
                  pl.BlockSpec((tm, WIDTH), lambda i: (i, 2)),
                  pl.BlockSpec((tm, WIDTH), tok),
                  pl.BlockSpec((tm, d), tok),
                  pl.BlockSpec((1, 1, d), row_map),
                  pl.BlockSpec((1, 1, d), row_map),
                  pl.BlockSpec((1, 1, d), row_map),
                  pl.BlockSpec((1, WIDTH), full),
                  pl.BlockSpec((1, WIDTH), full),
                  pl.BlockSpec(wout.shape, full),
                  pl.BlockSpec((1, d), full),
                  pl.BlockSpec(wr.shape, full),
                  pl.BlockSpec(br.shape, full)],
        out_specs=[pl.BlockSpec((tm, d), tok),
                   pl.BlockSpec((tm, d), tok),
                   pl.BlockSpec((tm, LANES), tok),
                   pl.BlockSpec((SUBLANES, tm), lambda i: (0, i))],
        out_shape=[jax.ShapeDtypeStruct((t, d), F32),
                   jax.ShapeDtypeStruct((t, d), BF16),
                   jax.ShapeDtypeStruct((t, LANES), BF16),
                   jax.ShapeDtypeStruct((SUBLANES, t), F32)],
        compiler_params=_params(("parallel",)),
        name="mix_router",
    )(hf, hb, qvo, hd, x2d, g1, sc2, sh2, gm, gd, wout, nrm, wr, br)


def _moe_kernel(h2_ref, chl_ref, grpt_ref, wgu_ref, wdn_ref, x1_ref, g2_ref, nrm_ref, y_ref,
                acc_ref, rank_ref, *, hidden, blk):
    g = pl.program_id(1)
    tm = h2_ref.shape[0]

    @pl.when(g == 0)
    def _():
        acc_ref[...] = jnp.zeros_like(acc_ref)
        gid = lax.broadcasted_iota(jnp.int32, (SUBLANES, tm), 0).astype(F32)
        member = grpt_ref[...] == gid
        src = lax.broadcasted_iota(jnp.int32, (tm, tm), 0)
        dst = lax.broadcasted_iota(jnp.int32, (tm, tm), 1)
        before = jnp.where(src < dst, 1.0, 0.0).astype(BF16)
        rank = jnp.dot(jnp.where(member, 1.0, 0.0).astype(BF16), before, preferred_element_type=F32)
        rank_ref[...] = jnp.where(member, rank, -1.0)

    rank_g = rank_ref[pl.ds(g, 1), :]
    n_g = jnp.max(rank_g) + 1.0

    for j in range(tm // blk):
        @pl.when(n_g > j * blk)
        def _():
            slot = lax.broadcasted_iota(jnp.int32, (blk, tm), 0).astype(F32) + float(j * blk)
            pick = jnp.where(rank_g == slot, 1.0, 0.0).astype(BF16)
            xg = jnp.dot(pick, h2_ref[...], preferred_element_type=F32).astype(BF16)
            cg = jnp.dot(pick, chl_ref[...], preferred_element_type=F32)
            out = jnp.zeros((blk, acc_ref.shape[1]), F32)
            for e in range(EXPERTS_PER_GROUP):
                gu = jnp.dot(xg, wgu_ref[e], preferred_element_type=F32)
                a, u = gu[:, :hidden], gu[:, hidden:]
                ce = cg[:, e:e + 1] + cg[:, EXPERTS_PER_GROUP + e:EXPERTS_PER_GROUP + e + 1]
                act = (a * _sigmoid(a) * u * ce).astype(BF16)
                out = out + jnp.dot(act, wdn_ref[e], preferred_element_type=F32)
            acc_ref[...] += lax.dot_general(pick, out.astype(BF16), (((0,), (0,)), ((), ())),
                                            preferred_element_type=F32)

    @pl.when(g == pl.num_programs(1) - 1)
    def _():
        x2 = x1_ref[...] + g2_ref[0] * acc_ref[...]
        ms = jnp.mean(x2 * x2, axis=-1, keepdims=True)
        y_ref[...] = x2 * lax.rsqrt(ms + EPS) * nrm_ref[...]


def _moe(h2, chl, grpt, wgu, wdn, x1, g2, row0, tiles_per_row, nrm, tm, blk):
    t, d = x1.shape
    _, _, two_h = wgu.shape
    hidden = two_h // 2
    row_map = lambda i, g: (row0 + i // tiles_per_row, 0, 0)
    tok = lambda i, g: (i, 0)
    return pl.pallas_call(
        functools.partial(_moe_kernel, hidden=hidden, blk=blk),
        grid=(t // tm, N_GROUPS),
        in_specs=[pl.BlockSpec((tm, d), tok),
                  pl.BlockSpec((tm, LANES), tok),
                  pl.BlockSpec((SUBLANES, tm), lambda i, g: (0, i)),
                  pl.BlockSpec((EXPERTS_PER_GROUP, d, two_h), lambda i, g: (g, 0, 0)),
                  pl.BlockSpec((EXPERTS_PER_GROUP, hidden, d), lambda i, g: (g, 0, 0)),
                  pl.BlockSpec((tm, d), tok),
                  pl.BlockSpec((1, 1, d), row_map),
                  pl.BlockSpec((1, d), lambda i, g: (0, 0))],
        out_specs=pl.BlockSpec((tm, d), tok),
        out_shape=jax.ShapeDtypeStruct((t, d), F32),
        scratch_shapes=[pltpu.VMEM((tm, d), F32), pltpu.VMEM((SUBLANES, tm), F32)],
        compiler_params=_params(("parallel", "arbitrary")),
        name="moe_experts",
    )(h2, chl, grpt, wgu, wdn, x1, g2, nrm)


def _layer(x, mods, row0, per_request, weights, states, cache, rope_tabs, emit_cache, tm, tq, hp):
    b, s, d = x.shape
    sh1, sc1, g1, sh2, sc2, g2 = mods
    (nrm_mix, nrm_ffn, w_in_r, w_in_t, b_gate, gm, gd, lp, wout, wr, br, wgu, wdn, nrm_final, lam_init) = weights
    x2d = x.reshape(b * s, d)
    tiles_per_row = (s // tm) if per_request else (b * s // tm)
    outs = _proj(x2d, nrm_mix, sc1, sh1, row0, tiles_per_row, w_in_r, w_in_t, b_gate, rope_tabs, emit_cache, tm)
    qvo, kmt, qd, kd, vdt, gt = outs[:6]
    c0, n0, m0 = states
    hf, hb, cf, nf, mf = _mlstm(qvo.reshape(b, s, 3 * WIDTH), kmt, gt, c0, n0, m0, hp)
    hd = _attn(lp, qd.reshape(b, s, WIDTH), kd.reshape(b, s, WIDTH), vdt, cache, lam_init, tq)
    x1, h2, chl, grpt = _mix(hf.reshape(b * s, WIDTH), hb.reshape(b * s, WIDTH), qvo, hd.reshape(b * s, WIDTH),
                             x2d, g1, sc2, sh2, row0, tiles_per_row, gm, gd, wout, nrm_ffn, wr, br,
                             1.0 - lam_init, tm)
    tm_moe = 2 * tm
    y = _moe(h2, chl, grpt, wgu, wdn, x1, g2, row0, tiles_per_row // 2 if per_request else b * s // tm_moe,
             nrm_final, tm_moe, MOE_BLOCK)
    new_kv = outs[6:] if emit_cache else None
    return y.reshape(b, s, d), new_kv, (cf, nf, mf)


def kernel(x_prompt, x_sample, cache_k, cache_v, state_C, state_n, state_m, c, c_ctx, w_ada, b_ada, norm_mix, norm_ffn, w_in, b_gate, mlstm_norm, diff_norm, diff_lambda, w_out, w_route_group, b_route_group, w_route_expert, b_route_expert, w_gate_up, w_down, final_norm):
    depth = w_in.shape[0]
    assert depth == 1, "the final norm is fused into the last layer's expert kernel"
    bp, sp, d = x_prompt.shape
    bs, ss, _ = x_sample.shape
    rows = 16
    assert bs < rows
    cvec = jnp.zeros((rows, d), F32).at[:bs].set(c).at[bs].set(c_ctx)
    rope_tabs = _rope_tables(ss)
    xp, xs = x_prompt, x_sample
    new_k, new_v, new_c, new_n, new_m = [], [], [], [], []
    for l in range(depth):
        lam_init = 0.8 - 0.6 * math.exp(-0.3 * l)
        mod = _ada(cvec, w_ada[l], b_ada[l])
        mods = [mod[:, i * d:(i + 1) * d].reshape(rows, 1, d) for i in range(6)]
        wl = w_in[l]
        off_dq = 4 * WIDTH + N_GATES
        w_in_r = jnp.concatenate(
            [wl[:, :WIDTH], wl[:, 2 * WIDTH:4 * WIDTH], wl[:, off_dq:]], axis=1).astype(BF16)
        w_in_t = jnp.concatenate(
            [wl[:, WIDTH:2 * WIDTH], wl[:, off_dq + 2 * WIDTH:], wl[:, 4 * WIDTH:off_dq],
             jnp.zeros((d, LANES - N_GATES), F32)], axis=1).T.astype(BF16)
        bg = b_gate[l].reshape(N_GATES, 1)
        wr = jnp.concatenate([w_route_expert[l], w_route_group[l],
                              jnp.zeros((d, LANES - N_EXPERTS - N_GROUPS), F32)], axis=1)
        wr_hi = wr.astype(BF16)
        wr = jnp.concatenate([wr_hi, (wr - wr_hi.astype(F32)).astype(BF16)], axis=1)
        br = jnp.concatenate([b_route_expert[l], b_route_group[l],
                              jnp.zeros((LANES - N_EXPERTS - N_GROUPS,), F32)]).reshape(1, LANES)
        weights = (norm_mix[l].reshape(1, d), norm_ffn[l].reshape(1, d), w_in_r, w_in_t, bg,
                   mlstm_norm[l].reshape(1, WIDTH), diff_norm[l].reshape(1, WIDTH), diff_lambda[l],
                   w_out[l].astype(BF16), wr, br, w_gate_up[l].astype(BF16), w_down[l].astype(BF16),
                   final_norm.reshape(1, d), lam_init)

        zero_states = (jnp.zeros((bp, 2, N_HEADS, HEAD_DIM, HEAD_DIM), F32),
                       jnp.zeros((bp, 2, N_HEADS, HEAD_DIM), F32),
                       jnp.zeros((bp, 2, N_HEADS), F32))
        xp, (nk, nv), (cf, nf, mf) = _layer(xp, mods, bs, False, weights, zero_states, None, None, True,
                                            tm=512, tq=256, hp=2)
        new_k.append(nk.reshape(bp, sp, N_HEADS, 2, QK_DIM))
        new_v.append(nv.reshape(bp, sp, N_HEADS, HEAD_DIM))
        new_c.append(cf)
        new_n.append(nf)
        new_m.append(mf)

        states = (state_C[:, l], state_n[:, l], state_m[:, l])
        cache = (cache_k[:, l].reshape(bs, -1, WIDTH).astype(BF16),
                 cache_v[:, l].reshape(bs, -1, WIDTH).transpose(0, 2, 1).astype(BF16))
        xs, _, _ = _layer(xs, mods, 0, True, weights, states, cache, rope_tabs, False,
                          tm=512, tq=512, hp=4)

    return (xp, xs, jnp.stack(new_k, axis=1), jnp.stack(new_v, axis=1),
            jnp.stack(new_c, axis=1), jnp.stack(new_n, axis=1), jnp.stack(new_m, axis=1))
```

```python
import functools
import math

import jax
import jax.numpy as jnp
from jax import lax
from jax.experimental import pallas as pl
from jax.experimental.pallas import tpu as pltpu

F32 = jnp.float32
BF16 = jnp.bfloat16
HIGHEST = lax.Precision.HIGHEST

EPS = 1e-6
GRID_W = 64
ROPE_THETA = 10000.0
N_HEADS = 4
HEAD_DIM = 128
QK_DIM = 64
WIDTH = N_HEADS * HEAD_DIM
CHUNK = 128
N_GATES = 4 * N_HEADS
N_GROUPS = 4
EXPERTS_PER_GROUP = 4
N_EXPERTS = 16
LANES = 128
SUBLANES = 8
MOE_BLOCK = 256
ATTN_SUB = 128
LOG2E = math.log2(math.e)
VMEM_LIMIT = 56 * 1024 * 1024


def _params(sem):
    return pltpu.CompilerParams(dimension_semantics=sem, vmem_limit_bytes=VMEM_LIMIT)


def _log_sigmoid(x):
    return jnp.minimum(x, 0.0) - jnp.log1p(jnp.exp(-jnp.abs(x)))


def _sigmoid(x):
    return 1.0 / (1.0 + jnp.exp(-x))


def _ada_kernel(c_ref, w_ref, b_ref, o_ref):
    c = c_ref[...]
    s = c * _sigmoid(c)
    o_ref[...] = jnp.dot(s, w_ref[...], preferred_element_type=F32, precision=HIGHEST) + b_ref[...]


def _ada(cvec, w_ada, b_ada):
    rows, d = cvec.shape
    n = w_ada.shape[1]
    tn = 1024
    return pl.pallas_call(
        _ada_kernel,
        grid=(n // tn,),
        in_specs=[pl.BlockSpec((rows, d), lambda j: (0, 0)),
                  pl.BlockSpec((d, tn), lambda j: (0, j)),
                  pl.BlockSpec((1, tn), lambda j: (0, j))],
        out_specs=pl.BlockSpec((rows, tn), lambda j: (0, j)),
        out_shape=jax.ShapeDtypeStruct((rows, n), F32),
        compiler_params=_params(("arbitrary",)),
        name="ada",
    )(cvec, w_ada, b_ada.reshape(1, n))


def _proj_kernel(*refs, rope, emit_cache):
    x_ref, nrm_ref, sc_ref, sh_ref, w_ref, wt_ref, bg_ref = refs[:7]
    refs = refs[7:]
    if rope:
        cos_ref, sa_ref, sb_ref = refs[:3]
        refs = refs[3:]
    m_ref, kmt_ref, q_ref, k_ref, vt_ref, gt_ref = refs[:6]
    refs = refs[6:]
    if emit_cache:
        newk_ref, newv_ref = refs

    x = x_ref[...]
    ms = jnp.mean(x * x, axis=-1, keepdims=True)
    y = x * lax.rsqrt(ms + EPS) * nrm_ref[...]
    h = (y * (1.0 + sc_ref[0]) + sh_ref[0]).astype(BF16)

    def section(j):
        return jnp.dot(h, w_ref[:, j * WIDTH:(j + 1) * WIDTH], preferred_element_type=F32)

    def rotate(p):
        cos, sa, sb = cos_ref[...], sa_ref[...], sb_ref[...]
        outs = []
        for j in range(N_HEADS):
            xb = p[:, j * LANES:(j + 1) * LANES]
            outs.append(xb * cos + pltpu.roll(xb, LANES - 16, 1) * sa + pltpu.roll(xb, 16, 1) * sb)
        return jnp.concatenate(outs, axis=1)

    def section_t(j, n):
        return lax.dot_general(wt_ref[j * WIDTH:j * WIDTH + n, :], h, (((1,), (1,)), ((), ())),
                               preferred_element_type=F32)

    for j in range(3):
        m_ref[:, j * WIDTH:(j + 1) * WIDTH] = section(j).astype(BF16)
    kmt_ref[...] = (section_t(0, WIDTH) * (HEAD_DIM ** -0.5)).astype(BF16)
    pq = section(3)
    if rope:
        pq = rotate(pq)
    q_ref[...] = (pq * (LOG2E * QK_DIM ** -0.5)).astype(BF16)
    pk = section(4)
    if emit_cache:
        newk_ref[...] = pk
    if rope:
        pk = rotate(pk)
    k_ref[...] = pk.astype(BF16)
    if emit_cache:
        newv_ref[...] = section(5)
    vt_ref[...] = section_t(1, WIDTH).astype(BF16)
    gt_ref[...] = section_t(2, LANES)[:N_GATES, :] + bg_ref[...]


def _proj(x2d, nrm, sc, sh, row0, tiles_per_row, w_r, w_t, b_gate, rope_tabs, emit_cache, tm):
    t, d = x2d.shape
    nt = t // tm
    rope = rope_tabs is not None
    row_map = lambda i: (row0 + i // tiles_per_row, 0, 0)
    tok = lambda i: (i, 0)
    feat = lambda i: (0, i)
    in_specs = [pl.BlockSpec((tm, d), tok),
                pl.BlockSpec((1, d), lambda i: (0, 0)),
                pl.BlockSpec((1, 1, d), row_map),
                pl.BlockSpec((1, 1, d), row_map),
                pl.BlockSpec(w_r.shape, lambda i: (0, 0)),
                pl.BlockSpec(w_t.shape, lambda i: (0, 0)),
                pl.BlockSpec((N_GATES, 1), lambda i: (0, 0))]
    args = [x2d, nrm, sc, sh, w_r, w_t, b_gate]
    if rope:
        tps = rope_tabs[0].shape[0] // tm
        for tab in rope_tabs:
            in_specs.append(pl.BlockSpec((tm, LANES), lambda i: (i % tps, 0)))
            args.append(tab)
    out_shape = [jax.ShapeDtypeStruct((t, 3 * WIDTH), BF16),
                 jax.ShapeDtypeStruct((WIDTH, t), BF16),
                 jax.ShapeDtypeStruct((t, WIDTH), BF16),
                 jax.ShapeDtypeStruct((t, WIDTH), BF16),
                 jax.ShapeDtypeStruct((WIDTH, t), BF16),
                 jax.ShapeDtypeStruct((N_GATES, t), F32)]
    out_specs = [pl.BlockSpec((tm, 3 * WIDTH), tok),
                 pl.BlockSpec((WIDTH, tm), feat),
                 pl.BlockSpec((tm, WIDTH), tok),
                 pl.BlockSpec((tm, WIDTH), tok),
                 pl.BlockSpec((WIDTH, tm), feat),
                 pl.BlockSpec((N_GATES, tm), feat)]
    if emit_cache:
        out_shape += [jax.ShapeDtypeStruct((t, WIDTH), F32)] * 2
        out_specs += [pl.BlockSpec((tm, WIDTH), tok)] * 2
    return pl.pallas_call(
        functools.partial(_proj_kernel, rope=rope, emit_cache=emit_cache),
        grid=(nt,),
        in_specs=in_specs,
        out_specs=out_specs,
        out_shape=out_shape,
        compiler_params=_params(("parallel",)),
        name="proj_rope" if rope else "proj_ctx",
    )(*args)


def _rope_tables(seq):
    t = jnp.arange(seq)
    row = (t // GRID_W).astype(F32)[:, None]
    col = (t % GRID_W).astype(F32)[:, None]
    lane = jnp.arange(LANES)
    freqs = ROPE_THETA ** (-(lane % 16).astype(F32) / 16.0)
    pos = jnp.where(((lane % 64) < 32)[None, :], row, col)
    ang = pos * freqs[None, :]
    cos, sin = jnp.cos(ang), jnp.sin(ang)
    first = ((lane % 32) < 16)[None, :]
    return cos, jnp.where(first, -sin, 0.0), jnp.where(first, 0.0, sin)


def _lane_scan(x, pos, op, forward):
    n = x.shape[-1]
    shift = 1
    while shift < n:
        if forward:
            y = pltpu.roll(x, shift, 1)
            ok = pos >= shift
        else:
            y = pltpu.roll(x, n - shift, 1)
            ok = pos < n - shift
        x = jnp.where(ok, op(x, y), x)
        shift *= 2
    return x


def _mlstm_kernel(q_ref, kt_ref, v_ref, g_ref, c0_ref, n0_ref, m0_ref,
                  hf_ref, hb_ref, cf_ref, nf_ref, mf_ref,
                  st_ref, r_ref, mp_ref, mn_ref, dc_ref, tab_ref, *, nc, hp):
    L = CHUNK
    rows = nc * hp
    pos = lax.broadcasted_iota(jnp.int32, (rows, L), 1)
    tab_ref[...] = jnp.zeros_like(tab_ref)
    for d in range(2):
        i_pre = g_ref[0, 0, (2 * d) * rows:(2 * d + 1) * rows, :]
        lf = _log_sigmoid(g_ref[0, 0, (2 * d + 1) * rows:(2 * d + 2) * rows, :])
        pre = _lane_scan(lf, pos, jnp.add, True)
        suf = _lane_scan(lf, pos, jnp.add, False)
        tot = pre + suf - lf
        b = pre if d == 0 else suf
        r = i_pre - b
        r_max = _lane_scan(r, pos, jnp.maximum, d == 0)
        gk = tot - b + i_pre
        g_max = jnp.maximum(_lane_scan(gk, pos, jnp.maximum, True), _lane_scan(gk, pos, jnp.maximum, False))
        m = m0_ref[0, d, :, 0, :]
        for c in (range(nc) if d == 0 else reversed(range(nc))):
            sl = slice(c * hp, (c + 1) * hp)
            mp_ref[d, sl, :] = m
            m = jnp.maximum(tot[sl] + m, g_max[sl])
            mn_ref[d, sl, :] = m
        mf_ref[0, d, :, 0, :] = m
        m_prev = mp_ref[d]
        m_new = mn_ref[d]
        m_row = jnp.maximum(m_prev, r_max)
        r_ref[d] = r
        dc_ref[d] = jnp.exp(tot + m_prev - m_new)
        w_k = jnp.exp(gk - m_new)
        m_t = b + m_row
        for c in range(nc):
            src = slice(c * hp, (c + 1) * hp)
            slot = c if d == 0 else nc - 1 - c
            for qi, val in enumerate((m_row, m_t, w_k)):
                base = (2 * qi + d) * hp
                tab_ref[slot, base:base + hp, :] = val[src]

    for d in range(2):
        for hh in range(hp):
            st_ref[d, hh, :, :HEAD_DIM] = jnp.transpose(c0_ref[0, d, hh])
            st_ref[d, hh, :, HEAD_DIM:] = jnp.transpose(jnp.broadcast_to(n0_ref[0, d, hh], (HEAD_DIM, HEAD_DIM)))

    rr = lax.broadcasted_iota(jnp.int32, (L, L), 0)
    cc = lax.broadcasted_iota(jnp.int32, (L, L), 1)
    masks = (cc <= rr, cc >= rr)
    ones = jnp.ones((L, HEAD_DIM), BF16)

    def body(c, carry):
        tab = jnp.transpose(tab_ref[c])
        for d in range(2):
            ci = c if d == 0 else nc - 1 - c
            off = pl.multiple_of(ci * L, L)
            for hh in range(hp):
                lanes = slice(hh * HEAD_DIM, (hh + 1) * HEAD_DIM)
                row = ci * hp + hh
                col = d * hp + hh
                m_col = jnp.broadcast_to(tab[:, col:col + 1], (L, HEAD_DIM))
                mt_col = jnp.broadcast_to(tab[:, 2 * hp + col:2 * hp + col + 1], (L, HEAD_DIM))
                wk_col = jnp.broadcast_to(tab[:, 4 * hp + col:4 * hp + col + 1], (L, HEAD_DIM))
                r_row = r_ref[d, pl.ds(row, 1), :]
                m_prev = mp_ref[d, pl.ds(row, 1), :]
                decay = dc_ref[d, pl.ds(row, 1), :]
                q = q_ref[0, pl.ds(off, L), lanes]
                v = v_ref[0, pl.ds(off, L), lanes]
                kt = kt_ref[lanes, pl.ds(off, L)]
                state = st_ref[d, hh]

                w_intra = jnp.exp(jnp.where(masks[d], r_row - m_col, -jnp.inf))
                s = (jnp.dot(q, kt, preferred_element_type=F32) * w_intra).astype(BF16)
                intra = jnp.dot(s, jnp.concatenate([v, ones], axis=1), preferred_element_type=F32)
                inter = jnp.dot(q, state.astype(BF16), preferred_element_type=F32)
                w_state = jnp.exp(m_prev - m_col)
                both = intra + jnp.concatenate([w_state, w_state], axis=1) * inter
                den = jnp.maximum(jnp.abs(both[:, HEAD_DIM:]), jnp.exp(-mt_col))
                h = both[:, :HEAD_DIM] / den

                vw = (v.astype(F32) * wk_col).astype(BF16)
                upd = jnp.dot(kt, jnp.concatenate([vw, wk_col.astype(BF16)], axis=1),
                              preferred_element_type=F32)
                st_ref[d, hh] = jnp.concatenate([decay, decay], axis=1) * state + upd
                out_ref = hf_ref if d == 0 else hb_ref
                out_ref[0, pl.ds(off, L), lanes] = h.astype(BF16)
        return carry

    lax.fori_loop(0, nc, body, 0)

    for d in range(2):
        for hh in range(hp):
            state = st_ref[d, hh]
            cf_ref[0, d, hh] = jnp.transpose(state[:, :HEAD_DIM])
            nf_ref[0, d, hh] = jnp.transpose(state[:, HEAD_DIM:])[0:1, :]


def _mlstm(qvo, kt, gt, c0, n0, m0, hp):
    b, s, _ = qvo.shape
    nc = s // CHUNK
    hg = N_HEADS // hp
    rows = nc * hp
    g6 = gt.reshape(4, hg, hp, b, nc, CHUNK).transpose(3, 1, 0, 4, 2, 5).reshape(b, hg, 4 * rows, CHUNK)
    n0r = n0.reshape(b, 2, N_HEADS, 1, HEAD_DIM)
    m0r = jnp.broadcast_to(m0[..., None, None], (b, 2, N_HEADS, 1, HEAD_DIM))
    blk = hp * HEAD_DIM
    nblk = WIDTH // blk
    state_spec = lambda w: pl.BlockSpec((1, 2, hp, w, HEAD_DIM), lambda i, j: (i, 0, j, 0, 0))
    in_specs = [pl.BlockSpec((1, s, blk), lambda i, j: (i, 0, j)),
                pl.BlockSpec((blk, s), lambda i, j: (j, i)),
                pl.BlockSpec((1, s, blk), lambda i, j: (i, 0, nblk + j)),
                pl.BlockSpec((1, 1, 4 * rows, CHUNK), lambda i, j: (i, j, 0, 0)),
                state_spec(HEAD_DIM), state_spec(1), state_spec(1)]
    out_specs = [pl.BlockSpec((1, s, blk), lambda i, j: (i, 0, j)),
                 pl.BlockSpec((1, s, blk), lambda i, j: (i, 0, j)),
                 state_spec(HEAD_DIM), state_spec(1), state_spec(1)]
    out_shape = [jax.ShapeDtypeStruct((b, s, WIDTH), BF16),
                 jax.ShapeDtypeStruct((b, s, WIDTH), BF16),
                 jax.ShapeDtypeStruct((b, 2, N_HEADS, HEAD_DIM, HEAD_DIM), F32),
                 jax.ShapeDtypeStruct((b, 2, N_HEADS, 1, HEAD_DIM), F32),
                 jax.ShapeDtypeStruct((b, 2, N_HEADS, 1, HEAD_DIM), F32)]
    scratch = [pltpu.VMEM((2, hp, HEAD_DIM, 2 * HEAD_DIM), F32),
               pltpu.VMEM((2, rows, CHUNK), F32),
               pltpu.VMEM((2, rows, CHUNK), F32),
               pltpu.VMEM((2, rows, CHUNK), F32),
               pltpu.VMEM((2, rows, CHUNK), F32),
               pltpu.VMEM((nc, LANES, CHUNK), F32)]
    hf, hb, cf, nf, mf = pl.pallas_call(
        functools.partial(_mlstm_kernel, nc=nc, hp=hp),
        grid=(b, hg),
        in_specs=in_specs,
        out_specs=out_specs,
        out_shape=out_shape,
        scratch_shapes=scratch,
        compiler_params=_params(("parallel", "parallel")),
        name="mlstm",
    )(qvo, kt, qvo, g6, c0, n0r, m0r)
    return hf, hb, cf, nf[:, :, :, 0, :], mf[:, :, :, 0, 0]


def _attn_kernel(*refs, sub, n_new, n_cache, lam_init):
    if n_cache:
        lp_ref, q_ref, k_ref, kc_ref, vt_ref, vct_ref, o_ref, s_even_ref, s_odd_ref = refs
    else:
        lp_ref, q_ref, k_ref, vt_ref, o_ref, s_even_ref, s_odd_ref = refs
    lp = lp_ref[...]
    lam = (jnp.exp(jnp.sum(lp[0:1] * lp[1:2], axis=-1, keepdims=True))
           - jnp.exp(jnp.sum(lp[2:3] * lp[3:4], axis=-1, keepdims=True)) + lam_init)
    lane = lax.broadcasted_iota(jnp.int32, (sub, HEAD_DIM), 1)
    n_sub = n_new // sub
    nt = (((1,), (1,)), ((), ()))

    def scores(idx, s_ref):
        r0 = pl.multiple_of(idx * sub, sub)
        q = q_ref[0, pl.ds(r0, sub), :]
        zero = jnp.zeros_like(q)
        qq = jnp.concatenate([jnp.where(lane < QK_DIM, q, zero), jnp.where(lane >= QK_DIM, q, zero)], axis=0)
        s = lax.dot_general(k_ref[0], qq, nt, preferred_element_type=F32)
        s_ref[0:n_new, :] = s
        m = jnp.max(s, axis=0, keepdims=True)
        if n_cache:
            sc = lax.dot_general(kc_ref[0], qq, nt, preferred_element_type=F32)
            s_ref[n_new:n_new + n_cache, :] = sc
            m = jnp.maximum(m, jnp.max(sc, axis=0, keepdims=True))
        return m

    def finish(idx, s_ref, m):
        r0 = pl.multiple_of(idx * sub, sub)
        e = jnp.exp2(s_ref[...] - m)
        l = jnp.sum(e, axis=0, keepdims=True)
        eb = e.astype(BF16)
        o = jnp.dot(vt_ref[...], eb[0:n_new], preferred_element_type=F32)
        if n_cache:
            o = o + jnp.dot(vct_ref[0], eb[n_new:n_new + n_cache], preferred_element_type=F32)
        o = o / l
        out_t = o[:, :sub] - lam * o[:, sub:]
        o_ref[0, pl.ds(r0, sub), :] = jnp.transpose(out_t).astype(BF16)

    def pair(i, m_even):
        a = 2 * i
        m_odd = scores(a + 1, s_odd_ref)
        finish(a, s_even_ref, m_even)
        m_even_next = scores(jnp.minimum(a + 2, n_sub - 1), s_even_ref)
        finish(a + 1, s_odd_ref, m_odd)
        return m_even_next

    lax.fori_loop(0, n_sub // 2, pair, scores(0, s_even_ref))


def _attn(lp, q, k, vt, cache, lam_init):
    b, s, _ = q.shape
    assert s % (2 * ATTN_SUB) == 0
    n_cache = 0 if cache is None else cache[0].shape[1]
    in_specs = [pl.BlockSpec(lp.shape, lambda i, h: (0, 0)),
                pl.BlockSpec((1, s, HEAD_DIM), lambda i, h: (i, 0, h)),
                pl.BlockSpec((1, s, HEAD_DIM), lambda i, h: (i, 0, h))]
    args = [lp, q, k]
    if n_cache:
        in_specs.append(pl.BlockSpec((1, n_cache, HEAD_DIM), lambda i, h: (i, 0, h)))
        args.append(cache[0])
    in_specs.append(pl.BlockSpec((HEAD_DIM, s), lambda i, h: (h, i)))
    args.append(vt)
    if n_cache:
        in_specs.append(pl.BlockSpec((1, HEAD_DIM, n_cache), lambda i, h: (i, h, 0)))
        args.append(cache[1])
    keys = s + n_cache
    return pl.pallas_call(
        functools.partial(_attn_kernel, sub=ATTN_SUB, n_new=s, n_cache=n_cache, lam_init=lam_init),
        grid=(b, N_HEADS),
        in_specs=in_specs,
        out_specs=pl.BlockSpec((1, s, HEAD_DIM), lambda i, h: (i, 0, h)),
        out_shape=jax.ShapeDtypeStruct((b, s, WIDTH), BF16),
        scratch_shapes=[pltpu.VMEM((keys, 2 * ATTN_SUB), F32), pltpu.VMEM((keys, 2 * ATTN_SUB), F32)],
        compiler_params=_params(("parallel", "parallel")),
        name="diff_attn",
    )(*args)


def _head_norm(x):
    outs = []
    for j in range(N_HEADS):
        xb = x[:, j * HEAD_DIM:(j + 1) * HEAD_DIM]
        outs.append(xb * lax.rsqrt(jnp.mean(xb * xb, axis=-1, keepdims=True) + EPS))
    return jnp.concatenate(outs, axis=1)


def _mix_kernel(hf_ref, hb_ref, om_ref, hd_ref, x_ref, g1_ref, sc2_ref, sh2_ref, gm_ref, gd_ref,
                wout_ref, nrm_ref, wr_ref, br_ref, x1_ref, h2_ref, chl_ref, grpt_ref, *, diff_scale):
    hm = _head_norm(hf_ref[...].astype(F32) + hb_ref[...].astype(F32))
    hm = hm * gm_ref[...] * _sigmoid(om_ref[...].astype(F32))
    hd = _head_norm(hd_ref[...].astype(F32)) * gd_ref[...] * diff_scale
    mix = (jnp.dot(hm.astype(BF16), wout_ref[:WIDTH, :], preferred_element_type=F32)
           + jnp.dot(hd.astype(BF16), wout_ref[WIDTH:, :], preferred_element_type=F32))
    x1 = x_ref[...] + g1_ref[0] * mix
    x1_ref[...] = x1
    ms = jnp.mean(x1 * x1, axis=-1, keepdims=True)
    h2 = (x1 * lax.rsqrt(ms + EPS) * nrm_ref[...]) * (1.0 + sc2_ref[0]) + sh2_ref[0]
    h2_hi = h2.astype(BF16)
    h2_ref[...] = h2_hi

    h2_lo = (h2 - h2_hi.astype(F32)).astype(BF16)
    parts = (jnp.dot(h2_hi, wr_ref[...], preferred_element_type=F32)
             + jnp.dot(h2_lo, wr_ref[...], preferred_element_type=F32))
    logits = parts[:, :LANES] + parts[:, LANES:] + br_ref[...]
    lane = lax.broadcasted_iota(jnp.int32, logits.shape, 1).astype(F32)
    big = float(LANES)
    is_grp = (lane >= N_EXPERTS) & (lane < N_EXPERTS + N_GROUPS)
    lg = jnp.where(is_grp, logits, -jnp.inf)
    mx = jnp.max(lg, axis=-1, keepdims=True)
    p_grp = 1.0 / jnp.sum(jnp.exp(lg - mx), axis=-1, keepdims=True)
    grp = jnp.min(jnp.where(lg == mx, lane, big), axis=-1, keepdims=True) - N_EXPERTS
    base = grp * EXPERTS_PER_GROUP
    sel = (lane >= base) & (lane < base + EXPERTS_PER_GROUP)
    le = jnp.where(sel, logits, -jnp.inf)
    ee = jnp.exp(le - jnp.max(le, axis=-1, keepdims=True))
    pe = ee / jnp.sum(ee, axis=-1, keepdims=True)
    top1 = jnp.max(pe, axis=-1, keepdims=True)
    idx1 = jnp.min(jnp.where(sel & (pe == top1), lane, big), axis=-1, keepdims=True)
    rest = sel & (lane != idx1)
    top2 = jnp.max(jnp.where(rest, pe, -1.0), axis=-1, keepdims=True)
    idx2 = jnp.min(jnp.where(rest & (pe == top2), lane, big), axis=-1, keepdims=True)
    denom = top1 + top2
    w1 = top1 / denom * p_grp
    w2 = top2 / denom * p_grp
    chl = jnp.zeros_like(logits)
    for j in range(EXPERTS_PER_GROUP):
        cj = jnp.where(idx1 == base + j, w1, 0.0) + jnp.where(idx2 == base + j, w2, 0.0)
        hi = cj.astype(BF16).astype(F32)
        chl = chl + jnp.where(lane == j, hi, 0.0) + jnp.where(lane == EXPERTS_PER_GROUP + j, cj - hi, 0.0)
    chl_ref[...] = chl.astype(BF16)
    grpt_ref[...] = jnp.transpose(jnp.broadcast_to(grp, logits.shape))[:SUBLANES, :]


def _mix(hf, hb, qvo, hd, x2d, g1, sc2, sh2, row0, tiles_per_row, gm, gd, wout, nrm, wr, br, diff_scale, tm):
    t, d = x2d.shape
    row_map = lambda i: (row0 + i // tiles_per_row, 0, 0)
    tok = lambda i: (i, 0)
    full = lambda i: (0, 0)
    return pl.pallas_call(
        functools.partial(_mix_kernel, diff_scale=diff_scale),
        grid=(t // tm,),
        in_specs=[pl.BlockSpec((tm, WIDTH), tok),
                  pl.BlockSpec((tm, WIDTH), tok),
                  pl.BlockSpec((tm, WIDTH), lambda i: (i, 2)),
                  pl.BlockSpec((tm, WIDTH), tok),
                  pl.BlockSpec((tm, d), tok),
                  pl.BlockSpec((1, 1, d), row_map),
                  pl.BlockSpec((1, 1, d), row_map),
                  pl.BlockSpec((1, 1, d), row_map),
                  pl.BlockSpec((1, WIDTH), full),
                  pl.BlockSpec((1, WIDTH), full),
                  pl.BlockSpec(wout.shape, full),
                  pl.BlockSpec((1, d), full),
                  pl.BlockSpec(wr.shape, full),
                  pl.BlockSpec(br.shape, full)],
        out_specs=[pl.BlockSpec((tm, d), tok),
                   pl.BlockSpec((tm, d), tok),
                   pl.BlockSpec((tm, LANES), tok),
                   pl.BlockSpec((SUBLANES, tm), lambda i: (0, i))],
        out_shape=[jax.ShapeDtypeStruct((t, d), F32),
                   jax.ShapeDtypeStruct((t, d), BF16),
                   jax.ShapeDtypeStruct((t, LANES), BF16),
                   jax.ShapeDtypeStruct((SUBLANES, t), F32)],
        compiler_params=_params(("parallel",)),
        name="mix_router",
    )(hf, hb, qvo, hd, x2d, g1, sc2, sh2, gm, gd, wout, nrm, wr, br)


def _moe_kernel(h2_ref, chl_ref, grpt_ref, wgu_ref, wdn_ref, x1_ref, g2_ref, nrm_ref, y_ref,
                acc_ref, rank_ref, *, hidden, blk):
    g = pl.program_id(1)
    tm = h2_ref.shape[0]

    @pl.when(g == 0)
    def _():
        acc_ref[...] = jnp.zeros_like(acc_ref)
        gid = lax.broadcasted_iota(jnp.int32, (SUBLANES, tm), 0).astype(F32)
        member = grpt_ref[...] == gid
        src = lax.broadcasted_iota(jnp.int32, (tm, tm), 0)
        dst = lax.broadcasted_iota(jnp.int32, (tm, tm), 1)
        before = jnp.where(src < dst, 1.0, 0.0).astype(BF16)
        rank = jnp.dot(jnp.where(member, 1.0, 0.0).astype(BF16), before, preferred_element_type=F32)
        rank_ref[...] = jnp.where(member, rank, -1.0)

    rank_g = rank_ref[pl.ds(g, 1), :]
    n_g = jnp.max(rank_g) + 1.0

    for j in range(tm // blk):
        @pl.when(n_g > j * blk)
        def _():
            slot = lax.broadcasted_iota(jnp.int32, (blk, tm), 0).astype(F32) + float(j * blk)
            pick = jnp.where(rank_g == slot, 1.0, 0.0).astype(BF16)
            xg = jnp.dot(pick, h2_ref[...], preferred_element_type=F32).astype(BF16)
            cg = jnp.dot(pick, chl_ref[...], preferred_element_type=F32)
            out = jnp.zeros((blk, acc_ref.shape[1]), F32)
            for e in range(EXPERTS_PER_GROUP):
                gu = jnp.dot(xg, wgu_ref[e], preferred_element_type=F32)
                a, u = gu[:, :hidden], gu[:, hidden:]
                ce = cg[:, e:e + 1] + cg[:, EXPERTS_PER_GROUP + e:EXPERTS_PER_GROUP + e + 1]
                act = (a * _sigmoid(a) * u * ce).astype(BF16)
                out = out + jnp.dot(act, wdn_ref[e], preferred_element_type=F32)
            acc_ref[...] += lax.dot_general(pick, out.astype(BF16), (((0,), (0,)), ((), ())),
                                            preferred_element_type=F32)

    @pl.when(g == pl.num_programs(1) - 1)
    def _():
        x2 = x1_ref[...] + g2_ref[0] * acc_ref[...]
        ms = jnp.mean(x2 * x2, axis=-1, keepdims=True)
        y_ref[...] = x2 * lax.rsqrt(ms + EPS) * nrm_ref[...]


def _moe(h2, chl, grpt, wgu, wdn, x1, g2, row0, tiles_per_row, nrm, tm, blk):
    t, d = x1.shape
    _, _, two_h = wgu.shape
    hidden = two_h // 2
    row_map = lambda i, g: (row0 + i // tiles_per_row, 0, 0)
    tok = lambda i, g: (i, 0)
    return pl.pallas_call(
        functools.partial(_moe_kernel, hidden=hidden, blk=blk),
        grid=(t // tm, N_GROUPS),
        in_specs=[pl.BlockSpec((tm, d), tok),
                  pl.BlockSpec((tm, LANES), tok),
                  pl.BlockSpec((SUBLANES, tm), lambda i, g: (0, i)),
                  pl.BlockSpec((EXPERTS_PER_GROUP, d, two_h), lambda i, g: (g, 0, 0)),
                  pl.BlockSpec((EXPERTS_PER_GROUP, hidden, d), lambda i, g: (g, 0, 0)),
                  pl.BlockSpec((tm, d), tok),
                  pl.BlockSpec((1, 1, d), row_map),
                  pl.BlockSpec((1, d), lambda i, g: (0, 0))],
        out_specs=pl.BlockSpec((tm, d), tok),
        out_shape=jax.ShapeDtypeStruct((t, d), F32),
        scratch_shapes=[pltpu.VMEM((tm, d), F32), pltpu.VMEM((SUBLANES, tm), F32)],
        compiler_params=_params(("parallel", "arbitrary")),
        name="moe_experts",
    )(h2, chl, grpt, wgu, wdn, x1, g2, nrm)


def _layer(x, mods, row0, per_request, weights, states, cache, rope_tabs, emit_cache, tm, hp):
    b, s, d = x.shape
    sh1, sc1, g1, sh2, sc2, g2 = mods
    (nrm_mix, nrm_ffn, w_in_r, w_in_t, b_gate, gm, gd, lp, wout, wr, br, wgu, wdn, nrm_final, lam_init) = weights
    x2d = x.reshape(b * s, d)
    tiles_per_row = (s // tm) if per_request else (b * s // tm)
    outs = _proj(x2d, nrm_mix, sc1, sh1, row0, tiles_per_row, w_in_r, w_in_t, b_gate, rope_tabs, emit_cache, tm)
    qvo, kmt, qd, kd, vdt, gt = outs[:6]
    c0, n0, m0 = states
    hf, hb, cf, nf, mf = _mlstm(qvo.reshape(b, s, 3 * WIDTH), kmt, gt, c0, n0, m0, hp)
    hd = _attn(lp, qd.reshape(b, s, WIDTH), kd.reshape(b, s, WIDTH), vdt, cache, lam_init)
    x1, h2, chl, grpt = _mix(hf.reshape(b * s, WIDTH), hb.reshape(b * s, WIDTH), qvo, hd.reshape(b * s, WIDTH),
                             x2d, g1, sc2, sh2, row0, tiles_per_row, gm, gd, wout, nrm_ffn, wr, br,
                             1.0 - lam_init, tm)
    tm_moe = 2 * tm
    y = _moe(h2, chl, grpt, wgu, wdn, x1, g2, row0, tiles_per_row // 2 if per_request else b * s // tm_moe,
             nrm_final, tm_moe, MOE_BLOCK)
    new_kv = outs[6:] if emit_cache else None
    return y.reshape(b, s, d), new_kv, (cf, nf, mf)


def kernel(x_prompt, x_sample, cache_k, cache_v, state_C, state_n, state_m, c, c_ctx, w_ada, b_ada, norm_mix, norm_ffn, w_in, b_gate, mlstm_norm, diff_norm, diff_lambda, w_out, w_route_group, b_route_group, w_route_expert, b_route_expert, w_gate_up, w_down, final_norm):
    depth = w_in.shape[0]
    assert depth == 1, "the final norm is fused into the last layer's expert kernel"
    bp, sp, d = x_prompt.shape
    bs, ss, _ = x_sample.shape
    rows = 16
    assert bs < rows
    cvec = jnp.zeros((rows, d), F32).at[:bs].set(c).at[bs].set(c_ctx)
    rope_tabs = _rope_tables(ss)
    xp, xs = x_prompt, x_sample
    new_k, new_v, new_c, new_n, new_m = [], [], [], [], []
    for l in range(depth):
        lam_init = 0.8 - 0.6 * math.exp(-0.3 * l)
        mod = _ada(cvec, w_ada[l], b_ada[l])
        mods = [mod[:, i * d:(i + 1) * d].reshape(rows, 1, d) for i in range(6)]
        wl = w_in[l]
        off_dq = 4 * WIDTH + N_GATES
        w_in_r = jnp.concatenate(
            [wl[:, :WIDTH], wl[:, 2 * WIDTH:4 * WIDTH], wl[:, off_dq:]], axis=1).astype(BF16)
        w_in_t = jnp.concatenate(
            [wl[:, WIDTH:2 * WIDTH], wl[:, off_dq + 2 * WIDTH:], wl[:, 4 * WIDTH:off_dq],
             jnp.zeros((d, LANES - N_GATES), F32)], axis=1).T.astype(BF16)
        bg = b_gate[l].reshape(N_GATES, 1)
        wr = jnp.concatenate([w_route_expert[l], w_route_group[l],
                              jnp.zeros((d, LANES - N_EXPERTS - N_GROUPS), F32)], axis=1)
        wr_hi = wr.astype(BF16)
        wr = jnp.concatenate([wr_hi, (wr - wr_hi.astype(F32)).astype(BF16)], axis=1)
        br = jnp.concatenate([b_route_expert[l], b_route_group[l],
                              jnp.zeros((LANES - N_EXPERTS - N_GROUPS,), F32)]).reshape(1, LANES)
        weights = (norm_mix[l].reshape(1, d), norm_ffn[l].reshape(1, d), w_in_r, w_in_t, bg,
                   mlstm_norm[l].reshape(1, WIDTH), diff_norm[l].reshape(1, WIDTH), diff_lambda[l],
                   w_out[l].astype(BF16), wr, br, w_gate_up[l].astype(BF16), w_down[l].astype(BF16),
                   final_norm.reshape(1, d), lam_init)

        zero_states = (jnp.zeros((bp, 2, N_HEADS, HEAD_DIM, HEAD_DIM), F32),
                       jnp.zeros((bp, 2, N_HEADS, HEAD_DIM), F32),
                       jnp.zeros((bp, 2, N_HEADS), F32))
        xp, (nk, nv), (cf, nf, mf) = _layer(xp, mods, bs, False, weights, zero_states, None, None, True,
                                            tm=512, hp=2)
        new_k.append(nk.reshape(bp, sp, N_HEADS, 2, QK_DIM))
        new_v.append(nv.reshape(bp, sp, N_HEADS, HEAD_DIM))
        new_c.append(cf)
        new_n.append(nf)
        new_m.append(mf)

        states = (state_C[:, l], state_n[:, l], state_m[:, l])
        cache = (cache_k[:, l].reshape(bs, -1, WIDTH).astype(BF16),
                 cache_v[:, l].reshape(bs, -1, WIDTH).transpose(0, 2, 1).astype(BF16))
        xs, _, _ = _layer(xs, mods, 0, True, weights, states, cache, rope_tabs, False,
                          tm=512, hp=4)

    return (xp, xs, jnp.stack(new_k, axis=1), jnp.stack(new_v, axis=1),
            jnp.stack(new_c, axis=1), jnp.stack(new_n, axis=1), jnp.stack(new_m, axis=1))
```

```python
import functools
import math

import jax
import jax.numpy as jnp
from jax import lax
from jax.experimental import pallas as pl
from jax.experimental.pallas import tpu as pltpu

F32 = jnp.float32
BF16 = jnp.bfloat16
HIGHEST = lax.Precision.HIGHEST

EPS = 1e-6
GRID_W = 64
ROPE_THETA = 10000.0
N_HEADS = 4
HEAD_DIM = 128
QK_DIM = 64
WIDTH = N_HEADS * HEAD_DIM
CHUNK = 128
N_GATES = 4 * N_HEADS
N_GROUPS = 4
EXPERTS_PER_GROUP = 4
N_EXPERTS = 16
LANES = 128
SUBLANES = 8
MOE_BLOCK = 256
ATTN_SUB = 128
LOG2E = math.log2(math.e)
VMEM_LIMIT = 56 * 1024 * 1024


def _params(sem):
    return pltpu.CompilerParams(dimension_semantics=sem, vmem_limit_bytes=VMEM_LIMIT)


def _log_sigmoid(x):
    return jnp.minimum(x, 0.0) - jnp.log1p(jnp.exp(-jnp.abs(x)))


def _sigmoid(x):
    return 1.0 / (1.0 + jnp.exp(-x))


def _ada_kernel(c_ref, w_ref, b_ref, o_ref):
    c = c_ref[...]
    s = c * _sigmoid(c)
    o_ref[...] = jnp.dot(s, w_ref[...], preferred_element_type=F32, precision=HIGHEST) + b_ref[...]


def _ada(cvec, w_ada, b_ada):
    rows, d = cvec.shape
    n = w_ada.shape[1]
    tn = 1024
    return pl.pallas_call(
        _ada_kernel,
        grid=(n // tn,),
        in_specs=[pl.BlockSpec((rows, d), lambda j: (0, 0)),
                  pl.BlockSpec((d, tn), lambda j: (0, j)),
                  pl.BlockSpec((1, tn), lambda j: (0, j))],
        out_specs=pl.BlockSpec((rows, tn), lambda j: (0, j)),
        out_shape=jax.ShapeDtypeStruct((rows, n), F32),
        compiler_params=_params(("arbitrary",)),
        name="ada",
    )(cvec, w_ada, b_ada.reshape(1, n))


def _proj_kernel(*refs, rope, emit_cache):
    x_ref, nrm_ref, sc_ref, sh_ref, w_ref, wt_ref, bg_ref = refs[:7]
    refs = refs[7:]
    if rope:
        cos_ref, sa_ref, sb_ref = refs[:3]
        refs = refs[3:]
    m_ref, kmt_ref, q_ref, k_ref, vt_ref, gt_ref = refs[:6]
    refs = refs[6:]
    if emit_cache:
        newk_ref, newv_ref = refs

    x = x_ref[...]
    ms = jnp.mean(x * x, axis=-1, keepdims=True)
    y = x * lax.rsqrt(ms + EPS) * nrm_ref[...]
    h = (y * (1.0 + sc_ref[0]) + sh_ref[0]).astype(BF16)

    def section(j):
        return jnp.dot(h, w_ref[:, j * WIDTH:(j + 1) * WIDTH], preferred_element_type=F32)

    def rotate(p):
        cos, sa, sb = cos_ref[...], sa_ref[...], sb_ref[...]
        outs = []
        for j in range(N_HEADS):
            xb = p[:, j * LANES:(j + 1) * LANES]
            outs.append(xb * cos + pltpu.roll(xb, LANES - 16, 1) * sa + pltpu.roll(xb, 16, 1) * sb)
        return jnp.concatenate(outs, axis=1)

    def section_t(j, n):
        return lax.dot_general(wt_ref[j * WIDTH:j * WIDTH + n, :], h, (((1,), (1,)), ((), ())),
                               preferred_element_type=F32)

    for j in range(3):
        m_ref[:, j * WIDTH:(j + 1) * WIDTH] = section(j).astype(BF16)
    kmt_ref[...] = (section_t(0, WIDTH) * (HEAD_DIM ** -0.5)).astype(BF16)
    pq = section(3)
    if rope:
        pq = rotate(pq)
    q_ref[...] = (pq * (LOG2E * QK_DIM ** -0.5)).astype(BF16)
    pk = section(4)
    if emit_cache:
        newk_ref[...] = pk
    if rope:
        pk = rotate(pk)
    k_ref[...] = pk.astype(BF16)
    if emit_cache:
        newv_ref[...] = section(5)
    vt_ref[...] = section_t(1, WIDTH).astype(BF16)
    gt_ref[...] = section_t(2, LANES)[:N_GATES, :] + bg_ref[...]


def _proj(x2d, nrm, sc, sh, row0, tiles_per_row, w_r, w_t, b_gate, rope_tabs, emit_cache, tm):
    t, d = x2d.shape
    nt = t // tm
    rope = rope_tabs is not None
    row_map = lambda i: (row0 + i // tiles_per_row, 0, 0)
    tok = lambda i: (i, 0)
    feat = lambda i: (0, i)
    in_specs = [pl.BlockSpec((tm, d), tok),
                pl.BlockSpec((1, d), lambda i: (0, 0)),
                pl.BlockSpec((1, 1, d), row_map),
                pl.BlockSpec((1, 1, d), row_map),
                pl.BlockSpec(w_r.shape, lambda i: (0, 0)),
                pl.BlockSpec(w_t.shape, lambda i: (0, 0)),
                pl.BlockSpec((N_GATES, 1), lambda i: (0, 0))]
    args = [x2d, nrm, sc, sh, w_r, w_t, b_gate]
    if rope:
        tps = rope_tabs[0].shape[0] // tm
        for tab in rope_tabs:
            in_specs.append(pl.BlockSpec((tm, LANES), lambda i: (i % tps, 0)))
            args.append(tab)
    out_shape = [jax.ShapeDtypeStruct((t, 3 * WIDTH), BF16),
                 jax.ShapeDtypeStruct((WIDTH, t), BF16),
                 jax.ShapeDtypeStruct((t, WIDTH), BF16),
                 jax.ShapeDtypeStruct((t, WIDTH), BF16),
                 jax.ShapeDtypeStruct((WIDTH, t), BF16),
                 jax.ShapeDtypeStruct((N_GATES, t), F32)]
    out_specs = [pl.BlockSpec((tm, 3 * WIDTH), tok),
                 pl.BlockSpec((WIDTH, tm), feat),
                 pl.BlockSpec((tm, WIDTH), tok),
                 pl.BlockSpec((tm, WIDTH), tok),
                 pl.BlockSpec((WIDTH, tm), feat),
                 pl.BlockSpec((N_GATES, tm), feat)]
    if emit_cache:
        out_shape += [jax.ShapeDtypeStruct((t, WIDTH), F32)] * 2
        out_specs += [pl.BlockSpec((tm, WIDTH), tok)] * 2
    return pl.pallas_call(
        functools.partial(_proj_kernel, rope=rope, emit_cache=emit_cache),
        grid=(nt,),
        in_specs=in_specs,
        out_specs=out_specs,
        out_shape=out_shape,
        compiler_params=_params(("parallel",)),
        name="proj_rope" if rope else "proj_ctx",
    )(*args)


def _rope_tables(seq):
    t = jnp.arange(seq)
    row = (t // GRID_W).astype(F32)[:, None]
    col = (t % GRID_W).astype(F32)[:, None]
    lane = jnp.arange(LANES)
    freqs = ROPE_THETA ** (-(lane % 16).astype(F32) / 16.0)
    pos = jnp.where(((lane % 64) < 32)[None, :], row, col)
    ang = pos * freqs[None, :]
    cos, sin = jnp.cos(ang), jnp.sin(ang)
    first = ((lane % 32) < 16)[None, :]
    return cos, jnp.where(first, -sin, 0.0), jnp.where(first, 0.0, sin)


def _lane_scan(x, pos, op, forward):
    n = x.shape[-1]
    shift = 1
    while shift < n:
        if forward:
            y = pltpu.roll(x, shift, 1)
            ok = pos >= shift
        else:
            y = pltpu.roll(x, n - shift, 1)
            ok = pos < n - shift
        x = jnp.where(ok, op(x, y), x)
        shift *= 2
    return x


def _mlstm_kernel(q_ref, kt_ref, v_ref, g_ref, c0_ref, n0_ref, m0_ref,
                  hf_ref, hb_ref, cf_ref, nf_ref, mf_ref,
                  r_ref, mp_ref, mn_ref, dc_ref, tab_ref, *st_refs, nc, hp):
    L = CHUNK
    rows = nc * hp
    pos = lax.broadcasted_iota(jnp.int32, (rows, L), 1)
    tab_ref[...] = jnp.zeros_like(tab_ref)
    for d in range(2):
        i_pre = g_ref[0, 0, (2 * d) * rows:(2 * d + 1) * rows, :]
        lf = _log_sigmoid(g_ref[0, 0, (2 * d + 1) * rows:(2 * d + 2) * rows, :])
        pre = _lane_scan(lf, pos, jnp.add, True)
        suf = _lane_scan(lf, pos, jnp.add, False)
        tot = pre + suf - lf
        b = pre if d == 0 else suf
        r = i_pre - b
        r_max = _lane_scan(r, pos, jnp.maximum, d == 0)
        gk = tot - b + i_pre
        g_max = jnp.maximum(_lane_scan(gk, pos, jnp.maximum, True), _lane_scan(gk, pos, jnp.maximum, False))
        m = m0_ref[0, d, :, 0, :]
        for c in (range(nc) if d == 0 else reversed(range(nc))):
            sl = slice(c * hp, (c + 1) * hp)
            mp_ref[d, sl, :] = m
            m = jnp.maximum(tot[sl] + m, g_max[sl])
            mn_ref[d, sl, :] = m
        mf_ref[0, d, :, 0, :] = m
        m_prev = mp_ref[d]
        m_new = mn_ref[d]
        m_row = jnp.maximum(m_prev, r_max)
        r_ref[d] = r
        dc_ref[d] = jnp.exp(tot + m_prev - m_new)
        w_k = jnp.exp(gk - m_new)
        floor = jnp.exp(-(b + m_row))
        for c in range(nc):
            src = slice(c * hp, (c + 1) * hp)
            slot = c if d == 0 else nc - 1 - c
            for qi, val in enumerate((m_row, floor, w_k)):
                base = (2 * qi + d) * hp
                tab_ref[slot, base:base + hp, :] = val[src]

    for d in range(2):
        for hh in range(hp):
            st_refs[d * hp + hh][:, :HEAD_DIM] = jnp.transpose(c0_ref[0, d, hh])
            st_refs[d * hp + hh][:, HEAD_DIM:] = jnp.transpose(
                jnp.broadcast_to(n0_ref[0, d, hh], (HEAD_DIM, HEAD_DIM)))

    rr = lax.broadcasted_iota(jnp.int32, (L, L), 0)
    cc = lax.broadcasted_iota(jnp.int32, (L, L), 1)
    masks = (cc <= rr, cc >= rr)
    ones = jnp.ones((L, HEAD_DIM), BF16)

    def body(c, carry):
        tab = jnp.transpose(tab_ref[c])
        units = [(d, hh) for d in range(2) for hh in range(hp)]

        def operands(d, hh):
            ci = c if d == 0 else nc - 1 - c
            off = pl.multiple_of(ci * L, L)
            lanes = slice(hh * HEAD_DIM, (hh + 1) * HEAD_DIM)
            return ci * hp + hh, off, lanes

        def column(d, hh, which):
            j = (2 * which + d) * hp + hh
            return jnp.broadcast_to(tab[:, j:j + 1], (L, HEAD_DIM))

        qk, inter, upd = {}, {}, {}
        for d, hh in units:
            row, off, lanes = operands(d, hh)
            q = q_ref[0, pl.ds(off, L), lanes]
            kt = kt_ref[lanes, pl.ds(off, L)]
            qk[d, hh] = jnp.dot(q, kt, preferred_element_type=F32)
            inter[d, hh] = jnp.dot(q, st_refs[d * hp + hh][...].astype(BF16), preferred_element_type=F32)
        for d, hh in units:
            row, off, lanes = operands(d, hh)
            v = v_ref[0, pl.ds(off, L), lanes]
            kt = kt_ref[lanes, pl.ds(off, L)]
            wk_col = column(d, hh, 2)
            vw = (v.astype(F32) * wk_col).astype(BF16)
            upd[d, hh] = jnp.dot(kt, jnp.concatenate([vw, wk_col.astype(BF16)], axis=1),
                                 preferred_element_type=F32)
        for d, hh in units:
            row, off, lanes = operands(d, hh)
            m_col = column(d, hh, 0)
            r_row = r_ref[d, pl.ds(row, 1), :]
            m_prev = mp_ref[d, pl.ds(row, 1), :]
            v = v_ref[0, pl.ds(off, L), lanes]
            w_intra = jnp.exp(jnp.where(masks[d], r_row - m_col, -jnp.inf))
            s = (qk[d, hh] * w_intra).astype(BF16)
            intra = jnp.dot(s, jnp.concatenate([v, ones], axis=1), preferred_element_type=F32)
            w_state = jnp.exp(m_prev - m_col)
            both = intra + jnp.concatenate([w_state, w_state], axis=1) * inter[d, hh]
            den = jnp.maximum(jnp.abs(both[:, HEAD_DIM:]), column(d, hh, 1))
            out_ref = hf_ref if d == 0 else hb_ref
            out_ref[0, pl.ds(off, L), lanes] = (both[:, :HEAD_DIM] / den).astype(BF16)
        for d, hh in units:
            row, off, lanes = operands(d, hh)
            decay = dc_ref[d, pl.ds(row, 1), :]
            st = st_refs[d * hp + hh]
            st[...] = jnp.concatenate([decay, decay], axis=1) * st[...] + upd[d, hh]
        return carry

    lax.fori_loop(0, nc, body, 0)

    for d in range(2):
        for hh in range(hp):
            state = st_refs[d * hp + hh][...]
            cf_ref[0, d, hh] = jnp.transpose(state[:, :HEAD_DIM])
            nf_ref[0, d, hh] = jnp.transpose(state[:, HEAD_DIM:])[0:1, :]


def _mlstm(qvo, kt, gt, c0, n0, m0, hp):
    b, s, _ = qvo.shape
    nc = s // CHUNK
    hg = N_HEADS // hp
    rows = nc * hp
    g6 = gt.reshape(4, hg, hp, b, nc, CHUNK).transpose(3, 1, 0, 4, 2, 5).reshape(b, hg, 4 * rows, CHUNK)
    n0r = n0.reshape(b, 2, N_HEADS, 1, HEAD_DIM)
    m0r = jnp.broadcast_to(m0[..., None, None], (b, 2, N_HEADS, 1, HEAD_DIM))
    blk = hp * HEAD_DIM
    nblk = WIDTH // blk
    state_spec = lambda w: pl.BlockSpec((1, 2, hp, w, HEAD_DIM), lambda i, j: (i, 0, j, 0, 0))
    in_specs = [pl.BlockSpec((1, s, blk), lambda i, j: (i, 0, j)),
                pl.BlockSpec((blk, s), lambda i, j: (j, i)),
                pl.BlockSpec((1, s, blk), lambda i, j: (i, 0, nblk + j)),
                pl.BlockSpec((1, 1, 4 * rows, CHUNK), lambda i, j: (i, j, 0, 0)),
                state_spec(HEAD_DIM), state_spec(1), state_spec(1)]
    out_specs = [pl.BlockSpec((1, s, blk), lambda i, j: (i, 0, j)),
                 pl.BlockSpec((1, s, blk), lambda i, j: (i, 0, j)),
                 state_spec(HEAD_DIM), state_spec(1), state_spec(1)]
    out_shape = [jax.ShapeDtypeStruct((b, s, WIDTH), BF16),
                 jax.ShapeDtypeStruct((b, s, WIDTH), BF16),
                 jax.ShapeDtypeStruct((b, 2, N_HEADS, HEAD_DIM, HEAD_DIM), F32),
                 jax.ShapeDtypeStruct((b, 2, N_HEADS, 1, HEAD_DIM), F32),
                 jax.ShapeDtypeStruct((b, 2, N_HEADS, 1, HEAD_DIM), F32)]
    scratch = [pltpu.VMEM((2, rows, CHUNK), F32),
               pltpu.VMEM((2, rows, CHUNK), F32),
               pltpu.VMEM((2, rows, CHUNK), F32),
               pltpu.VMEM((2, rows, CHUNK), F32),
               pltpu.VMEM((nc, LANES, CHUNK), F32)]
    scratch += [pltpu.VMEM((HEAD_DIM, 2 * HEAD_DIM), F32) for _ in range(2 * hp)]
    hf, hb, cf, nf, mf = pl.pallas_call(
        functools.partial(_mlstm_kernel, nc=nc, hp=hp),
        grid=(b, hg),
        in_specs=in_specs,
        out_specs=out_specs,
        out_shape=out_shape,
        scratch_shapes=scratch,
        compiler_params=_params(("parallel", "parallel")),
        name="mlstm",
    )(qvo, kt, qvo, g6, c0, n0r, m0r)
    return hf, hb, cf, nf[:, :, :, 0, :], mf[:, :, :, 0, 0]


def _attn_kernel(*refs, sub, n_new, n_cache, lam_init):
    if n_cache:
        lp_ref, q_ref, k_ref, kc_ref, vt_ref, vct_ref, o_ref, s_even_ref, s_odd_ref = refs
    else:
        lp_ref, q_ref, k_ref, vt_ref, o_ref, s_even_ref, s_odd_ref = refs
    lp = lp_ref[...]
    lam = (jnp.exp(jnp.sum(lp[0:1] * lp[1:2], axis=-1, keepdims=True))
           - jnp.exp(jnp.sum(lp[2:3] * lp[3:4], axis=-1, keepdims=True)) + lam_init)
    lane = lax.broadcasted_iota(jnp.int32, (sub, HEAD_DIM), 1)
    n_sub = n_new // sub
    nt = (((1,), (1,)), ((), ()))

    def scores(idx, s_ref):
        r0 = pl.multiple_of(idx * sub, sub)
        q = q_ref[0, pl.ds(r0, sub), :]
        zero = jnp.zeros_like(q)
        qq = jnp.concatenate([jnp.where(lane < QK_DIM, q, zero), jnp.where(lane >= QK_DIM, q, zero)], axis=0)
        s = lax.dot_general(k_ref[0], qq, nt, preferred_element_type=F32)
        s_ref[0:n_new, :] = s
        m = jnp.max(s, axis=0, keepdims=True)
        if n_cache:
            sc = lax.dot_general(kc_ref[0], qq, nt, preferred_element_type=F32)
            s_ref[n_new:n_new + n_cache, :] = sc
            m = jnp.maximum(m, jnp.max(sc, axis=0, keepdims=True))
        return m

    def finish(idx, s_ref, m):
        r0 = pl.multiple_of(idx * sub, sub)
        e = jnp.exp2(s_ref[...] - m)
        l = jnp.sum(e, axis=0, keepdims=True)
        eb = e.astype(BF16)
        o = jnp.dot(vt_ref[...], eb[0:n_new], preferred_element_type=F32)
        if n_cache:
            o = o + jnp.dot(vct_ref[0], eb[n_new:n_new + n_cache], preferred_element_type=F32)
        o = o / l
        out_t = o[:, :sub] - lam * o[:, sub:]
        o_ref[0, pl.ds(r0, sub), :] = jnp.transpose(out_t).astype(BF16)

    def pair(i, m_even):
        a = 2 * i
        m_odd = scores(a + 1, s_odd_ref)
        finish(a, s_even_ref, m_even)
        m_even_next = scores(jnp.minimum(a + 2, n_sub - 1), s_even_ref)
        finish(a + 1, s_odd_ref, m_odd)
        return m_even_next

    lax.fori_loop(0, n_sub // 2, pair, scores(0, s_even_ref))


def _attn(lp, q, k, vt, cache, lam_init):
    b, s, _ = q.shape
    assert s % (2 * ATTN_SUB) == 0
    n_cache = 0 if cache is None else cache[0].shape[1]
    in_specs = [pl.BlockSpec(lp.shape, lambda i, h: (0, 0)),
                pl.BlockSpec((1, s, HEAD_DIM), lambda i, h: (i, 0, h)),
                pl.BlockSpec((1, s, HEAD_DIM), lambda i, h: (i, 0, h))]
    args = [lp, q, k]
    if n_cache:
        in_specs.append(pl.BlockSpec((1, n_cache, HEAD_DIM), lambda i, h: (i, 0, h)))
        args.append(cache[0])
    in_specs.append(pl.BlockSpec((HEAD_DIM, s), lambda i, h: (h, i)))
    args.append(vt)
    if n_cache:
        in_specs.append(pl.BlockSpec((1, HEAD_DIM, n_cache), lambda i, h: (i, h, 0)))
        args.append(cache[1])
    keys = s + n_cache
    return pl.pallas_call(
        functools.partial(_attn_kernel, sub=ATTN_SUB, n_new=s, n_cache=n_cache, lam_init=lam_init),
        grid=(b, N_HEADS),
        in_specs=in_specs,
        out_specs=pl.BlockSpec((1, s, HEAD_DIM), lambda i, h: (i, 0, h)),
        out_shape=jax.ShapeDtypeStruct((b, s, WIDTH), BF16),
        scratch_shapes=[pltpu.VMEM((keys, 2 * ATTN_SUB), F32), pltpu.VMEM((keys, 2 * ATTN_SUB), F32)],
        compiler_params=_params(("parallel", "parallel")),
        name="diff_attn",
    )(*args)


def _head_norm(x):
    outs = []
    for j in range(N_HEADS):
        xb = x[:, j * HEAD_DIM:(j + 1) * HEAD_DIM]
        outs.append(xb * lax.rsqrt(jnp.mean(xb * xb, axis=-1, keepdims=True) + EPS))
    return jnp.concatenate(outs, axis=1)


def _mix_kernel(hf_ref, hb_ref, om_ref, hd_ref, x_ref, g1_ref, sc2_ref, sh2_ref, gm_ref, gd_ref,
                wout_ref, nrm_ref, wr_ref, br_ref, x1_ref, h2_ref, chl_ref, grpt_ref, *, diff_scale):
    hm = _head_norm(hf_ref[...].astype(F32) + hb_ref[...].astype(F32))
    hm = hm * gm_ref[...] * _sigmoid(om_ref[...].astype(F32))
    hd = _head_norm(hd_ref[...].astype(F32)) * gd_ref[...] * diff_scale
    mix = (jnp.dot(hm.astype(BF16), wout_ref[:WIDTH, :], preferred_element_type=F32)
           + jnp.dot(hd.astype(BF16), wout_ref[WIDTH:, :], preferred_element_type=F32))
    x1 = x_ref[...] + g1_ref[0] * mix
    x1_ref[...] = x1
    ms = jnp.mean(x1 * x1, axis=-1, keepdims=True)
    h2 = (x1 * lax.rsqrt(ms + EPS) * nrm_ref[...]) * (1.0 + sc2_ref[0]) + sh2_ref[0]
    h2_hi = h2.astype(BF16)
    h2_ref[...] = h2_hi

    h2_lo = (h2 - h2_hi.astype(F32)).astype(BF16)
    parts = (jnp.dot(h2_hi, wr_ref[...], preferred_element_type=F32)
             + jnp.dot(h2_lo, wr_ref[...], preferred_element_type=F32))
    logits = parts[:, :LANES] + parts[:, LANES:] + br_ref[...]
    lane = lax.broadcasted_iota(jnp.int32, logits.shape, 1).astype(F32)
    big = float(LANES)
    is_grp = (lane >= N_EXPERTS) & (lane < N_EXPERTS + N_GROUPS)
    lg = jnp.where(is_grp, logits, -jnp.inf)
    mx = jnp.max(lg, axis=-1, keepdims=True)
    p_grp = 1.0 / jnp.sum(jnp.exp(lg - mx), axis=-1, keepdims=True)
    grp = jnp.min(jnp.where(lg == mx, lane, big), axis=-1, keepdims=True) - N_EXPERTS
    base = grp * EXPERTS_PER_GROUP
    sel = (lane >= base) & (lane < base + EXPERTS_PER_GROUP)
    le = jnp.where(sel, logits, -jnp.inf)
    ee = jnp.exp(le - jnp.max(le, axis=-1, keepdims=True))
    pe = ee / jnp.sum(ee, axis=-1, keepdims=True)
    top1 = jnp.max(pe, axis=-1, keepdims=True)
    idx1 = jnp.min(jnp.where(sel & (pe == top1), lane, big), axis=-1, keepdims=True)
    rest = sel & (lane != idx1)
    top2 = jnp.max(jnp.where(rest, pe, -1.0), axis=-1, keepdims=True)
    idx2 = jnp.min(jnp.where(rest & (pe == top2), lane, big), axis=-1, keepdims=True)
    denom = top1 + top2
    w1 = top1 / denom * p_grp
    w2 = top2 / denom * p_grp
    chl = jnp.zeros_like(logits)
    for j in range(EXPERTS_PER_GROUP):
        cj = jnp.where(idx1 == base + j, w1, 0.0) + jnp.where(idx2 == base + j, w2, 0.0)
        hi = cj.astype(BF16).astype(F32)
        chl = chl + jnp.where(lane == j, hi, 0.0) + jnp.where(lane == EXPERTS_PER_GROUP + j, cj - hi, 0.0)
    chl_ref[...] = chl.astype(BF16)
    grpt_ref[...] = jnp.transpose(jnp.broadcast_to(grp, logits.shape))[:SUBLANES, :]


def _mix(hf, hb, qvo, hd, x2d, g1, sc2, sh2, row0, tiles_per_row, gm, gd, wout, nrm, wr, br, diff_scale, tm):
    t, d = x2d.shape
    row_map = lambda i: (row0 + i // tiles_per_row, 0, 0)
    tok = lambda i: (i, 0)
    full = lambda i: (0, 0)
    return pl.pallas_call(
        functools.partial(_mix_kernel, diff_scale=diff_scale),
        grid=(t // tm,),
        in_specs=[pl.BlockSpec((tm, WIDTH), tok),
                  pl.BlockSpec((tm, WIDTH), tok),
                  pl.BlockSpec((tm, WIDTH), lambda i: (i, 2)),
                  pl.BlockSpec((tm, WIDTH), tok),
                  pl.BlockSpec((tm, d), tok),
                  pl.BlockSpec((1, 1, d), row_map),
                  pl.BlockSpec((1, 1, d), row_map),
                  pl.BlockSpec((1, 1, d), row_map),
                  pl.BlockSpec((1, WIDTH), full),
                  pl.BlockSpec((1, WIDTH), full),
                  pl.BlockSpec(wout.shape, full),
                  pl.BlockSpec((1, d), full),
                  pl.BlockSpec(wr.shape, full),
                  pl.BlockSpec(br.shape, full)],
        out_specs=[pl.BlockSpec((tm, d), tok),
                   pl.BlockSpec((tm, d), tok),
                   pl.BlockSpec((tm, LANES), tok),
                   pl.BlockSpec((SUBLANES, tm), lambda i: (0, i))],
        out_shape=[jax.ShapeDtypeStruct((t, d), F32),
                   jax.ShapeDtypeStruct((t, d), BF16),
                   jax.ShapeDtypeStruct((t, LANES), BF16),
                   jax.ShapeDtypeStruct((SUBLANES, t), F32)],
        compiler_params=_params(("parallel",)),
        name="mix_router",
    )(hf, hb, qvo, hd, x2d, g1, sc2, sh2, gm, gd, wout, nrm, wr, br)


def _moe_kernel(h2_ref, chl_ref, grpt_ref, wgu_ref, wdn_ref, x1_ref, g2_ref, nrm_ref, y_ref,
                acc_ref, rank_ref, *, hidden, blk):
    g = pl.program_id(1)
    tm = h2_ref.shape[0]

    @pl.when(g == 0)
    def _():
        acc_ref[...] = jnp.zeros_like(acc_ref)
        gid = lax.broadcasted_iota(jnp.int32, (SUBLANES, tm), 0).astype(F32)
        member = grpt_ref[...] == gid
        src = lax.broadcasted_iota(jnp.int32, (tm, tm), 0)
        dst = lax.broadcasted_iota(jnp.int32, (tm, tm), 1)
        before = jnp.where(src < dst, 1.0, 0.0).astype(BF16)
        rank = jnp.dot(jnp.where(member, 1.0, 0.0).astype(BF16), before, preferred_element_type=F32)
        rank_ref[...] = jnp.where(member, rank, -1.0)

    rank_g = rank_ref[pl.ds(g, 1), :]
    n_g = jnp.max(rank_g) + 1.0

    sizes, start = [], 0
    while start < tm:
        size = min(blk // 4 if len(sizes) == 1 else blk, tm - start)
        sizes.append((start, size))
        start += size

    for start, size in sizes:
        @pl.when(n_g > start)
        def _():
            slot = lax.broadcasted_iota(jnp.int32, (size, tm), 0).astype(F32) + float(start)
            pick = jnp.where(rank_g == slot, 1.0, 0.0).astype(BF16)
            xg = jnp.dot(pick, h2_ref[...], preferred_element_type=F32).astype(BF16)
            cg = jnp.dot(pick, chl_ref[...], preferred_element_type=F32)
            gus = [jnp.dot(xg, wgu_ref[e], preferred_element_type=F32) for e in range(EXPERTS_PER_GROUP)]
            out = jnp.zeros((size, acc_ref.shape[1]), F32)
            for e, gu in enumerate(gus):
                a, u = gu[:, :hidden], gu[:, hidden:]
                ce = cg[:, e:e + 1] + cg[:, EXPERTS_PER_GROUP + e:EXPERTS_PER_GROUP + e + 1]
                act = (a * _sigmoid(a) * u * ce).astype(BF16)
                out = out + jnp.dot(act, wdn_ref[e], preferred_element_type=F32)
            acc_ref[...] += lax.dot_general(pick, out.astype(BF16), (((0,), (0,)), ((), ())),
                                            preferred_element_type=F32)

    @pl.when(g == pl.num_programs(1) - 1)
    def _():
        x2 = x1_ref[...] + g2_ref[0] * acc_ref[...]
        ms = jnp.mean(x2 * x2, axis=-1, keepdims=True)
        y_ref[...] = x2 * lax.rsqrt(ms + EPS) * nrm_ref[...]


def _moe(h2, chl, grpt, wgu, wdn, x1, g2, row0, tiles_per_row, nrm, tm, blk):
    t, d = x1.shape
    _, _, two_h = wgu.shape
    hidden = two_h // 2
    row_map = lambda i, g: (row0 + i // tiles_per_row, 0, 0)
    tok = lambda i, g: (i, 0)
    return pl.pallas_call(
        functools.partial(_moe_kernel, hidden=hidden, blk=blk),
        grid=(t // tm, N_GROUPS),
        in_specs=[pl.BlockSpec((tm, d), tok),
                  pl.BlockSpec((tm, LANES), tok),
                  pl.BlockSpec((SUBLANES, tm), lambda i, g: (0, i)),
                  pl.BlockSpec((EXPERTS_PER_GROUP, d, two_h), lambda i, g: (g, 0, 0)),
                  pl.BlockSpec((EXPERTS_PER_GROUP, hidden, d), lambda i, g: (g, 0, 0)),
                  pl.BlockSpec((tm, d), tok),
                  pl.BlockSpec((1, 1, d), row_map),
                  pl.BlockSpec((1, d), lambda i, g: (0, 0))],
        out_specs=pl.BlockSpec((tm, d), tok),
        out_shape=jax.ShapeDtypeStruct((t, d), F32),
        scratch_shapes=[pltpu.VMEM((tm, d), F32), pltpu.VMEM((SUBLANES, tm), F32)],
        compiler_params=_params(("parallel", "arbitrary")),
        name="moe_experts",
    )(h2, chl, grpt, wgu, wdn, x1, g2, nrm)


def _layer(x, mods, row0, per_request, weights, states, cache, rope_tabs, emit_cache, tm, hp):
    b, s, d = x.shape
    sh1, sc1, g1, sh2, sc2, g2 = mods
    (nrm_mix, nrm_ffn, w_in_r, w_in_t, b_gate, gm, gd, lp, wout, wr, br, wgu, wdn, nrm_final, lam_init) = weights
    x2d = x.reshape(b * s, d)
    tiles_per_row = (s // tm) if per_request else (b * s // tm)
    outs = _proj(x2d, nrm_mix, sc1, sh1, row0, tiles_per_row, w_in_r, w_in_t, b_gate, rope_tabs, emit_cache, tm)
    qvo, kmt, qd, kd, vdt, gt = outs[:6]
    c0, n0, m0 = states
    hf, hb, cf, nf, mf = _mlstm(qvo.reshape(b, s, 3 * WIDTH), kmt, gt, c0, n0, m0, hp)
    hd = _attn(lp, qd.reshape(b, s, WIDTH), kd.reshape(b, s, WIDTH), vdt, cache, lam_init)
    x1, h2, chl, grpt = _mix(hf.reshape(b * s, WIDTH), hb.reshape(b * s, WIDTH), qvo, hd.reshape(b * s, WIDTH),
                             x2d, g1, sc2, sh2, row0, tiles_per_row, gm, gd, wout, nrm_ffn, wr, br,
                             1.0 - lam_init, tm)
    tm_moe = 2 * tm
    y = _moe(h2, chl, grpt, wgu, wdn, x1, g2, row0, tiles_per_row // 2 if per_request else b * s // tm_moe,
             nrm_final, tm_moe, MOE_BLOCK)
    new_kv = outs[6:] if emit_cache else None
    return y.reshape(b, s, d), new_kv, (cf, nf, mf)


def kernel(x_prompt, x_sample, cache_k, cache_v, state_C, state_n, state_m, c, c_ctx, w_ada, b_ada, norm_mix, norm_ffn, w_in, b_gate, mlstm_norm, diff_norm, diff_lambda, w_out, w_route_group, b_route_group, w_route_expert, b_route_expert, w_gate_up, w_down, final_norm):
    depth = w_in.shape[0]
    assert depth == 1, "the final norm is fused into the last layer's expert kernel"
    bp, sp, d = x_prompt.shape
    bs, ss, _ = x_sample.shape
    rows = 16
    assert bs < rows
    cvec = jnp.zeros((rows, d), F32).at[:bs].set(c).at[bs].set(c_ctx)
    rope_tabs = _rope_tables(ss)
    xp, xs = x_prompt, x_sample
    new_k, new_v, new_c, new_n, new_m = [], [], [], [], []
    for l in range(depth):
        lam_init = 0.8 - 0.6 * math.exp(-0.3 * l)
        mod = _ada(cvec, w_ada[l], b_ada[l])
        mods = [mod[:, i * d:(i + 1) * d].reshape(rows, 1, d) for i in range(6)]
        wl = w_in[l]
        off_dq = 4 * WIDTH + N_GATES
        w_in_r = jnp.concatenate(
            [wl[:, :WIDTH], wl[:, 2 * WIDTH:4 * WIDTH], wl[:, off_dq:]], axis=1).astype(BF16)
        w_in_t = jnp.concatenate(
            [wl[:, WIDTH:2 * WIDTH], wl[:, off_dq + 2 * WIDTH:], wl[:, 4 * WIDTH:off_dq],
             jnp.zeros((d, LANES - N_GATES), F32)], axis=1).T.astype(BF16)
        bg = b_gate[l].reshape(N_GATES, 1)
        wr = jnp.concatenate([w_route_expert[l], w_route_group[l],
                              jnp.zeros((d, LANES - N_EXPERTS - N_GROUPS), F32)], axis=1)
        wr_hi = wr.astype(BF16)
        wr = jnp.concatenate([wr_hi, (wr - wr_hi.astype(F32)).astype(BF16)], axis=1)
        br = jnp.concatenate([b_route_expert[l], b_route_group[l],
                              jnp.zeros((LANES - N_EXPERTS - N_GROUPS,), F32)]).reshape(1, LANES)
        weights = (norm_mix[l].reshape(1, d), norm_ffn[l].reshape(1, d), w_in_r, w_in_t, bg,
                   mlstm_norm[l].reshape(1, WIDTH), diff_norm[l].reshape(1, WIDTH), diff_lambda[l],
                   w_out[l].astype(BF16), wr, br, w_gate_up[l].astype(BF16), w_down[l].astype(BF16),
                   final_norm.reshape(1, d), lam_init)

        zero_states = (jnp.zeros((bp, 2, N_HEADS, HEAD_DIM, HEAD_DIM), F32),
                       jnp.zeros((bp, 2, N_HEADS, HEAD_DIM), F32),
                       jnp.zeros((bp, 2, N_HEADS), F32))
        xp, (nk, nv), (cf, nf, mf) = _layer(xp, mods, bs, False, weights, zero_states, None, None, True,
                                            tm=512, hp=4)
        new_k.append(nk.reshape(bp, sp, N_HEADS, 2, QK_DIM))
        new_v.append(nv.reshape(bp, sp, N_HEADS, HEAD_DIM))
        new_c.append(cf)
        new_n.append(nf)
        new_m.append(mf)

        states = (state_C[:, l], state_n[:, l], state_m[:, l])
        cache = (cache_k[:, l].reshape(bs, -1, WIDTH).astype(BF16),
                 cache_v[:, l].reshape(bs, -1, WIDTH).transpose(0, 2, 1).astype(BF16))
        xs, _, _ = _layer(xs, mods, 0, True, weights, states, cache, rope_tabs, False,
                          tm=512, hp=4)

    return (xp, xs, jnp.stack(new_k, axis=1), jnp.stack(new_v, axis=1),
            jnp.stack(new_c, axis=1), jnp.stack(new_n, axis=1), jnp.stack(new_m, axis=1))
```

```python
import functools
import math

import jax
import jax.numpy as jnp
from jax import lax
from jax.experimental import pallas as pl
from jax.experimental.pallas import tpu as pltpu

F32 = jnp.float32
BF16 = jnp.bfloat16
HIGHEST = lax.Precision.HIGHEST

EPS = 1e-6
GRID_W = 64
ROPE_THETA = 10000.0
N_HEADS = 4
HEAD_DIM = 128
QK_DIM = 64
WIDTH = N_HEADS * HEAD_DIM
CHUNK = 128
N_GATES = 4 * N_HEADS
N_GROUPS = 4
EXPERTS_PER_GROUP = 4
N_EXPERTS = 16
LANES = 128
SUBLANES = 8
MOE_BLOCK = 256
ATTN_SUB = 128
ATTN_KEY_CHUNK = 1024
LOG2E = math.log2(math.e)
VMEM_LIMIT = 56 * 1024 * 1024


def _params(sem):
    return pltpu.CompilerParams(dimension_semantics=sem, vmem_limit_bytes=VMEM_LIMIT)


def _log_sigmoid(x):
    return jnp.minimum(x, 0.0) - jnp.log1p(jnp.exp(-jnp.abs(x)))


def _sigmoid(x):
    return 1.0 / (1.0 + jnp.exp(-x))


def _ada_kernel(c_ref, w_ref, b_ref, o_ref):
    c = c_ref[...]
    s = c * _sigmoid(c)
    o_ref[...] = jnp.dot(s, w_ref[...], preferred_element_type=F32, precision=HIGHEST) + b_ref[...]


def _ada(cvec, w_ada, b_ada):
    rows, d = cvec.shape
    n = w_ada.shape[1]
    tn = 1024
    return pl.pallas_call(
        _ada_kernel,
        grid=(n // tn,),
        in_specs=[pl.BlockSpec((rows, d), lambda j: (0, 0)),
                  pl.BlockSpec((d, tn), lambda j: (0, j)),
                  pl.BlockSpec((1, tn), lambda j: (0, j))],
        out_specs=pl.BlockSpec((rows, tn), lambda j: (0, j)),
        out_shape=jax.ShapeDtypeStruct((rows, n), F32),
        compiler_params=_params(("arbitrary",)),
        name="ada",
    )(cvec, w_ada, b_ada.reshape(1, n))


def _proj_kernel(*refs, rope, emit_cache):
    x_ref, nrm_ref, sc_ref, sh_ref, w_ref, wt_ref, bg_ref = refs[:7]
    refs = refs[7:]
    if rope:
        cos_ref, sa_ref, sb_ref = refs[:3]
        refs = refs[3:]
    m_ref, kmt_ref, q_ref, k_ref, vt_ref, gt_ref = refs[:6]
    refs = refs[6:]
    if emit_cache:
        newk_ref, newv_ref = refs

    x = x_ref[...]
    ms = jnp.mean(x * x, axis=-1, keepdims=True)
    y = x * lax.rsqrt(ms + EPS) * nrm_ref[...]
    h = (y * (1.0 + sc_ref[0]) + sh_ref[0]).astype(BF16)

    def section(j):
        return jnp.dot(h, w_ref[:, j * WIDTH:(j + 1) * WIDTH], preferred_element_type=F32)

    def rotate(p):
        cos, sa, sb = cos_ref[...], sa_ref[...], sb_ref[...]
        outs = []
        for j in range(N_HEADS):
            xb = p[:, j * LANES:(j + 1) * LANES]
            outs.append(xb * cos + pltpu.roll(xb, LANES - 16, 1) * sa + pltpu.roll(xb, 16, 1) * sb)
        return jnp.concatenate(outs, axis=1)

    def section_t(j, n):
        return lax.dot_general(wt_ref[j * WIDTH:j * WIDTH + n, :], h, (((1,), (1,)), ((), ())),
                               preferred_element_type=F32)

    for j in range(3):
        m_ref[:, j * WIDTH:(j + 1) * WIDTH] = section(j).astype(BF16)
    kmt_ref[...] = (section_t(0, WIDTH) * (HEAD_DIM ** -0.5)).astype(BF16)
    pq = section(3)
    if rope:
        pq = rotate(pq)
    q_ref[...] = (pq * (LOG2E * QK_DIM ** -0.5)).astype(BF16)
    pk = section(4)
    if emit_cache:
        newk_ref[...] = pk
    if rope:
        pk = rotate(pk)
    k_ref[...] = pk.astype(BF16)
    if emit_cache:
        newv_ref[...] = section(5)
    vt_ref[...] = section_t(1, WIDTH).astype(BF16)
    gt_ref[...] = section_t(2, LANES)[:N_GATES, :] + bg_ref[...]


def _proj(x2d, nrm, sc, sh, row0, tiles_per_row, w_r, w_t, b_gate, rope_tabs, emit_cache, tm):
    t, d = x2d.shape
    nt = t // tm
    rope = rope_tabs is not None
    row_map = lambda i: (row0 + i // tiles_per_row, 0, 0)
    tok = lambda i: (i, 0)
    feat = lambda i: (0, i)
    in_specs = [pl.BlockSpec((tm, d), tok),
                pl.BlockSpec((1, d), lambda i: (0, 0)),
                pl.BlockSpec((1, 1, d), row_map),
                pl.BlockSpec((1, 1, d), row_map),
                pl.BlockSpec(w_r.shape, lambda i: (0, 0)),
                pl.BlockSpec(w_t.shape, lambda i: (0, 0)),
                pl.BlockSpec((N_GATES, 1), lambda i: (0, 0))]
    args = [x2d, nrm, sc, sh, w_r, w_t, b_gate]
    if rope:
        tps = rope_tabs[0].shape[0] // tm
        for tab in rope_tabs:
            in_specs.append(pl.BlockSpec((tm, LANES), lambda i: (i % tps, 0)))
            args.append(tab)
    out_shape = [jax.ShapeDtypeStruct((t, 3 * WIDTH), BF16),
                 jax.ShapeDtypeStruct((WIDTH, t), BF16),
                 jax.ShapeDtypeStruct((t, WIDTH), BF16),
                 jax.ShapeDtypeStruct((t, WIDTH), BF16),
                 jax.ShapeDtypeStruct((WIDTH, t), BF16),
                 jax.ShapeDtypeStruct((N_GATES, t), F32)]
    out_specs = [pl.BlockSpec((tm, 3 * WIDTH), tok),
                 pl.BlockSpec((WIDTH, tm), feat),
                 pl.BlockSpec((tm, WIDTH), tok),
                 pl.BlockSpec((tm, WIDTH), tok),
                 pl.BlockSpec((WIDTH, tm), feat),
                 pl.BlockSpec((N_GATES, tm), feat)]
    if emit_cache:
        out_shape += [jax.ShapeDtypeStruct((t, WIDTH), F32)] * 2
        out_specs += [pl.BlockSpec((tm, WIDTH), tok)] * 2
    return pl.pallas_call(
        functools.partial(_proj_kernel, rope=rope, emit_cache=emit_cache),
        grid=(nt,),
        in_specs=in_specs,
        out_specs=out_specs,
        out_shape=out_shape,
        compiler_params=_params(("parallel",)),
        name="proj_rope" if rope else "proj_ctx",
    )(*args)


def _rope_tables(seq):
    t = jnp.arange(seq)
    row = (t // GRID_W).astype(F32)[:, None]
    col = (t % GRID_W).astype(F32)[:, None]
    lane = jnp.arange(LANES)
    freqs = ROPE_THETA ** (-(lane % 16).astype(F32) / 16.0)
    pos = jnp.where(((lane % 64) < 32)[None, :], row, col)
    ang = pos * freqs[None, :]
    cos, sin = jnp.cos(ang), jnp.sin(ang)
    first = ((lane % 32) < 16)[None, :]
    return cos, jnp.where(first, -sin, 0.0), jnp.where(first, 0.0, sin)


def _lane_scan(x, pos, op, forward):
    n = x.shape[-1]
    shift = 1
    while shift < n:
        if forward:
            y = pltpu.roll(x, shift, 1)
            ok = pos >= shift
        else:
            y = pltpu.roll(x, n - shift, 1)
            ok = pos < n - shift
        x = jnp.where(ok, op(x, y), x)
        shift *= 2
    return x


def _mlstm_kernel(q_ref, kt_ref, v_ref, g_ref, c0_ref, n0_ref, m0_ref,
                  hf_ref, hb_ref, cf_ref, nf_ref, mf_ref,
                  r_ref, mp_ref, mn_ref, dc_ref, tab_ref, *st_refs, nc, hp):
    L = CHUNK
    rows = nc * hp
    pos = lax.broadcasted_iota(jnp.int32, (rows, L), 1)
    tab_ref[...] = jnp.zeros_like(tab_ref)
    for d in range(2):
        i_pre = g_ref[0, 0, (2 * d) * rows:(2 * d + 1) * rows, :]
        lf = _log_sigmoid(g_ref[0, 0, (2 * d + 1) * rows:(2 * d + 2) * rows, :])
        pre = _lane_scan(lf, pos, jnp.add, True)
        suf = _lane_scan(lf, pos, jnp.add, False)
        tot = pre + suf - lf
        b = pre if d == 0 else suf
        r = i_pre - b
        r_max = _lane_scan(r, pos, jnp.maximum, d == 0)
        gk = tot - b + i_pre
        g_max = jnp.maximum(_lane_scan(gk, pos, jnp.maximum, True), _lane_scan(gk, pos, jnp.maximum, False))
        m = m0_ref[0, d, :, 0, :]
        for c in (range(nc) if d == 0 else reversed(range(nc))):
            sl = slice(c * hp, (c + 1) * hp)
            mp_ref[d, sl, :] = m
            m = jnp.maximum(tot[sl] + m, g_max[sl])
            mn_ref[d, sl, :] = m
        mf_ref[0, d, :, 0, :] = m
        m_prev = mp_ref[d]
        m_new = mn_ref[d]
        m_row = jnp.maximum(m_prev, r_max)
        r_ref[d] = r
        dc_ref[d] = jnp.exp(tot + m_prev - m_new)
        w_k = jnp.exp(gk - m_new)
        floor = jnp.exp(-(b + m_row))
        for c in range(nc):
            src = slice(c * hp, (c + 1) * hp)
            slot = c if d == 0 else nc - 1 - c
            for qi, val in enumerate((m_row, floor, w_k)):
                base = (2 * qi + d) * hp
                tab_ref[slot, base:base + hp, :] = val[src]

    for d in range(2):
        for hh in range(hp):
            st_refs[d * hp + hh][:, :HEAD_DIM] = jnp.transpose(c0_ref[0, d, hh])
            st_refs[d * hp + hh][:, HEAD_DIM:] = jnp.transpose(
                jnp.broadcast_to(n0_ref[0, d, hh], (HEAD_DIM, HEAD_DIM)))

    rr = lax.broadcasted_iota(jnp.int32, (L, L), 0)
    cc = lax.broadcasted_iota(jnp.int32, (L, L), 1)
    masks = (cc <= rr, cc >= rr)
    ones = jnp.ones((L, HEAD_DIM), BF16)

    def body(c, carry):
        tab = jnp.transpose(tab_ref[c])
        units = [(d, hh) for d in range(2) for hh in range(hp)]

        def operands(d, hh):
            ci = c if d == 0 else nc - 1 - c
            off = pl.multiple_of(ci * L, L)
            lanes = slice(hh * HEAD_DIM, (hh + 1) * HEAD_DIM)
            return ci * hp + hh, off, lanes

        def column(d, hh, which):
            j = (2 * which + d) * hp + hh
            return jnp.broadcast_to(tab[:, j:j + 1], (L, HEAD_DIM))

        qk, inter, upd = {}, {}, {}
        for d, hh in units:
            row, off, lanes = operands(d, hh)
            q = q_ref[0, pl.ds(off, L), lanes]
            kt = kt_ref[lanes, pl.ds(off, L)]
            qk[d, hh] = jnp.dot(q, kt, preferred_element_type=F32)
            inter[d, hh] = jnp.dot(q, st_refs[d * hp + hh][...].astype(BF16), preferred_element_type=F32)
        for d, hh in units:
            row, off, lanes = operands(d, hh)
            v = v_ref[0, pl.ds(off, L), lanes]
            kt = kt_ref[lanes, pl.ds(off, L)]
            wk_col = column(d, hh, 2)
            vw = (v.astype(F32) * wk_col).astype(BF16)
            upd[d, hh] = jnp.dot(kt, jnp.concatenate([vw, wk_col.astype(BF16)], axis=1),
                                 preferred_element_type=F32)
        for d, hh in units:
            row, off, lanes = operands(d, hh)
            m_col = column(d, hh, 0)
            r_row = r_ref[d, pl.ds(row, 1), :]
            m_prev = mp_ref[d, pl.ds(row, 1), :]
            v = v_ref[0, pl.ds(off, L), lanes]
            w_intra = jnp.exp(jnp.where(masks[d], r_row - m_col, -jnp.inf))
            s = (qk[d, hh] * w_intra).astype(BF16)
            intra = jnp.dot(s, jnp.concatenate([v, ones], axis=1), preferred_element_type=F32)
            w_state = jnp.exp(m_prev - m_col)
            both = intra + jnp.concatenate([w_state, w_state], axis=1) * inter[d, hh]
            den = jnp.maximum(jnp.abs(both[:, HEAD_DIM:]), column(d, hh, 1))
            out_ref = hf_ref if d == 0 else hb_ref
            out_ref[0, pl.ds(off, L), lanes] = (both[:, :HEAD_DIM] / den).astype(BF16)
        for d, hh in units:
            row, off, lanes = operands(d, hh)
            decay = dc_ref[d, pl.ds(row, 1), :]
            st = st_refs[d * hp + hh]
            st[...] = jnp.concatenate([decay, decay], axis=1) * st[...] + upd[d, hh]
        return carry

    lax.fori_loop(0, nc, body, 0)

    for d in range(2):
        for hh in range(hp):
            state = st_refs[d * hp + hh][...]
            cf_ref[0, d, hh] = jnp.transpose(state[:, :HEAD_DIM])
            nf_ref[0, d, hh] = jnp.transpose(state[:, HEAD_DIM:])[0:1, :]


def _mlstm(qvo, kt, gt, c0, n0, m0, hp):
    b, s, _ = qvo.shape
    nc = s // CHUNK
    hg = N_HEADS // hp
    rows = nc * hp
    g6 = gt.reshape(4, hg, hp, b, nc, CHUNK).transpose(3, 1, 0, 4, 2, 5).reshape(b, hg, 4 * rows, CHUNK)
    n0r = n0.reshape(b, 2, N_HEADS, 1, HEAD_DIM)
    m0r = jnp.broadcast_to(m0[..., None, None], (b, 2, N_HEADS, 1, HEAD_DIM))
    blk = hp * HEAD_DIM
    nblk = WIDTH // blk
    state_spec = lambda w: pl.BlockSpec((1, 2, hp, w, HEAD_DIM), lambda i, j: (i, 0, j, 0, 0))
    in_specs = [pl.BlockSpec((1, s, blk), lambda i, j: (i, 0, j)),
                pl.BlockSpec((blk, s), lambda i, j: (j, i)),
                pl.BlockSpec((1, s, blk), lambda i, j: (i, 0, nblk + j)),
                pl.BlockSpec((1, 1, 4 * rows, CHUNK), lambda i, j: (i, j, 0, 0)),
                state_spec(HEAD_DIM), state_spec(1), state_spec(1)]
    out_specs = [pl.BlockSpec((1, s, blk), lambda i, j: (i, 0, j)),
                 pl.BlockSpec((1, s, blk), lambda i, j: (i, 0, j)),
                 state_spec(HEAD_DIM), state_spec(1), state_spec(1)]
    out_shape = [jax.ShapeDtypeStruct((b, s, WIDTH), BF16),
                 jax.ShapeDtypeStruct((b, s, WIDTH), BF16),
                 jax.ShapeDtypeStruct((b, 2, N_HEADS, HEAD_DIM, HEAD_DIM), F32),
                 jax.ShapeDtypeStruct((b, 2, N_HEADS, 1, HEAD_DIM), F32),
                 jax.ShapeDtypeStruct((b, 2, N_HEADS, 1, HEAD_DIM), F32)]
    scratch = [pltpu.VMEM((2, rows, CHUNK), F32),
               pltpu.VMEM((2, rows, CHUNK), F32),
               pltpu.VMEM((2, rows, CHUNK), F32),
               pltpu.VMEM((2, rows, CHUNK), F32),
               pltpu.VMEM((nc, LANES, CHUNK), F32)]
    scratch += [pltpu.VMEM((HEAD_DIM, 2 * HEAD_DIM), F32) for _ in range(2 * hp)]
    hf, hb, cf, nf, mf = pl.pallas_call(
        functools.partial(_mlstm_kernel, nc=nc, hp=hp),
        grid=(b, hg),
        in_specs=in_specs,
        out_specs=out_specs,
        out_shape=out_shape,
        scratch_shapes=scratch,
        compiler_params=_params(("parallel", "parallel")),
        name="mlstm",
    )(qvo, kt, qvo, g6, c0, n0r, m0r)
    return hf, hb, cf, nf[:, :, :, 0, :], mf[:, :, :, 0, 0]


def _attn_kernel(*refs, sub, key_chunk, n_new, n_cache, lam_init):
    if n_cache:
        lp_ref, q_ref, k_ref, kc_ref, vt_ref, vct_ref, o_ref, s_even_ref, s_odd_ref = refs
    else:
        lp_ref, q_ref, k_ref, vt_ref, o_ref, s_even_ref, s_odd_ref = refs
    lp = lp_ref[...]
    lam = (jnp.exp(jnp.sum(lp[0:1] * lp[1:2], axis=-1, keepdims=True))
           - jnp.exp(jnp.sum(lp[2:3] * lp[3:4], axis=-1, keepdims=True)) + lam_init)
    lane = lax.broadcasted_iota(jnp.int32, (sub, HEAD_DIM), 1)
    n_sub = n_new // sub
    nt = (((1,), (1,)), ((), ()))

    bounds = list(range(0, n_new, key_chunk)) + [n_new]
    chunks = [(False, c0, c0, c1 - c0) for c0, c1 in zip(bounds[:-1], bounds[1:])]
    if n_cache:
        chunks.append((True, 0, n_new, n_cache))

    def keys_of(chunk):
        cached, c0, _, n = chunk
        return kc_ref[0, c0:c0 + n, :] if cached else k_ref[0, c0:c0 + n, :]

    def values_of(chunk):
        cached, c0, _, n = chunk
        return vct_ref[0, :, c0:c0 + n] if cached else vt_ref[:, c0:c0 + n]

    def queries(idx):
        r0 = pl.multiple_of(idx * sub, sub)
        q = q_ref[0, pl.ds(r0, sub), :]
        zero = jnp.zeros_like(q)
        return jnp.concatenate([jnp.where(lane < QK_DIM, q, zero), jnp.where(lane >= QK_DIM, q, zero)], axis=0)

    def scores(qq, chunk, s_ref):
        s = lax.dot_general(keys_of(chunk), qq, nt, preferred_element_type=F32)
        s_ref[chunk[2]:chunk[2] + chunk[3], :] = s
        return jnp.max(s, axis=0, keepdims=True)

    def step(idx_next, s_next_ref, idx, s_ref, m):
        qq = queries(idx_next)
        m_next, l, o = None, None, None
        for chunk in chunks:
            mc = scores(qq, chunk, s_next_ref)
            m_next = mc if m_next is None else jnp.maximum(m_next, mc)
            e = jnp.exp2(s_ref[chunk[2]:chunk[2] + chunk[3], :] - m)
            lc = jnp.sum(e, axis=0, keepdims=True)
            oc = jnp.dot(values_of(chunk), e.astype(BF16), preferred_element_type=F32)
            l = lc if l is None else l + lc
            o = oc if o is None else o + oc
        o = o / l
        out_t = o[:, :sub] - lam * o[:, sub:]
        r0 = pl.multiple_of(idx * sub, sub)
        o_ref[0, pl.ds(r0, sub), :] = jnp.transpose(out_t).astype(BF16)
        return m_next

    def pair(i, m_even):
        a = 2 * i
        m_odd = step(a + 1, s_odd_ref, a, s_even_ref, m_even)
        return step(jnp.minimum(a + 2, n_sub - 1), s_even_ref, a + 1, s_odd_ref, m_odd)

    qq0 = queries(0)
    m0 = None
    for chunk in chunks:
        mc = scores(qq0, chunk, s_even_ref)
        m0 = mc if m0 is None else jnp.maximum(m0, mc)
    lax.fori_loop(0, n_sub // 2, pair, m0)


def _attn(lp, q, k, vt, cache, lam_init):
    b, s, _ = q.shape
    assert s % (2 * ATTN_SUB) == 0
    n_cache = 0 if cache is None else cache[0].shape[1]
    in_specs = [pl.BlockSpec(lp.shape, lambda i, h: (0, 0)),
                pl.BlockSpec((1, s, HEAD_DIM), lambda i, h: (i, 0, h)),
                pl.BlockSpec((1, s, HEAD_DIM), lambda i, h: (i, 0, h))]
    args = [lp, q, k]
    if n_cache:
        in_specs.append(pl.BlockSpec((1, n_cache, HEAD_DIM), lambda i, h: (i, 0, h)))
        args.append(cache[0])
    in_specs.append(pl.BlockSpec((HEAD_DIM, s), lambda i, h: (h, i)))
    args.append(vt)
    if n_cache:
        in_specs.append(pl.BlockSpec((1, HEAD_DIM, n_cache), lambda i, h: (i, h, 0)))
        args.append(cache[1])
    keys = s + n_cache
    return pl.pallas_call(
        functools.partial(_attn_kernel, sub=ATTN_SUB, key_chunk=min(ATTN_KEY_CHUNK, s), n_new=s, n_cache=n_cache,
                          lam_init=lam_init),
        grid=(b, N_HEADS),
        in_specs=in_specs,
        out_specs=pl.BlockSpec((1, s, HEAD_DIM), lambda i, h: (i, 0, h)),
        out_shape=jax.ShapeDtypeStruct((b, s, WIDTH), BF16),
        scratch_shapes=[pltpu.VMEM((keys, 2 * ATTN_SUB), F32), pltpu.VMEM((keys, 2 * ATTN_SUB), F32)],
        compiler_params=_params(("parallel", "parallel")),
        name="diff_attn",
    )(*args)


def _head_norm(x):
    outs = []
    for j in range(N_HEADS):
        xb = x[:, j * HEAD_DIM:(j + 1) * HEAD_DIM]
        outs.append(xb * lax.rsqrt(jnp.mean(xb * xb, axis=-1, keepdims=True) + EPS))
    return jnp.concatenate(outs, axis=1)


def _mix_kernel(hf_ref, hb_ref, om_ref, hd_ref, x_ref, g1_ref, sc2_ref, sh2_ref, gm_ref, gd_ref,
                wout_ref, nrm_ref, wr_ref, br_ref, x1_ref, h2_ref, chl_ref, grpt_ref, *, diff_scale):
    hm = _head_norm(hf_ref[...].astype(F32) + hb_ref[...].astype(F32))
    hm = hm * gm_ref[...] * _sigmoid(om_ref[...].astype(F32))
    hd = _head_norm(hd_ref[...].astype(F32)) * gd_ref[...] * diff_scale
    mix = (jnp.dot(hm.astype(BF16), wout_ref[:WIDTH, :], preferred_element_type=F32)
           + jnp.dot(hd.astype(BF16), wout_ref[WIDTH:, :], preferred_element_type=F32))
    x1 = x_ref[...] + g1_ref[0] * mix
    x1_ref[...] = x1
    ms = jnp.mean(x1 * x1, axis=-1, keepdims=True)
    h2 = (x1 * lax.rsqrt(ms + EPS) * nrm_ref[...]) * (1.0 + sc2_ref[0]) + sh2_ref[0]
    h2_hi = h2.astype(BF16)
    h2_ref[...] = h2_hi

    h2_lo = (h2 - h2_hi.astype(F32)).astype(BF16)
    parts = (jnp.dot(h2_hi, wr_ref[...], preferred_element_type=F32)
             + jnp.dot(h2_lo, wr_ref[...], preferred_element_type=F32))
    logits = parts[:, :LANES] + parts[:, LANES:] + br_ref[...]
    lane = lax.broadcasted_iota(jnp.int32, logits.shape, 1).astype(F32)
    big = float(LANES)
    is_grp = (lane >= N_EXPERTS) & (lane < N_EXPERTS + N_GROUPS)
    lg = jnp.where(is_grp, logits, -jnp.inf)
    mx = jnp.max(lg, axis=-1, keepdims=True)
    p_grp = 1.0 / jnp.sum(jnp.exp(lg - mx), axis=-1, keepdims=True)
    grp = jnp.min(jnp.where(lg == mx, lane, big), axis=-1, keepdims=True) - N_EXPERTS
    base = grp * EXPERTS_PER_GROUP
    sel = (lane >= base) & (lane < base + EXPERTS_PER_GROUP)
    le = jnp.where(sel, logits, -jnp.inf)
    ee = jnp.exp(le - jnp.max(le, axis=-1, keepdims=True))
    pe = ee / jnp.sum(ee, axis=-1, keepdims=True)
    top1 = jnp.max(pe, axis=-1, keepdims=True)
    idx1 = jnp.min(jnp.where(sel & (pe == top1), lane, big), axis=-1, keepdims=True)
    rest = sel & (lane != idx1)
    top2 = jnp.max(jnp.where(rest, pe, -1.0), axis=-1, keepdims=True)
    idx2 = jnp.min(jnp.where(rest & (pe == top2), lane, big), axis=-1, keepdims=True)
    denom = top1 + top2
    w1 = top1 / denom * p_grp
    w2 = top2 / denom * p_grp
    chl = jnp.zeros_like(logits)
    for j in range(EXPERTS_PER_GROUP):
        cj = jnp.where(idx1 == base + j, w1, 0.0) + jnp.where(idx2 == base + j, w2, 0.0)
        hi = cj.astype(BF16).astype(F32)
        chl = chl + jnp.where(lane == j, hi, 0.0) + jnp.where(lane == EXPERTS_PER_GROUP + j, cj - hi, 0.0)
    chl_ref[...] = chl.astype(BF16)
    grpt_ref[...] = jnp.transpose(jnp.broadcast_to(grp, logits.shape))[:SUBLANES, :]


def _mix(hf, hb, qvo, hd, x2d, g1, sc2, sh2, row0, tiles_per_row, gm, gd, wout, nrm, wr, br, diff_scale, tm):
    t, d = x2d.shape
    row_map = lambda i: (row0 + i // tiles_per_row, 0, 0)
    tok = lambda i: (i, 0)
    full = lambda i: (0, 0)
    return pl.pallas_call(
        functools.partial(_mix_kernel, diff_scale=diff_scale),
        grid=(t // tm,),
        in_specs=[pl.BlockSpec((tm, WIDTH), tok),
                  pl.BlockSpec((tm, WIDTH), tok),
                  pl.BlockSpec((tm, WIDTH), lambda i: (i, 2)),
                  pl.BlockSpec((tm, WIDTH), tok),
                  pl.BlockSpec((tm, d), tok),
                  pl.BlockSpec((1, 1, d), row_map),
                  pl.BlockSpec((1, 1, d), row_map),
                  pl.BlockSpec((1, 1, d), row_map),
                  pl.BlockSpec((1, WIDTH), full),
                  pl.BlockSpec((1, WIDTH), full),
                  pl.BlockSpec(wout.shape, full),
                  pl.BlockSpec((1, d), full),
                  pl.BlockSpec(wr.shape, full),
                  pl.BlockSpec(br.shape, full)],
        out_specs=[pl.BlockSpec((tm, d), tok),
                   pl.BlockSpec((tm, d), tok),
                   pl.BlockSpec((tm, LANES), tok),
                   pl.BlockSpec((SUBLANES, tm), lambda i: (0, i))],
        out_shape=[jax.ShapeDtypeStruct((t, d), F32),
                   jax.ShapeDtypeStruct((t, d), BF16),
                   jax.ShapeDtypeStruct((t, LANES), BF16),
                   jax.ShapeDtypeStruct((SUBLANES, t), F32)],
        compiler_params=_params(("parallel",)),
        name="mix_router",
    )(hf, hb, qvo, hd, x2d, g1, sc2, sh2, gm, gd, wout, nrm, wr, br)


def _moe_kernel(h2_ref, chl_ref, grpt_ref, wgu_ref, wdn_ref, x1_ref, g2_ref, nrm_ref, y_ref,
                acc_ref, rank_ref, *, hidden, blk):
    g = pl.program_id(1)
    tm = h2_ref.shape[0]

    @pl.when(g == 0)
    def _():
        acc_ref[...] = jnp.zeros_like(acc_ref)
        gid = lax.broadcasted_iota(jnp.int32, (SUBLANES, tm), 0).astype(F32)
        member = grpt_ref[...] == gid
        src = lax.broadcasted_iota(jnp.int32, (tm, tm), 0)
        dst = lax.broadcasted_iota(jnp.int32, (tm, tm), 1)
        before = jnp.where(src < dst, 1.0, 0.0).astype(BF16)
        rank = jnp.dot(jnp.where(member, 1.0, 0.0).astype(BF16), before, preferred_element_type=F32)
        rank_ref[...] = jnp.where(member, rank, -1.0)

    rank_g = rank_ref[pl.ds(g, 1), :]
    n_g = jnp.max(rank_g) + 1.0

    sizes, start = [], 0
    while start < tm:
        size = min(blk // 4 if len(sizes) == 1 else blk, tm - start)
        sizes.append((start, size))
        start += size

    for start, size in sizes:
        @pl.when(n_g > start)
        def _():
            slot = lax.broadcasted_iota(jnp.int32, (size, tm), 0).astype(F32) + float(start)
            pick = jnp.where(rank_g == slot, 1.0, 0.0).astype(BF16)
            xg = jnp.dot(pick, h2_ref[...], preferred_element_type=F32).astype(BF16)
            cg = jnp.dot(pick, chl_ref[...], preferred_element_type=F32)
            gus = [jnp.dot(xg, wgu_ref[e], preferred_element_type=F32) for e in range(EXPERTS_PER_GROUP)]
            out = jnp.zeros((size, acc_ref.shape[1]), F32)
            for e, gu in enumerate(gus):
                a, u = gu[:, :hidden], gu[:, hidden:]
                ce = cg[:, e:e + 1] + cg[:, EXPERTS_PER_GROUP + e:EXPERTS_PER_GROUP + e + 1]
                act = (a * _sigmoid(a) * u * ce).astype(BF16)
                out = out + jnp.dot(act, wdn_ref[e], preferred_element_type=F32)
            acc_ref[...] += lax.dot_general(pick, out.astype(BF16), (((0,), (0,)), ((), ())),
                                            preferred_element_type=F32)

    @pl.when(g == pl.num_programs(1) - 1)
    def _():
        x2 = x1_ref[...] + g2_ref[0] * acc_ref[...]
        ms = jnp.mean(x2 * x2, axis=-1, keepdims=True)
        y_ref[...] = x2 * lax.rsqrt(ms + EPS) * nrm_ref[...]


def _moe(h2, chl, grpt, wgu, wdn, x1, g2, row0, tiles_per_row, nrm, tm, blk):
    t, d = x1.shape
    _, _, two_h = wgu.shape
    hidden = two_h // 2
    row_map = lambda i, g: (row0 + i // tiles_per_row, 0, 0)
    tok = lambda i, g: (i, 0)
    return pl.pallas_call(
        functools.partial(_moe_kernel, hidden=hidden, blk=blk),
        grid=(t // tm, N_GROUPS),
        in_specs=[pl.BlockSpec((tm, d), tok),
                  pl.BlockSpec((tm, LANES), tok),
                  pl.BlockSpec((SUBLANES, tm), lambda i, g: (0, i)),
                  pl.BlockSpec((EXPERTS_PER_GROUP, d, two_h), lambda i, g: (g, 0, 0)),
                  pl.BlockSpec((EXPERTS_PER_GROUP, hidden, d), lambda i, g: (g, 0, 0)),
                  pl.BlockSpec((tm, d), tok),
                  pl.BlockSpec((1, 1, d), row_map),
                  pl.BlockSpec((1, d), lambda i, g: (0, 0))],
        out_specs=pl.BlockSpec((tm, d), tok),
        out_shape=jax.ShapeDtypeStruct((t, d), F32),
        scratch_shapes=[pltpu.VMEM((tm, d), F32), pltpu.VMEM((SUBLANES, tm), F32)],
        compiler_params=_params(("parallel", "arbitrary")),
        name="moe_experts",
    )(h2, chl, grpt, wgu, wdn, x1, g2, nrm)


def _layer(x, mods, row0, per_request, weights, states, cache, rope_tabs, emit_cache, tm, hp):
    b, s, d = x.shape
    sh1, sc1, g1, sh2, sc2, g2 = mods
    (nrm_mix, nrm_ffn, w_in_r, w_in_t, b_gate, gm, gd, lp, wout, wr, br, wgu, wdn, nrm_final, lam_init) = weights
    x2d = x.reshape(b * s, d)
    tiles_per_row = (s // tm) if per_request else (b * s // tm)
    outs = _proj(x2d, nrm_mix, sc1, sh1, row0, tiles_per_row, w_in_r, w_in_t, b_gate, rope_tabs, emit_cache, tm)
    qvo, kmt, qd, kd, vdt, gt = outs[:6]
    c0, n0, m0 = states
    hf, hb, cf, nf, mf = _mlstm(qvo.reshape(b, s, 3 * WIDTH), kmt, gt, c0, n0, m0, hp)
    hd = _attn(lp, qd.reshape(b, s, WIDTH), kd.reshape(b, s, WIDTH), vdt, cache, lam_init)
    x1, h2, chl, grpt = _mix(hf.reshape(b * s, WIDTH), hb.reshape(b * s, WIDTH), qvo, hd.reshape(b * s, WIDTH),
                             x2d, g1, sc2, sh2, row0, tiles_per_row, gm, gd, wout, nrm_ffn, wr, br,
                             1.0 - lam_init, tm)
    tm_moe = 2 * tm
    y = _moe(h2, chl, grpt, wgu, wdn, x1, g2, row0, tiles_per_row // 2 if per_request else b * s // tm_moe,
             nrm_final, tm_moe, MOE_BLOCK)
    new_kv = outs[6:] if emit_cache else None
    return y.reshape(b, s, d), new_kv, (cf, nf, mf)


def kernel(x_prompt, x_sample, cache_k, cache_v, state_C, state_n, state_m, c, c_ctx, w_ada, b_ada, norm_mix, norm_ffn, w_in, b_gate, mlstm_norm, diff_norm, diff_lambda, w_out, w_route_group, b_route_group, w_route_expert, b_route_expert, w_gate_up, w_down, final_norm):
    depth = w_in.shape[0]
    assert depth == 1, "the final norm is fused into the last layer's expert kernel"
    bp, sp, d = x_prompt.shape
    bs, ss, _ = x_sample.shape
    rows = 16
    assert bs < rows
    cvec = jnp.zeros((rows, d), F32).at[:bs].set(c).at[bs].set(c_ctx)
    rope_tabs = _rope_tables(ss)
    xp, xs = x_prompt, x_sample
    new_k, new_v, new_c, new_n, new_m = [], [], [], [], []
    for l in range(depth):
        lam_init = 0.8 - 0.6 * math.exp(-0.3 * l)
        mod = _ada(cvec, w_ada[l], b_ada[l])
        mods = [mod[:, i * d:(i + 1) * d].reshape(rows, 1, d) for i in range(6)]
        wl = w_in[l]
        off_dq = 4 * WIDTH + N_GATES
        w_in_r = jnp.concatenate(
            [wl[:, :WIDTH], wl[:, 2 * WIDTH:4 * WIDTH], wl[:, off_dq:]], axis=1).astype(BF16)
        w_in_t = jnp.concatenate(
            [wl[:, WIDTH:2 * WIDTH], wl[:, off_dq + 2 * WIDTH:], wl[:, 4 * WIDTH:off_dq],
             jnp.zeros((d, LANES - N_GATES), F32)], axis=1).T.astype(BF16)
        bg = b_gate[l].reshape(N_GATES, 1)
        wr = jnp.concatenate([w_route_expert[l], w_route_group[l],
                              jnp.zeros((d, LANES - N_EXPERTS - N_GROUPS), F32)], axis=1)
        wr_hi = wr.astype(BF16)
        wr = jnp.concatenate([wr_hi, (wr - wr_hi.astype(F32)).astype(BF16)], axis=1)
        br = jnp.concatenate([b_route_expert[l], b_route_group[l],
                              jnp.zeros((LANES - N_EXPERTS - N_GROUPS,), F32)]).reshape(1, LANES)
        weights = (norm_mix[l].reshape(1, d), norm_ffn[l].reshape(1, d), w_in_r, w_in_t, bg,
                   mlstm_norm[l].reshape(1, WIDTH), diff_norm[l].reshape(1, WIDTH), diff_lambda[l],
                   w_out[l].astype(BF16), wr, br, w_gate_up[l].astype(BF16), w_down[l].astype(BF16),
                   final_norm.reshape(1, d), lam_init)

        zero_states = (jnp.zeros((bp, 2, N_HEADS, HEAD_DIM, HEAD_DIM), F32),
                       jnp.zeros((bp, 2, N_HEADS, HEAD_DIM), F32),
                       jnp.zeros((bp, 2, N_HEADS), F32))
        xp, (nk, nv), (cf, nf, mf) = _layer(xp, mods, bs, False, weights, zero_states, None, None, True,
                                            tm=512, hp=4)
        new_k.append(nk.reshape(bp, sp, N_HEADS, 2, QK_DIM))
        new_v.append(nv.reshape(bp, sp, N_HEADS, HEAD_DIM))
        new_c.append(cf)
        new_n.append(nf)
        new_m.append(mf)

        states = (state_C[:, l], state_n[:, l], state_m[:, l])
        cache = (cache_k[:, l].reshape(bs, -1, WIDTH).astype(BF16),
                 cache_v[:, l].reshape(bs, -1, WIDTH).transpose(0, 2, 1).astype(BF16))
        xs, _, _ = _layer(xs, mods, 0, True, weights, states, cache, rope_tabs, False,
                          tm=512, hp=4)

    return (xp, xs, jnp.stack(new_k, axis=1), jnp.stack(new_v, axis=1),
            jnp.stack(new_c, axis=1), jnp.stack(new_n, axis=1), jnp.stack(new_m, axis=1))
```

```python
import functools
import math

import jax
import jax.numpy as jnp
from jax import lax
from jax.experimental import pallas as pl
from jax.experimental.pallas import tpu as pltpu

F32 = jnp.float32
BF16 = jnp.bfloat16
HIGHEST = lax.Precision.HIGHEST

EPS = 1e-6
GRID_W = 64
ROPE_THETA = 10000.0
N_HEADS = 4
HEAD_DIM = 128
QK_DIM = 64
WIDTH = N_HEADS * HEAD_DIM
CHUNK = 128
N_GATES = 4 * N_HEADS
N_GROUPS = 4
EXPERTS_PER_GROUP = 4
N_EXPERTS = 16
LANES = 128
SUBLANES = 8
MOE_BLOCK = 256
ATTN_SUB = 128
ATTN_KEY_CHUNK = 1024
ATTN_SHORT = 512
LOG2E = math.log2(math.e)
VMEM_LIMIT = 56 * 1024 * 1024


def _params(sem):
    return pltpu.CompilerParams(dimension_semantics=sem, vmem_limit_bytes=VMEM_LIMIT)


def _log_sigmoid(x):
    return jnp.minimum(x, 0.0) - jnp.log1p(jnp.exp(-jnp.abs(x)))


def _sigmoid(x):
    return 1.0 / (1.0 + jnp.exp(-x))


def _ada_kernel(c_ref, w_ref, b_ref, o_ref):
    c = c_ref[...]
    s = c * _sigmoid(c)
    o_ref[...] = jnp.dot(s, w_ref[...], preferred_element_type=F32, precision=HIGHEST) + b_ref[...]


def _ada(cvec, w_ada, b_ada):
    rows, d = cvec.shape
    n = w_ada.shape[1]
    tn = 1024
    return pl.pallas_call(
        _ada_kernel,
        grid=(n // tn,),
        in_specs=[pl.BlockSpec((rows, d), lambda j: (0, 0)),
                  pl.BlockSpec((d, tn), lambda j: (0, j)),
                  pl.BlockSpec((1, tn), lambda j: (0, j))],
        out_specs=pl.BlockSpec((rows, tn), lambda j: (0, j)),
        out_shape=jax.ShapeDtypeStruct((rows, n), F32),
        compiler_params=_params(("arbitrary",)),
        name="ada",
    )(cvec, w_ada, b_ada.reshape(1, n))


def _proj_kernel(*refs, rope, emit_cache):
    x_ref, nrm_ref, sc_ref, sh_ref, w_ref, wt_ref, bg_ref = refs[:7]
    refs = refs[7:]
    if rope:
        cos_ref, sa_ref, sb_ref = refs[:3]
        refs = refs[3:]
    m_ref, kmt_ref, q_ref, k_ref, vt_ref, gt_ref = refs[:6]
    refs = refs[6:]
    if emit_cache:
        newk_ref, newv_ref = refs

    x = x_ref[...]
    ms = jnp.mean(x * x, axis=-1, keepdims=True)
    y = x * lax.rsqrt(ms + EPS) * nrm_ref[...]
    h = (y * (1.0 + sc_ref[0]) + sh_ref[0]).astype(BF16)

    def section(j):
        return jnp.dot(h, w_ref[:, j * WIDTH:(j + 1) * WIDTH], preferred_element_type=F32)

    def rotate(p):
        cos, sa, sb = cos_ref[...], sa_ref[...], sb_ref[...]
        outs = []
        for j in range(N_HEADS):
            xb = p[:, j * LANES:(j + 1) * LANES]
            outs.append(xb * cos + pltpu.roll(xb, LANES - 16, 1) * sa + pltpu.roll(xb, 16, 1) * sb)
        return jnp.concatenate(outs, axis=1)

    def section_t(j, n):
        return lax.dot_general(wt_ref[j * WIDTH:j * WIDTH + n, :], h, (((1,), (1,)), ((), ())),
                               preferred_element_type=F32)

    for j in range(3):
        m_ref[:, j * WIDTH:(j + 1) * WIDTH] = section(j).astype(BF16)
    kmt_ref[...] = (section_t(0, WIDTH) * (HEAD_DIM ** -0.5)).astype(BF16)
    pq = section(3)
    if rope:
        pq = rotate(pq)
    q_ref[...] = (pq * (LOG2E * QK_DIM ** -0.5)).astype(BF16)
    pk = section(4)
    if emit_cache:
        newk_ref[...] = pk
    if rope:
        pk = rotate(pk)
    k_ref[...] = pk.astype(BF16)
    if emit_cache:
        newv_ref[...] = section(5)
    vt_ref[...] = section_t(1, WIDTH).astype(BF16)
    gt_ref[...] = section_t(2, LANES)[:N_GATES, :] + bg_ref[...]


def _proj(x2d, nrm, sc, sh, row0, tiles_per_row, w_r, w_t, b_gate, rope_tabs, emit_cache, tm):
    t, d = x2d.shape
    nt = t // tm
    rope = rope_tabs is not None
    row_map = lambda i: (row0 + i // tiles_per_row, 0, 0)
    tok = lambda i: (i, 0)
    feat = lambda i: (0, i)
    in_specs = [pl.BlockSpec((tm, d), tok),
                pl.BlockSpec((1, d), lambda i: (0, 0)),
                pl.BlockSpec((1, 1, d), row_map),
                pl.BlockSpec((1, 1, d), row_map),
                pl.BlockSpec(w_r.shape, lambda i: (0, 0)),
                pl.BlockSpec(w_t.shape, lambda i: (0, 0)),
                pl.BlockSpec((N_GATES, 1), lambda i: (0, 0))]
    args = [x2d, nrm, sc, sh, w_r, w_t, b_gate]
    if rope:
        tps = rope_tabs[0].shape[0] // tm
        for tab in rope_tabs:
            in_specs.append(pl.BlockSpec((tm, LANES), lambda i: (i % tps, 0)))
            args.append(tab)
    out_shape = [jax.ShapeDtypeStruct((t, 3 * WIDTH), BF16),
                 jax.ShapeDtypeStruct((WIDTH, t), BF16),
                 jax.ShapeDtypeStruct((t, WIDTH), BF16),
                 jax.ShapeDtypeStruct((t, WIDTH), BF16),
                 jax.ShapeDtypeStruct((WIDTH, t), BF16),
                 jax.ShapeDtypeStruct((N_GATES, t), F32)]
    out_specs = [pl.BlockSpec((tm, 3 * WIDTH), tok),
                 pl.BlockSpec((WIDTH, tm), feat),
                 pl.BlockSpec((tm, WIDTH), tok),
                 pl.BlockSpec((tm, WIDTH), tok),
                 pl.BlockSpec((WIDTH, tm), feat),
                 pl.BlockSpec((N_GATES, tm), feat)]
    if emit_cache:
        out_shape += [jax.ShapeDtypeStruct((t, WIDTH), F32)] * 2
        out_specs += [pl.BlockSpec((tm, WIDTH), tok)] * 2
    return pl.pallas_call(
        functools.partial(_proj_kernel, rope=rope, emit_cache=emit_cache),
        grid=(nt,),
        in_specs=in_specs,
        out_specs=out_specs,
        out_shape=out_shape,
        compiler_params=_params(("parallel",)),
        name="proj_rope" if rope else "proj_ctx",
    )(*args)


def _rope_tables(seq):
    t = jnp.arange(seq)
    row = (t // GRID_W).astype(F32)[:, None]
    col = (t % GRID_W).astype(F32)[:, None]
    lane = jnp.arange(LANES)
    freqs = ROPE_THETA ** (-(lane % 16).astype(F32) / 16.0)
    pos = jnp.where(((lane % 64) < 32)[None, :], row, col)
    ang = pos * freqs[None, :]
    cos, sin = jnp.cos(ang), jnp.sin(ang)
    first = ((lane % 32) < 16)[None, :]
    return cos, jnp.where(first, -sin, 0.0), jnp.where(first, 0.0, sin)


def _lane_scan(x, pos, op, forward):
    n = x.shape[-1]
    shift = 1
    while shift < n:
        if forward:
            y = pltpu.roll(x, shift, 1)
            ok = pos >= shift
        else:
            y = pltpu.roll(x, n - shift, 1)
            ok = pos < n - shift
        x = jnp.where(ok, op(x, y), x)
        shift *= 2
    return x


def _mlstm_kernel(q_ref, kt_ref, v_ref, g_ref, c0_ref, n0_ref, m0_ref,
                  hf_ref, hb_ref, cf_ref, nf_ref, mf_ref,
                  r_ref, mp_ref, mn_ref, dc_ref, tab_ref, *st_refs, nc, hp):
    L = CHUNK
    rows = nc * hp
    pos = lax.broadcasted_iota(jnp.int32, (rows, L), 1)
    tab_ref[...] = jnp.zeros_like(tab_ref)
    for d in range(2):
        i_pre = g_ref[0, 0, (2 * d) * rows:(2 * d + 1) * rows, :]
        lf = _log_sigmoid(g_ref[0, 0, (2 * d + 1) * rows:(2 * d + 2) * rows, :])
        pre = _lane_scan(lf, pos, jnp.add, True)
        tot = jnp.broadcast_to(jnp.sum(lf, axis=-1, keepdims=True), lf.shape)
        b = pre if d == 0 else tot - pre + lf
        r = i_pre - b
        r_max = _lane_scan(r, pos, jnp.maximum, d == 0)
        gk = tot - b + i_pre
        g_max = jnp.broadcast_to(jnp.max(gk, axis=-1, keepdims=True), gk.shape)
        m = m0_ref[0, d, :, 0, :]
        for c in (range(nc) if d == 0 else reversed(range(nc))):
            sl = slice(c * hp, (c + 1) * hp)
            mp_ref[d, sl, :] = m
            m = jnp.maximum(tot[sl] + m, g_max[sl])
            mn_ref[d, sl, :] = m
        mf_ref[0, d, :, 0, :] = m
        m_prev = mp_ref[d]
        m_new = mn_ref[d]
        m_row = jnp.maximum(m_prev, r_max)
        r_ref[d] = r
        dc_ref[d] = jnp.exp(tot + m_prev - m_new)
        w_k = jnp.exp(gk - m_new)
        floor = jnp.exp(-(b + m_row))
        for c in range(nc):
            src = slice(c * hp, (c + 1) * hp)
            slot = c if d == 0 else nc - 1 - c
            for qi, val in enumerate((m_row, floor, w_k)):
                base = (2 * qi + d) * hp
                tab_ref[slot, base:base + hp, :] = val[src]

    for d in range(2):
        for hh in range(hp):
            st_refs[d * hp + hh][:, :HEAD_DIM] = jnp.transpose(c0_ref[0, d, hh])
            st_refs[d * hp + hh][:, HEAD_DIM:] = jnp.transpose(
                jnp.broadcast_to(n0_ref[0, d, hh], (HEAD_DIM, HEAD_DIM)))

    rr = lax.broadcasted_iota(jnp.int32, (L, L), 0)
    cc = lax.broadcasted_iota(jnp.int32, (L, L), 1)
    masks = (cc <= rr, cc >= rr)
    ones = jnp.ones((L, HEAD_DIM), BF16)

    def body(c, carry):
        tab = jnp.transpose(tab_ref[c])
        units = [(d, hh) for d in range(2) for hh in range(hp)]

        def operands(d, hh):
            ci = c if d == 0 else nc - 1 - c
            off = pl.multiple_of(ci * L, L)
            lanes = slice(hh * HEAD_DIM, (hh + 1) * HEAD_DIM)
            return ci * hp + hh, off, lanes

        def column(d, hh, which):
            j = (2 * which + d) * hp + hh
            return jnp.broadcast_to(tab[:, j:j + 1], (L, HEAD_DIM))

        qk, inter, upd = {}, {}, {}
        for d, hh in units:
            row, off, lanes = operands(d, hh)
            q = q_ref[0, pl.ds(off, L), lanes]
            kt = kt_ref[lanes, pl.ds(off, L)]
            qk[d, hh] = jnp.dot(q, kt, preferred_element_type=F32)
            inter[d, hh] = jnp.dot(q, st_refs[d * hp + hh][...].astype(BF16), preferred_element_type=F32)
        for d, hh in units:
            row, off, lanes = operands(d, hh)
            v = v_ref[0, pl.ds(off, L), lanes]
            kt = kt_ref[lanes, pl.ds(off, L)]
            wk_col = column(d, hh, 2)
            vw = (v.astype(F32) * wk_col).astype(BF16)
            upd[d, hh] = jnp.dot(kt, jnp.concatenate([vw, wk_col.astype(BF16)], axis=1),
                                 preferred_element_type=F32)
        for d, hh in units:
            row, off, lanes = operands(d, hh)
            m_col = column(d, hh, 0)
            r_row = r_ref[d, pl.ds(row, 1), :]
            m_prev = mp_ref[d, pl.ds(row, 1), :]
            v = v_ref[0, pl.ds(off, L), lanes]
            w_intra = jnp.exp(jnp.where(masks[d], r_row - m_col, -jnp.inf))
            s = (qk[d, hh] * w_intra).astype(BF16)
            intra = jnp.dot(s, jnp.concatenate([v, ones], axis=1), preferred_element_type=F32)
            w_state = jnp.exp(m_prev - m_col)
            both = intra + jnp.concatenate([w_state, w_state], axis=1) * inter[d, hh]
            den = jnp.maximum(jnp.abs(both[:, HEAD_DIM:]), column(d, hh, 1))
            out_ref = hf_ref if d == 0 else hb_ref
            out_ref[0, pl.ds(off, L), lanes] = (both[:, :HEAD_DIM] / den).astype(BF16)
        for d, hh in units:
            row, off, lanes = operands(d, hh)
            decay = dc_ref[d, pl.ds(row, 1), :]
            st = st_refs[d * hp + hh]
            st[...] = jnp.concatenate([decay, decay], axis=1) * st[...] + upd[d, hh]
        return carry

    lax.fori_loop(0, nc, body, 0)

    for d in range(2):
        for hh in range(hp):
            state = st_refs[d * hp + hh][...]
            cf_ref[0, d, hh] = jnp.transpose(state[:, :HEAD_DIM])
            nf_ref[0, d, hh] = jnp.transpose(state[:, HEAD_DIM:])[0:1, :]


def _mlstm(qvo, kt, gt, c0, n0, m0, hp):
    b, s, _ = qvo.shape
    nc = s // CHUNK
    hg = N_HEADS // hp
    rows = nc * hp
    g6 = gt.reshape(4, hg, hp, b, nc, CHUNK).transpose(3, 1, 0, 4, 2, 5).reshape(b, hg, 4 * rows, CHUNK)
    n0r = n0.reshape(b, 2, N_HEADS, 1, HEAD_DIM)
    m0r = jnp.broadcast_to(m0[..., None, None], (b, 2, N_HEADS, 1, HEAD_DIM))
    blk = hp * HEAD_DIM
    nblk = WIDTH // blk
    state_spec = lambda w: pl.BlockSpec((1, 2, hp, w, HEAD_DIM), lambda i, j: (i, 0, j, 0, 0))
    in_specs = [pl.BlockSpec((1, s, blk), lambda i, j: (i, 0, j)),
                pl.BlockSpec((blk, s), lambda i, j: (j, i)),
                pl.BlockSpec((1, s, blk), lambda i, j: (i, 0, nblk + j)),
                pl.BlockSpec((1, 1, 4 * rows, CHUNK), lambda i, j: (i, j, 0, 0)),
                state_spec(HEAD_DIM), state_spec(1), state_spec(1)]
    out_specs = [pl.BlockSpec((1, s, blk), lambda i, j: (i, 0, j)),
                 pl.BlockSpec((1, s, blk), lambda i, j: (i, 0, j)),
                 state_spec(HEAD_DIM), state_spec(1), state_spec(1)]
    out_shape = [jax.ShapeDtypeStruct((b, s, WIDTH), BF16),
                 jax.ShapeDtypeStruct((b, s, WIDTH), BF16),
                 jax.ShapeDtypeStruct((b, 2, N_HEADS, HEAD_DIM, HEAD_DIM), F32),
                 jax.ShapeDtypeStruct((b, 2, N_HEADS, 1, HEAD_DIM), F32),
                 jax.ShapeDtypeStruct((b, 2, N_HEADS, 1, HEAD_DIM), F32)]
    scratch = [pltpu.VMEM((2, rows, CHUNK), F32),
               pltpu.VMEM((2, rows, CHUNK), F32),
               pltpu.VMEM((2, rows, CHUNK), F32),
               pltpu.VMEM((2, rows, CHUNK), F32),
               pltpu.VMEM((nc, LANES, CHUNK), F32)]
    scratch += [pltpu.VMEM((HEAD_DIM, 2 * HEAD_DIM), F32) for _ in range(2 * hp)]
    hf, hb, cf, nf, mf = pl.pallas_call(
        functools.partial(_mlstm_kernel, nc=nc, hp=hp),
        grid=(b, hg),
        in_specs=in_specs,
        out_specs=out_specs,
        out_shape=out_shape,
        scratch_shapes=scratch,
        compiler_params=_params(("parallel", "parallel")),
        name="mlstm",
    )(qvo, kt, qvo, g6, c0, n0r, m0r)
    return hf, hb, cf, nf[:, :, :, 0, :], mf[:, :, :, 0, 0]


def _attn_kernel(*refs, sub, key_chunk, n_new, n_cache, lam_init):
    if n_cache:
        lp_ref, q_ref, k_ref, kc_ref, vt_ref, vct_ref, o_ref, s_even_ref, s_odd_ref = refs
    else:
        lp_ref, q_ref, k_ref, vt_ref, o_ref, s_even_ref, s_odd_ref = refs
    lp = lp_ref[...]
    lam = (jnp.exp(jnp.sum(lp[0:1] * lp[1:2], axis=-1, keepdims=True))
           - jnp.exp(jnp.sum(lp[2:3] * lp[3:4], axis=-1, keepdims=True)) + lam_init)
    lane = lax.broadcasted_iota(jnp.int32, (sub, HEAD_DIM), 1)
    n_sub = n_new // sub
    nt = (((1,), (1,)), ((), ()))

    bounds = list(range(0, n_new, key_chunk)) + [n_new]
    chunks = [(False, c0, c0, c1 - c0) for c0, c1 in zip(bounds[:-1], bounds[1:])]
    if n_cache:
        chunks.append((True, 0, n_new, n_cache))

    def keys_of(chunk):
        cached, c0, _, n = chunk
        return kc_ref[0, c0:c0 + n, :] if cached else k_ref[0, c0:c0 + n, :]

    def values_of(chunk):
        cached, c0, _, n = chunk
        return vct_ref[0, :, c0:c0 + n] if cached else vt_ref[:, c0:c0 + n]

    def queries(idx):
        r0 = pl.multiple_of(idx * sub, sub)
        q = q_ref[0, pl.ds(r0, sub), :]
        zero = jnp.zeros_like(q)
        return jnp.concatenate([jnp.where(lane < QK_DIM, q, zero), jnp.where(lane >= QK_DIM, q, zero)], axis=0)

    def scores(qq, chunk, s_ref):
        s = lax.dot_general(keys_of(chunk), qq, nt, preferred_element_type=F32)
        s_ref[chunk[2]:chunk[2] + chunk[3], :] = s
        return jnp.max(s, axis=0, keepdims=True)

    def step(idx_next, s_next_ref, idx, s_ref, m):
        qq = queries(idx_next)
        m_next, l, o = None, None, None
        for chunk in chunks:
            mc = scores(qq, chunk, s_next_ref)
            m_next = mc if m_next is None else jnp.maximum(m_next, mc)
            e = jnp.exp2(s_ref[chunk[2]:chunk[2] + chunk[3], :] - m)
            lc = jnp.sum(e, axis=0, keepdims=True)
            oc = jnp.dot(values_of(chunk), e.astype(BF16), preferred_element_type=F32)
            l = lc if l is None else l + lc
            o = oc if o is None else o + oc
        o = o / l
        out_t = o[:, :sub] - lam * o[:, sub:]
        r0 = pl.multiple_of(idx * sub, sub)
        o_ref[0, pl.ds(r0, sub), :] = jnp.transpose(out_t).astype(BF16)
        return m_next

    def pair(i, m_even):
        a = 2 * i
        m_odd = step(a + 1, s_odd_ref, a, s_even_ref, m_even)
        return step(jnp.minimum(a + 2, n_sub - 1), s_even_ref, a + 1, s_odd_ref, m_odd)

    qq0 = queries(0)
    m0 = None
    for chunk in chunks:
        mc = scores(qq0, chunk, s_even_ref)
        m0 = mc if m0 is None else jnp.maximum(m0, mc)
    lax.fori_loop(0, n_sub // 2, pair, m0)


def _attn_short_kernel(lp_ref, q_ref, k_ref, vt_ref, o_ref, *, lam_init):
    lp = lp_ref[...]
    lam = (jnp.exp(jnp.sum(lp[0:1] * lp[1:2], axis=-1, keepdims=True))
           - jnp.exp(jnp.sum(lp[2:3] * lp[3:4], axis=-1, keepdims=True)) + lam_init)
    s_len = q_ref.shape[1]
    lane = lax.broadcasted_iota(jnp.int32, (s_len, HEAD_DIM), 1)
    nt = (((1,), (1,)), ((), ()))
    for hh in range(N_HEADS):
        lanes = slice(hh * HEAD_DIM, (hh + 1) * HEAD_DIM)
        q = q_ref[0, :, lanes]
        zero = jnp.zeros_like(q)
        qq = jnp.concatenate([jnp.where(lane < QK_DIM, q, zero), jnp.where(lane >= QK_DIM, q, zero)], axis=0)
        s = lax.dot_general(k_ref[0, :, lanes], qq, nt, preferred_element_type=F32)
        e = jnp.exp2(s - jnp.max(s, axis=0, keepdims=True))
        o = jnp.dot(vt_ref[lanes, :], e.astype(BF16), preferred_element_type=F32)
        o = o / jnp.sum(e, axis=0, keepdims=True)
        out_t = o[:, :s_len] - lam * o[:, s_len:]
        o_ref[0, :, lanes] = jnp.transpose(out_t).astype(BF16)


def _attn_short(lp, q, k, vt, lam_init):
    b, s, _ = q.shape
    return pl.pallas_call(
        functools.partial(_attn_short_kernel, lam_init=lam_init),
        grid=(b,),
        in_specs=[pl.BlockSpec(lp.shape, lambda i: (0, 0)),
                  pl.BlockSpec((1, s, WIDTH), lambda i: (i, 0, 0)),
                  pl.BlockSpec((1, s, WIDTH), lambda i: (i, 0, 0)),
                  pl.BlockSpec((WIDTH, s), lambda i: (0, i))],
        out_specs=pl.BlockSpec((1, s, WIDTH), lambda i: (i, 0, 0)),
        out_shape=jax.ShapeDtypeStruct((b, s, WIDTH), BF16),
        compiler_params=_params(("parallel",)),
        name="diff_attn_short",
    )(lp, q, k, vt)


def _attn(lp, q, k, vt, cache, lam_init):
    b, s, _ = q.shape
    if cache is None and s <= ATTN_SHORT:
        return _attn_short(lp, q, k, vt, lam_init)
    assert s % (2 * ATTN_SUB) == 0
    n_cache = 0 if cache is None else cache[0].shape[1]
    in_specs = [pl.BlockSpec(lp.shape, lambda i, h: (0, 0)),
                pl.BlockSpec((1, s, HEAD_DIM), lambda i, h: (i, 0, h)),
                pl.BlockSpec((1, s, HEAD_DIM), lambda i, h: (i, 0, h))]
    args = [lp, q, k]
    if n_cache:
        in_specs.append(pl.BlockSpec((1, n_cache, HEAD_DIM), lambda i, h: (i, 0, h)))
        args.append(cache[0])
    in_specs.append(pl.BlockSpec((HEAD_DIM, s), lambda i, h: (h, i)))
    args.append(vt)
    if n_cache:
        in_specs.append(pl.BlockSpec((1, HEAD_DIM, n_cache), lambda i, h: (i, h, 0)))
        args.append(cache[1])
    keys = s + n_cache
    return pl.pallas_call(
        functools.partial(_attn_kernel, sub=ATTN_SUB, key_chunk=min(ATTN_KEY_CHUNK, s), n_new=s, n_cache=n_cache,
                          lam_init=lam_init),
        grid=(b, N_HEADS),
        in_specs=in_specs,
        out_specs=pl.BlockSpec((1, s, HEAD_DIM), lambda i, h: (i, 0, h)),
        out_shape=jax.ShapeDtypeStruct((b, s, WIDTH), BF16),
        scratch_shapes=[pltpu.VMEM((keys, 2 * ATTN_SUB), F32), pltpu.VMEM((keys, 2 * ATTN_SUB), F32)],
        compiler_params=_params(("parallel", "parallel")),
        name="diff_attn",
    )(*args)


def _head_norm(x):
    outs = []
    for j in range(N_HEADS):
        xb = x[:, j * HEAD_DIM:(j + 1) * HEAD_DIM]
        outs.append(xb * lax.rsqrt(jnp.mean(xb * xb, axis=-1, keepdims=True) + EPS))
    return jnp.concatenate(outs, axis=1)


def _mix_kernel(hf_ref, hb_ref, om_ref, hd_ref, x_ref, g1_ref, nsc2_ref, sh2_ref, gm_ref, gd_ref,
                wout_ref, wr_ref, br_ref, x1_ref, h2_ref, chl_ref, grpt_ref, *, diff_scale, parts):
    n = x_ref.shape[0] // parts
    for p in range(parts):
        _mix_rows(slice(p * n, (p + 1) * n), hf_ref, hb_ref, om_ref, hd_ref, x_ref, g1_ref, nsc2_ref, sh2_ref,
                  gm_ref, gd_ref, wout_ref, wr_ref, br_ref, x1_ref, h2_ref, chl_ref, grpt_ref, diff_scale)


def _mix_rows(rows, hf_ref, hb_ref, om_ref, hd_ref, x_ref, g1_ref, nsc2_ref, sh2_ref, gm_ref, gd_ref,
              wout_ref, wr_ref, br_ref, x1_ref, h2_ref, chl_ref, grpt_ref, diff_scale):
    hm = _head_norm(hf_ref[rows, :].astype(F32) + hb_ref[rows, :].astype(F32))
    hm = hm * gm_ref[...] * _sigmoid(om_ref[rows, :].astype(F32))
    hd = _head_norm(hd_ref[rows, :].astype(F32)) * gd_ref[...] * diff_scale
    mix = jnp.dot(jnp.concatenate([hm.astype(BF16), hd.astype(BF16)], axis=1), wout_ref[...],
                  preferred_element_type=F32)
    x1 = x_ref[rows, :] + g1_ref[0] * mix
    x1_ref[rows, :] = x1
    ms = jnp.mean(x1 * x1, axis=-1, keepdims=True)
    h2 = x1 * lax.rsqrt(ms + EPS) * nsc2_ref[0] + sh2_ref[0]
    h2_hi = h2.astype(BF16)
    h2_ref[rows, :] = h2_hi

    h2_lo = (h2 - h2_hi.astype(F32)).astype(BF16)
    both = (jnp.dot(h2_hi, wr_ref[...], preferred_element_type=F32)
            + jnp.dot(h2_lo, wr_ref[...], preferred_element_type=F32))
    logits = both[:, :LANES] + both[:, LANES:] + br_ref[...]
    lane = lax.broadcasted_iota(jnp.int32, logits.shape, 1).astype(F32)
    big = float(LANES)
    is_grp = (lane >= N_EXPERTS) & (lane < N_EXPERTS + N_GROUPS)
    lg = jnp.where(is_grp, logits, -jnp.inf)
    mx = jnp.max(lg, axis=-1, keepdims=True)
    p_grp = 1.0 / jnp.sum(jnp.exp(lg - mx), axis=-1, keepdims=True)
    grp = jnp.min(jnp.where(lg == mx, lane, big), axis=-1, keepdims=True) - N_EXPERTS
    base = grp * EXPERTS_PER_GROUP
    sel = (lane >= base) & (lane < base + EXPERTS_PER_GROUP)
    le = jnp.where(sel, logits, -jnp.inf)
    ee = jnp.exp(le - jnp.max(le, axis=-1, keepdims=True))
    pe = ee / jnp.sum(ee, axis=-1, keepdims=True)
    top1 = jnp.max(pe, axis=-1, keepdims=True)
    idx1 = jnp.min(jnp.where(sel & (pe == top1), lane, big), axis=-1, keepdims=True)
    rest = sel & (lane != idx1)
    top2 = jnp.max(jnp.where(rest, pe, -1.0), axis=-1, keepdims=True)
    idx2 = jnp.min(jnp.where(rest & (pe == top2), lane, big), axis=-1, keepdims=True)
    denom = top1 + top2
    w1 = top1 / denom * p_grp
    w2 = top2 / denom * p_grp
    chl = jnp.zeros_like(logits)
    for j in range(EXPERTS_PER_GROUP):
        cj = jnp.where(idx1 == base + j, w1, 0.0) + jnp.where(idx2 == base + j, w2, 0.0)
        hi = cj.astype(BF16).astype(F32)
        chl = chl + jnp.where(lane == j, hi, 0.0) + jnp.where(lane == EXPERTS_PER_GROUP + j, cj - hi, 0.0)
    chl_ref[rows, :] = chl.astype(BF16)
    grpt_ref[:, rows] = jnp.transpose(jnp.broadcast_to(grp, logits.shape))[:SUBLANES, :]


def _mix(hf, hb, qvo, hd, x2d, g1, nsc2, sh2, row0, tiles_per_row, gm, gd, wout, wr, br, diff_scale, tm):
    t, d = x2d.shape
    row_map = lambda i: (row0 + i // tiles_per_row, 0, 0)
    tok = lambda i: (i, 0)
    full = lambda i: (0, 0)
    return pl.pallas_call(
        functools.partial(_mix_kernel, diff_scale=diff_scale, parts=4),
        grid=(t // tm,),
        in_specs=[pl.BlockSpec((tm, WIDTH), tok),
                  pl.BlockSpec((tm, WIDTH), tok),
                  pl.BlockSpec((tm, WIDTH), lambda i: (i, 2)),
                  pl.BlockSpec((tm, WIDTH), tok),
                  pl.BlockSpec((tm, d), tok),
                  pl.BlockSpec((1, 1, d), row_map),
                  pl.BlockSpec((1, 1, d), row_map),
                  pl.BlockSpec((1, 1, d), row_map),
                  pl.BlockSpec((1, WIDTH), full),
                  pl.BlockSpec((1, WIDTH), full),
                  pl.BlockSpec(wout.shape, full),
                  pl.BlockSpec(wr.shape, full),
                  pl.BlockSpec(br.shape, full)],
        out_specs=[pl.BlockSpec((tm, d), tok),
                   pl.BlockSpec((tm, d), tok),
                   pl.BlockSpec((tm, LANES), tok),
                   pl.BlockSpec((SUBLANES, tm), lambda i: (0, i))],
        out_shape=[jax.ShapeDtypeStruct((t, d), F32),
                   jax.ShapeDtypeStruct((t, d), BF16),
                   jax.ShapeDtypeStruct((t, LANES), BF16),
                   jax.ShapeDtypeStruct((SUBLANES, t), F32)],
        compiler_params=_params(("parallel",)),
        name="mix_router",
    )(hf, hb, qvo, hd, x2d, g1, nsc2, sh2, gm, gd, wout, wr, br)


def _moe_kernel(h2_ref, chl_ref, grpt_ref, wgu_ref, wdn_ref, x1_ref, g2_ref, nrm_ref, y_ref,
                acc_ref, rank_ref, *, hidden, blk):
    g = pl.program_id(1)
    tm = h2_ref.shape[0]

    @pl.when(g == 0)
    def _():
        acc_ref[...] = jnp.zeros_like(acc_ref)
        gid = lax.broadcasted_iota(jnp.int32, (SUBLANES, tm), 0).astype(F32)
        member = grpt_ref[...] == gid
        src = lax.broadcasted_iota(jnp.int32, (tm, tm), 0)
        dst = lax.broadcasted_iota(jnp.int32, (tm, tm), 1)
        before = jnp.where(src < dst, 1.0, 0.0).astype(BF16)
        rank = jnp.dot(jnp.where(member, 1.0, 0.0).astype(BF16), before, preferred_element_type=F32)
        rank_ref[...] = jnp.where(member, rank, -1.0)

    rank_g = rank_ref[pl.ds(g, 1), :]
    n_g = jnp.max(rank_g) + 1.0

    sizes, start = [], 0
    while start < tm:
        size = min(blk // 4 if len(sizes) == 1 else blk, tm - start)
        sizes.append((start, size))
        start += size

    for start, size in sizes:
        @pl.when(n_g > start)
        def _():
            slot = lax.broadcasted_iota(jnp.int32, (size, tm), 0).astype(F32) + float(start)
            pick = jnp.where(rank_g == slot, 1.0, 0.0).astype(BF16)
            xg = jnp.dot(pick, h2_ref[...], preferred_element_type=F32).astype(BF16)
            cg = jnp.dot(pick, chl_ref[...], preferred_element_type=F32)
            gus = [jnp.dot(xg, wgu_ref[e], preferred_element_type=F32) for e in range(EXPERTS_PER_GROUP)]
            out = jnp.zeros((size, acc_ref.shape[1]), F32)
            for e, gu in enumerate(gus):
                a, u = gu[:, :hidden], gu[:, hidden:]
                ce = cg[:, e:e + 1] + cg[:, EXPERTS_PER_GROUP + e:EXPERTS_PER_GROUP + e + 1]
                act = (a * _sigmoid(a) * u * ce).astype(BF16)
                out = out + jnp.dot(act, wdn_ref[e], preferred_element_type=F32)
            acc_ref[...] += lax.dot_general(pick, out.astype(BF16), (((0,), (0,)), ((), ())),
                                            preferred_element_type=F32)

    @pl.when(g == pl.num_programs(1) - 1)
    def _():
        x2 = x1_ref[...] + g2_ref[0] * acc_ref[...]
        ms = jnp.mean(x2 * x2, axis=-1, keepdims=True)
        y_ref[...] = x2 * lax.rsqrt(ms + EPS) * nrm_ref[...]


def _moe(h2, chl, grpt, wgu, wdn, x1, g2, row0, tiles_per_row, nrm, tm, blk):
    t, d = x1.shape
    _, _, two_h = wgu.shape
    hidden = two_h // 2
    row_map = lambda i, g: (row0 + i // tiles_per_row, 0, 0)
    tok = lambda i, g: (i, 0)
    return pl.pallas_call(
        functools.partial(_moe_kernel, hidden=hidden, blk=blk),
        grid=(t // tm, N_GROUPS),
        in_specs=[pl.BlockSpec((tm, d), tok),
                  pl.BlockSpec((tm, LANES), tok),
                  pl.BlockSpec((SUBLANES, tm), lambda i, g: (0, i)),
                  pl.BlockSpec((EXPERTS_PER_GROUP, d, two_h), lambda i, g: (g, 0, 0)),
                  pl.BlockSpec((EXPERTS_PER_GROUP, hidden, d), lambda i, g: (g, 0, 0)),
                  pl.BlockSpec((tm, d), tok),
                  pl.BlockSpec((1, 1, d), row_map),
                  pl.BlockSpec((1, d), lambda i, g: (0, 0))],
        out_specs=pl.BlockSpec((tm, d), tok),
        out_shape=jax.ShapeDtypeStruct((t, d), F32),
        scratch_shapes=[pltpu.VMEM((tm, d), F32), pltpu.VMEM((SUBLANES, tm), F32)],
        compiler_params=_params(("parallel", "arbitrary")),
        name="moe_experts",
    )(h2, chl, grpt, wgu, wdn, x1, g2, nrm)


def _layer(x, mods, row0, per_request, weights, states, cache, rope_tabs, emit_cache, tm, hp):
    b, s, d = x.shape
    sh1, sc1, g1, sh2, sc2, g2 = mods
    (nrm_mix, nrm_ffn, w_in_r, w_in_t, b_gate, gm, gd, lp, wout, wr, br, wgu, wdn, nrm_final, lam_init) = weights
    x2d = x.reshape(b * s, d)
    tiles_per_row = (s // tm) if per_request else (b * s // tm)
    outs = _proj(x2d, nrm_mix, sc1, sh1, row0, tiles_per_row, w_in_r, w_in_t, b_gate, rope_tabs, emit_cache, tm)
    qvo, kmt, qd, kd, vdt, gt = outs[:6]
    c0, n0, m0 = states
    hf, hb, cf, nf, mf = _mlstm(qvo.reshape(b, s, 3 * WIDTH), kmt, gt, c0, n0, m0, hp)
    hd = _attn(lp, qd.reshape(b, s, WIDTH), kd.reshape(b, s, WIDTH), vdt, cache, lam_init)
    x1, h2, chl, grpt = _mix(hf.reshape(b * s, WIDTH), hb.reshape(b * s, WIDTH), qvo, hd.reshape(b * s, WIDTH),
                             x2d, g1, nrm_ffn[None] * (1.0 + sc2), sh2, row0, tiles_per_row, gm, gd, wout, wr, br,
                             1.0 - lam_init, tm)
    tm_moe = 2 * tm
    y = _moe(h2, chl, grpt, wgu, wdn, x1, g2, row0, tiles_per_row // 2 if per_request else b * s // tm_moe,
             nrm_final, tm_moe, MOE_BLOCK)
    new_kv = outs[6:] if emit_cache else None
    return y.reshape(b, s, d), new_kv, (cf, nf, mf)


def kernel(x_prompt, x_sample, cache_k, cache_v, state_C, state_n, state_m, c, c_ctx, w_ada, b_ada, norm_mix, norm_ffn, w_in, b_gate, mlstm_norm, diff_norm, diff_lambda, w_out, w_route_group, b_route_group, w_route_expert, b_route_expert, w_gate_up, w_down, final_norm):
    depth = w_in.shape[0]
    assert depth == 1, "the final norm is fused into the last layer's expert kernel"
    bp, sp, d = x_prompt.shape
    bs, ss, _ = x_sample.shape
    rows = 16
    assert bs < rows
    cvec = jnp.zeros((rows, d), F32).at[:bs].set(c).at[bs].set(c_ctx)
    rope_tabs = _rope_tables(ss)
    xp, xs = x_prompt, x_sample
    new_k, new_v, new_c, new_n, new_m = [], [], [], [], []
    for l in range(depth):
        lam_init = 0.8 - 0.6 * math.exp(-0.3 * l)
        mod = _ada(cvec, w_ada[l], b_ada[l])
        mods = [mod[:, i * d:(i + 1) * d].reshape(rows, 1, d) for i in range(6)]
        wl = w_in[l]
        off_dq = 4 * WIDTH + N_GATES
        w_in_r = jnp.concatenate(
            [wl[:, :WIDTH], wl[:, 2 * WIDTH:4 * WIDTH], wl[:, off_dq:]], axis=1).astype(BF16)
        w_in_t = jnp.concatenate(
            [wl[:, WIDTH:2 * WIDTH], wl[:, off_dq + 2 * WIDTH:], wl[:, 4 * WIDTH:off_dq],
             jnp.zeros((d, LANES - N_GATES), F32)], axis=1).T.astype(BF16)
        bg = b_gate[l].reshape(N_GATES, 1)
        wr = jnp.concatenate([w_route_expert[l], w_route_group[l],
                              jnp.zeros((d, LANES - N_EXPERTS - N_GROUPS), F32)], axis=1)
        wr_hi = wr.astype(BF16)
        wr = jnp.concatenate([wr_hi, (wr - wr_hi.astype(F32)).astype(BF16)], axis=1)
        br = jnp.concatenate([b_route_expert[l], b_route_group[l],
                              jnp.zeros((LANES - N_EXPERTS - N_GROUPS,), F32)]).reshape(1, LANES)
        weights = (norm_mix[l].reshape(1, d), norm_ffn[l].reshape(1, d), w_in_r, w_in_t, bg,
                   mlstm_norm[l].reshape(1, WIDTH), diff_norm[l].reshape(1, WIDTH), diff_lambda[l],
                   w_out[l].astype(BF16), wr, br, w_gate_up[l].astype(BF16), w_down[l].astype(BF16),
                   final_norm.reshape(1, d), lam_init)

        zero_states = (jnp.zeros((bp, 2, N_HEADS, HEAD_DIM, HEAD_DIM), F32),
                       jnp.zeros((bp, 2, N_HEADS, HEAD_DIM), F32),
                       jnp.zeros((bp, 2, N_HEADS), F32))
        xp, (nk, nv), (cf, nf, mf) = _layer(xp, mods, bs, False, weights, zero_states, None, None, True,
                                            tm=512, hp=4)
        new_k.append(nk.reshape(bp, sp, N_HEADS, 2, QK_DIM))
        new_v.append(nv.reshape(bp, sp, N_HEADS, HEAD_DIM))
        new_c.append(cf)
        new_n.append(nf)
        new_m.append(mf)

        states = (state_C[:, l], state_n[:, l], state_m[:, l])
        cache = (cache_k[:, l].reshape(bs, -1, WIDTH).astype(BF16),
                 cache_v[:, l].reshape(bs, -1, WIDTH).transpose(0, 2, 1).astype(BF16))
        xs, _, _ = _layer(xs, mods, 0, True, weights, states, cache, rope_tabs, False,
                          tm=512, hp=4)

    return (xp, xs, jnp.stack(new_k, axis=1), jnp.stack(new_v, axis=1),
            jnp.stack(new_c, axis=1), jnp.stack(new_n, axis=1), jnp.stack(new_m, axis=1))
```

```python
import functools
import math

import jax
import jax.numpy as jnp
from jax import lax
from jax.experimental import pallas as pl
from jax.experimental.pallas import tpu as pltpu

F32 = jnp.float32
BF16 = jnp.bfloat16
HIGHEST = lax.Precision.HIGHEST

EPS = 1e-6
GRID_W = 64
ROPE_THETA = 10000.0
N_HEADS = 4
HEAD_DIM = 128
QK_DIM = 64
WIDTH = N_HEADS * HEAD_DIM
CHUNK = 128
N_GATES = 4 * N_HEADS
N_GROUPS = 4
EXPERTS_PER_GROUP = 4
N_EXPERTS = 16
LANES = 128
SUBLANES = 8
MOE_BLOCK = 256
ATTN_SUB = 128
ATTN_KEY_CHUNK = 1024
ATTN_SHORT = 512
LOG2E = math.log2(math.e)
VMEM_LIMIT = 56 * 1024 * 1024


def _params(sem):
    return pltpu.CompilerParams(dimension_semantics=sem, vmem_limit_bytes=VMEM_LIMIT)


def _log_sigmoid(x):
    return jnp.minimum(x, 0.0) - jnp.log1p(jnp.exp(-jnp.abs(x)))


def _sigmoid(x):
    return 1.0 / (1.0 + jnp.exp(-x))


def _ada_kernel(c_ref, w_ref, b_ref, o_ref):
    c = c_ref[...]
    s = c * _sigmoid(c)
    o_ref[...] = jnp.dot(s, w_ref[...], preferred_element_type=F32, precision=HIGHEST) + b_ref[...]


def _ada(cvec, w_ada, b_ada):
    rows, d = cvec.shape
    n = w_ada.shape[1]
    tn = 1024
    return pl.pallas_call(
        _ada_kernel,
        grid=(n // tn,),
        in_specs=[pl.BlockSpec((rows, d), lambda j: (0, 0)),
                  pl.BlockSpec((d, tn), lambda j: (0, j)),
                  pl.BlockSpec((1, tn), lambda j: (0, j))],
        out_specs=pl.BlockSpec((rows, tn), lambda j: (0, j)),
        out_shape=jax.ShapeDtypeStruct((rows, n), F32),
        compiler_params=_params(("arbitrary",)),
        name="ada",
    )(cvec, w_ada, b_ada.reshape(1, n))


def _proj_kernel(*refs, rope, emit_cache):
    x_ref, nrm_ref, sc_ref, sh_ref, w_ref, wt_ref, bg_ref = refs[:7]
    refs = refs[7:]
    if rope:
        cos_ref, sa_ref, sb_ref = refs[:3]
        refs = refs[3:]
    m_ref, kmt_ref, q_ref, k_ref, vt_ref, gt_ref = refs[:6]
    refs = refs[6:]
    if emit_cache:
        newk_ref, newv_ref = refs

    x = x_ref[...]
    ms = jnp.mean(x * x, axis=-1, keepdims=True)
    y = x * lax.rsqrt(ms + EPS) * nrm_ref[...]
    h = (y * (1.0 + sc_ref[0]) + sh_ref[0]).astype(BF16)

    def section(j):
        return jnp.dot(h, w_ref[:, j * WIDTH:(j + 1) * WIDTH], preferred_element_type=F32)

    def rotate(p):
        cos, sa, sb = cos_ref[...], sa_ref[...], sb_ref[...]
        outs = []
        for j in range(N_HEADS):
            xb = p[:, j * LANES:(j + 1) * LANES]
            outs.append(xb * cos + pltpu.roll(xb, LANES - 16, 1) * sa + pltpu.roll(xb, 16, 1) * sb)
        return jnp.concatenate(outs, axis=1)

    def section_t(j, n):
        return lax.dot_general(wt_ref[j * WIDTH:j * WIDTH + n, :], h, (((1,), (1,)), ((), ())),
                               preferred_element_type=F32)

    for j in range(3):
        m_ref[:, j * WIDTH:(j + 1) * WIDTH] = section(j).astype(BF16)
    kmt_ref[...] = (section_t(0, WIDTH) * (HEAD_DIM ** -0.5)).astype(BF16)
    pq = section(3)
    if rope:
        pq = rotate(pq)
    q_ref[...] = (pq * (LOG2E * QK_DIM ** -0.5)).astype(BF16)
    pk = section(4)
    if emit_cache:
        newk_ref[...] = pk
    if rope:
        pk = rotate(pk)
    k_ref[...] = pk.astype(BF16)
    if emit_cache:
        newv_ref[...] = section(5)
    vt_ref[...] = section_t(1, WIDTH).astype(BF16)
    gt_ref[...] = section_t(2, LANES)[:N_GATES, :] + bg_ref[...]


def _proj(x2d, nrm, sc, sh, row0, tiles_per_row, w_r, w_t, b_gate, rope_tabs, emit_cache, tm):
    t, d = x2d.shape
    nt = t // tm
    rope = rope_tabs is not None
    row_map = lambda i: (row0 + i // tiles_per_row, 0, 0)
    tok = lambda i: (i, 0)
    feat = lambda i: (0, i)
    in_specs = [pl.BlockSpec((tm, d), tok),
                pl.BlockSpec((1, d), lambda i: (0, 0)),
                pl.BlockSpec((1, 1, d), row_map),
                pl.BlockSpec((1, 1, d), row_map),
                pl.BlockSpec(w_r.shape, lambda i: (0, 0)),
                pl.BlockSpec(w_t.shape, lambda i: (0, 0)),
                pl.BlockSpec((N_GATES, 1), lambda i: (0, 0))]
    args = [x2d, nrm, sc, sh, w_r, w_t, b_gate]
    if rope:
        tps = rope_tabs[0].shape[0] // tm
        for tab in rope_tabs:
            in_specs.append(pl.BlockSpec((tm, LANES), lambda i: (i % tps, 0)))
            args.append(tab)
    out_shape = [jax.ShapeDtypeStruct((t, 3 * WIDTH), BF16),
                 jax.ShapeDtypeStruct((WIDTH, t), BF16),
                 jax.ShapeDtypeStruct((t, WIDTH), BF16),
                 jax.ShapeDtypeStruct((t, WIDTH), BF16),
                 jax.ShapeDtypeStruct((WIDTH, t), BF16),
                 jax.ShapeDtypeStruct((N_GATES, t), F32)]
    out_specs = [pl.BlockSpec((tm, 3 * WIDTH), tok),
                 pl.BlockSpec((WIDTH, tm), feat),
                 pl.BlockSpec((tm, WIDTH), tok),
                 pl.BlockSpec((tm, WIDTH), tok),
                 pl.BlockSpec((WIDTH, tm), feat),
                 pl.BlockSpec((N_GATES, tm), feat)]
    if emit_cache:
        out_shape += [jax.ShapeDtypeStruct((t, WIDTH), F32)] * 2
        out_specs += [pl.BlockSpec((tm, WIDTH), tok)] * 2
    return pl.pallas_call(
        functools.partial(_proj_kernel, rope=rope, emit_cache=emit_cache),
        grid=(nt,),
        in_specs=in_specs,
        out_specs=out_specs,
        out_shape=out_shape,
        compiler_params=_params(("parallel",)),
        name="proj_rope" if rope else "proj_ctx",
    )(*args)


def _rope_tables(seq):
    t = jnp.arange(seq)
    row = (t // GRID_W).astype(F32)[:, None]
    col = (t % GRID_W).astype(F32)[:, None]
    lane = jnp.arange(LANES)
    freqs = ROPE_THETA ** (-(lane % 16).astype(F32) / 16.0)
    pos = jnp.where(((lane % 64) < 32)[None, :], row, col)
    ang = pos * freqs[None, :]
    cos, sin = jnp.cos(ang), jnp.sin(ang)
    first = ((lane % 32) < 16)[None, :]
    return cos, jnp.where(first, -sin, 0.0), jnp.where(first, 0.0, sin)


def _lane_scan(x, pos, op, forward):
    n = x.shape[-1]
    shift = 1
    while shift < n:
        if forward:
            y = pltpu.roll(x, shift, 1)
            ok = pos >= shift
        else:
            y = pltpu.roll(x, n - shift, 1)
            ok = pos < n - shift
        x = jnp.where(ok, op(x, y), x)
        shift *= 2
    return x


def _mlstm_kernel(q_ref, kt_ref, v_ref, g_ref, c0_ref, n0_ref, m0_ref,
                  hf_ref, hb_ref, cf_ref, nf_ref, mf_ref,
                  r_ref, mp_ref, mn_ref, dc_ref, tab_ref, *st_refs, nc, hp):
    L = CHUNK
    rows = nc * hp
    pos = lax.broadcasted_iota(jnp.int32, (rows, L), 1)
    tab_ref[...] = jnp.zeros_like(tab_ref)
    for d in range(2):
        i_pre = g_ref[0, 0, (2 * d) * rows:(2 * d + 1) * rows, :]
        lf = _log_sigmoid(g_ref[0, 0, (2 * d + 1) * rows:(2 * d + 2) * rows, :])
        pre = _lane_scan(lf, pos, jnp.add, True)
        tot = jnp.broadcast_to(jnp.sum(lf, axis=-1, keepdims=True), lf.shape)
        b = pre if d == 0 else tot - pre + lf
        r = i_pre - b
        r_max = _lane_scan(r, pos, jnp.maximum, d == 0)
        gk = tot - b + i_pre
        g_max = jnp.broadcast_to(jnp.max(gk, axis=-1, keepdims=True), gk.shape)
        m = m0_ref[0, d, :, 0, :]
        for c in (range(nc) if d == 0 else reversed(range(nc))):
            sl = slice(c * hp, (c + 1) * hp)
            mp_ref[d, sl, :] = m
            m = jnp.maximum(tot[sl] + m, g_max[sl])
            mn_ref[d, sl, :] = m
        mf_ref[0, d, :, 0, :] = m
        m_prev = mp_ref[d]
        m_new = mn_ref[d]
        m_row = jnp.maximum(m_prev, r_max)
        r_ref[d] = r
        dc_ref[d] = jnp.exp(tot + m_prev - m_new)
        w_k = jnp.exp(gk - m_new)
        floor = jnp.exp(-(b + m_row))
        for c in range(nc):
            src = slice(c * hp, (c + 1) * hp)
            slot = c if d == 0 else nc - 1 - c
            for qi, val in enumerate((m_row, floor, w_k)):
                base = (2 * qi + d) * hp
                tab_ref[slot, base:base + hp, :] = val[src]

    for d in range(2):
        for hh in range(hp):
            st_refs[d * hp + hh][:, :HEAD_DIM] = jnp.transpose(c0_ref[0, d, hh])
            st_refs[d * hp + hh][:, HEAD_DIM:] = jnp.transpose(
                jnp.broadcast_to(n0_ref[0, d, hh], (HEAD_DIM, HEAD_DIM)))

    rr = lax.broadcasted_iota(jnp.int32, (L, L), 0)
    cc = lax.broadcasted_iota(jnp.int32, (L, L), 1)
    masks = (cc <= rr, cc >= rr)
    ones = jnp.ones((L, HEAD_DIM), BF16)

    def body(c, carry):
        tab = jnp.transpose(tab_ref[c])
        units = [(d, hh) for d in range(2) for hh in range(hp)]

        def operands(d, hh):
            ci = c if d == 0 else nc - 1 - c
            off = pl.multiple_of(ci * L, L)
            lanes = slice(hh * HEAD_DIM, (hh + 1) * HEAD_DIM)
            return ci * hp + hh, off, lanes

        def column(d, hh, which):
            j = (2 * which + d) * hp + hh
            return jnp.broadcast_to(tab[:, j:j + 1], (L, HEAD_DIM))

        qk, inter, upd = {}, {}, {}
        for d, hh in units:
            row, off, lanes = operands(d, hh)
            q = q_ref[0, pl.ds(off, L), lanes]
            kt = kt_ref[lanes, pl.ds(off, L)]
            qk[d, hh] = jnp.dot(q, kt, preferred_element_type=F32)
            inter[d, hh] = jnp.dot(q, st_refs[d * hp + hh][...].astype(BF16), preferred_element_type=F32)
        for d, hh in units:
            row, off, lanes = operands(d, hh)
            v = v_ref[0, pl.ds(off, L), lanes]
            kt = kt_ref[lanes, pl.ds(off, L)]
            wk_col = column(d, hh, 2)
            vw = (v.astype(F32) * wk_col).astype(BF16)
            upd[d, hh] = jnp.dot(kt, jnp.concatenate([vw, wk_col.astype(BF16)], axis=1),
                                 preferred_element_type=F32)
        for d, hh in units:
            row, off, lanes = operands(d, hh)
            m_col = column(d, hh, 0)
            r_row = r_ref[d, pl.ds(row, 1), :]
            m_prev = mp_ref[d, pl.ds(row, 1), :]
            v = v_ref[0, pl.ds(off, L), lanes]
            w_intra = jnp.exp(jnp.where(masks[d], r_row - m_col, -jnp.inf))
            s = (qk[d, hh] * w_intra).astype(BF16)
            intra = jnp.dot(s, jnp.concatenate([v, ones], axis=1), preferred_element_type=F32)
            w_state = jnp.exp(m_prev - m_col)
            both = intra + jnp.concatenate([w_state, w_state], axis=1) * inter[d, hh]
            den = jnp.maximum(jnp.abs(both[:, HEAD_DIM:]), column(d, hh, 1))
            out_ref = hf_ref if d == 0 else hb_ref
            out_ref[0, pl.ds(off, L), lanes] = (both[:, :HEAD_DIM] / den).astype(BF16)
        for d, hh in units:
            row, off, lanes = operands(d, hh)
            decay = dc_ref[d, pl.ds(row, 1), :]
            st = st_refs[d * hp + hh]
            st[...] = jnp.concatenate([decay, decay], axis=1) * st[...] + upd[d, hh]
        return carry

    lax.fori_loop(0, nc, body, 0, unroll=2)

    for d in range(2):
        for hh in range(hp):
            state = st_refs[d * hp + hh][...]
            cf_ref[0, d, hh] = jnp.transpose(state[:, :HEAD_DIM])
            nf_ref[0, d, hh] = jnp.transpose(state[:, HEAD_DIM:])[0:1, :]


def _mlstm(qvo, kt, gt, c0, n0, m0, hp):
    b, s, _ = qvo.shape
    nc = s // CHUNK
    hg = N_HEADS // hp
    rows = nc * hp
    g6 = gt.reshape(4, hg, hp, b, nc, CHUNK).transpose(3, 1, 0, 4, 2, 5).reshape(b, hg, 4 * rows, CHUNK)
    n0r = n0.reshape(b, 2, N_HEADS, 1, HEAD_DIM)
    m0r = jnp.broadcast_to(m0[..., None, None], (b, 2, N_HEADS, 1, HEAD_DIM))
    blk = hp * HEAD_DIM
    nblk = WIDTH // blk
    state_spec = lambda w: pl.BlockSpec((1, 2, hp, w, HEAD_DIM), lambda i, j: (i, 0, j, 0, 0))
    in_specs = [pl.BlockSpec((1, s, blk), lambda i, j: (i, 0, j)),
                pl.BlockSpec((blk, s), lambda i, j: (j, i)),
                pl.BlockSpec((1, s, blk), lambda i, j: (i, 0, nblk + j)),
                pl.BlockSpec((1, 1, 4 * rows, CHUNK), lambda i, j: (i, j, 0, 0)),
                state_spec(HEAD_DIM), state_spec(1), state_spec(1)]
    out_specs = [pl.BlockSpec((1, s, blk), lambda i, j: (i, 0, j)),
                 pl.BlockSpec((1, s, blk), lambda i, j: (i, 0, j)),
                 state_spec(HEAD_DIM), state_spec(1), state_spec(1)]
    out_shape = [jax.ShapeDtypeStruct((b, s, WIDTH), BF16),
                 jax.ShapeDtypeStruct((b, s, WIDTH), BF16),
                 jax.ShapeDtypeStruct((b, 2, N_HEADS, HEAD_DIM, HEAD_DIM), F32),
                 jax.ShapeDtypeStruct((b, 2, N_HEADS, 1, HEAD_DIM), F32),
                 jax.ShapeDtypeStruct((b, 2, N_HEADS, 1, HEAD_DIM), F32)]
    scratch = [pltpu.VMEM((2, rows, CHUNK), F32),
               pltpu.VMEM((2, rows, CHUNK), F32),
               pltpu.VMEM((2, rows, CHUNK), F32),
               pltpu.VMEM((2, rows, CHUNK), F32),
               pltpu.VMEM((nc, LANES, CHUNK), F32)]
    scratch += [pltpu.VMEM((HEAD_DIM, 2 * HEAD_DIM), F32) for _ in range(2 * hp)]
    hf, hb, cf, nf, mf = pl.pallas_call(
        functools.partial(_mlstm_kernel, nc=nc, hp=hp),
        grid=(b, hg),
        in_specs=in_specs,
        out_specs=out_specs,
        out_shape=out_shape,
        scratch_shapes=scratch,
        compiler_params=_params(("parallel", "parallel")),
        name="mlstm",
    )(qvo, kt, qvo, g6, c0, n0r, m0r)
    return hf, hb, cf, nf[:, :, :, 0, :], mf[:, :, :, 0, 0]


def _attn_kernel(*refs, sub, key_chunk, n_new, n_cache, lam_init):
    if n_cache:
        lp_ref, q_ref, k_ref, kc_ref, vt_ref, vct_ref, o_ref, s_even_ref, s_odd_ref = refs
    else:
        lp_ref, q_ref, k_ref, vt_ref, o_ref, s_even_ref, s_odd_ref = refs
    lp = lp_ref[...]
    lam = (jnp.exp(jnp.sum(lp[0:1] * lp[1:2], axis=-1, keepdims=True))
           - jnp.exp(jnp.sum(lp[2:3] * lp[3:4], axis=-1, keepdims=True)) + lam_init)
    lane = lax.broadcasted_iota(jnp.int32, (sub, HEAD_DIM), 1)
    n_sub = n_new // sub
    nt = (((1,), (1,)), ((), ()))

    bounds = list(range(0, n_new, key_chunk)) + [n_new]
    chunks = [(False, c0, c0, c1 - c0) for c0, c1 in zip(bounds[:-1], bounds[1:])]
    if n_cache:
        chunks.append((True, 0, n_new, n_cache))

    def keys_of(chunk):
        cached, c0, _, n = chunk
        return kc_ref[0, c0:c0 + n, :] if cached else k_ref[0, c0:c0 + n, :]

    def values_of(chunk):
        cached, c0, _, n = chunk
        return vct_ref[0, :, c0:c0 + n] if cached else vt_ref[:, c0:c0 + n]

    def queries(idx):
        r0 = pl.multiple_of(idx * sub, sub)
        q = q_ref[0, pl.ds(r0, sub), :]
        zero = jnp.zeros_like(q)
        return jnp.concatenate([jnp.where(lane < QK_DIM, q, zero), jnp.where(lane >= QK_DIM, q, zero)], axis=0)

    def scores(qq, chunk, s_ref):
        s = lax.dot_general(keys_of(chunk), qq, nt, preferred_element_type=F32)
        s_ref[chunk[2]:chunk[2] + chunk[3], :] = s
        return jnp.max(s, axis=0, keepdims=True)

    def step(idx_next, s_next_ref, idx, s_ref, m):
        qq = queries(idx_next)
        m_next, l, o = None, None, None
        for chunk in chunks:
            mc = scores(qq, chunk, s_next_ref)
            m_next = mc if m_next is None else jnp.maximum(m_next, mc)
            e = jnp.exp2(s_ref[chunk[2]:chunk[2] + chunk[3], :] - m)
            lc = jnp.sum(e, axis=0, keepdims=True)
            oc = jnp.dot(values_of(chunk), e.astype(BF16), preferred_element_type=F32)
            l = lc if l is None else l + lc
            o = oc if o is None else o + oc
        o = o / l
        out_t = o[:, :sub] - lam * o[:, sub:]
        r0 = pl.multiple_of(idx * sub, sub)
        o_ref[0, pl.ds(r0, sub), :] = jnp.transpose(out_t).astype(BF16)
        return m_next

    def pair(i, m_even):
        a = 2 * i
        m_odd = step(a + 1, s_odd_ref, a, s_even_ref, m_even)
        return step(jnp.minimum(a + 2, n_sub - 1), s_even_ref, a + 1, s_odd_ref, m_odd)

    qq0 = queries(0)
    m0 = None
    for chunk in chunks:
        mc = scores(qq0, chunk, s_even_ref)
        m0 = mc if m0 is None else jnp.maximum(m0, mc)
    lax.fori_loop(0, n_sub // 2, pair, m0)


def _attn_short_kernel(lp_ref, q_ref, k_ref, vt_ref, o_ref, *, lam_init):
    lp = lp_ref[...]
    lam = (jnp.exp(jnp.sum(lp[0:1] * lp[1:2], axis=-1, keepdims=True))
           - jnp.exp(jnp.sum(lp[2:3] * lp[3:4], axis=-1, keepdims=True)) + lam_init)
    s_len = q_ref.shape[1]
    lane = lax.broadcasted_iota(jnp.int32, (s_len, HEAD_DIM), 1)
    nt = (((1,), (1,)), ((), ()))
    for hh in range(N_HEADS):
        lanes = slice(hh * HEAD_DIM, (hh + 1) * HEAD_DIM)
        q = q_ref[0, :, lanes]
        zero = jnp.zeros_like(q)
        qq = jnp.concatenate([jnp.where(lane < QK_DIM, q, zero), jnp.where(lane >= QK_DIM, q, zero)], axis=0)
        s = lax.dot_general(k_ref[0, :, lanes], qq, nt, preferred_element_type=F32)
        e = jnp.exp2(s - jnp.max(s, axis=0, keepdims=True))
        o = jnp.dot(vt_ref[lanes, :], e.astype(BF16), preferred_element_type=F32)
        o = o / jnp.sum(e, axis=0, keepdims=True)
        out_t = o[:, :s_len] - lam * o[:, s_len:]
        o_ref[0, :, lanes] = jnp.transpose(out_t).astype(BF16)


def _attn_short(lp, q, k, vt, lam_init):
    b, s, _ = q.shape
    return pl.pallas_call(
        functools.partial(_attn_short_kernel, lam_init=lam_init),
        grid=(b,),
        in_specs=[pl.BlockSpec(lp.shape, lambda i: (0, 0)),
                  pl.BlockSpec((1, s, WIDTH), lambda i: (i, 0, 0)),
                  pl.BlockSpec((1, s, WIDTH), lambda i: (i, 0, 0)),
                  pl.BlockSpec((WIDTH, s), lambda i: (0, i))],
        out_specs=pl.BlockSpec((1, s, WIDTH), lambda i: (i, 0, 0)),
        out_shape=jax.ShapeDtypeStruct((b, s, WIDTH), BF16),
        compiler_params=_params(("parallel",)),
        name="diff_attn_short",
    )(lp, q, k, vt)


def _attn(lp, q, k, vt, cache, lam_init):
    b, s, _ = q.shape
    if cache is None and s <= ATTN_SHORT:
        return _attn_short(lp, q, k, vt, lam_init)
    assert s % (2 * ATTN_SUB) == 0
    n_cache = 0 if cache is None else cache[0].shape[1]
    in_specs = [pl.BlockSpec(lp.shape, lambda i, h: (0, 0)),
                pl.BlockSpec((1, s, HEAD_DIM), lambda i, h: (i, 0, h)),
                pl.BlockSpec((1, s, HEAD_DIM), lambda i, h: (i, 0, h))]
    args = [lp, q, k]
    if n_cache:
        in_specs.append(pl.BlockSpec((1, n_cache, HEAD_DIM), lambda i, h: (i, 0, h)))
        args.append(cache[0])
    in_specs.append(pl.BlockSpec((HEAD_DIM, s), lambda i, h: (h, i)))
    args.append(vt)
    if n_cache:
        in_specs.append(pl.BlockSpec((1, HEAD_DIM, n_cache), lambda i, h: (i, h, 0)))
        args.append(cache[1])
    keys = s + n_cache
    return pl.pallas_call(
        functools.partial(_attn_kernel, sub=ATTN_SUB, key_chunk=min(ATTN_KEY_CHUNK, s), n_new=s, n_cache=n_cache,
                          lam_init=lam_init),
        grid=(b, N_HEADS),
        in_specs=in_specs,
        out_specs=pl.BlockSpec((1, s, HEAD_DIM), lambda i, h: (i, 0, h)),
        out_shape=jax.ShapeDtypeStruct((b, s, WIDTH), BF16),
        scratch_shapes=[pltpu.VMEM((keys, 2 * ATTN_SUB), F32), pltpu.VMEM((keys, 2 * ATTN_SUB), F32)],
        compiler_params=_params(("parallel", "parallel")),
        name="diff_attn",
    )(*args)


def _head_norm(x):
    outs = []
    for j in range(N_HEADS):
        xb = x[:, j * HEAD_DIM:(j + 1) * HEAD_DIM]
        outs.append(xb * lax.rsqrt(jnp.mean(xb * xb, axis=-1, keepdims=True) + EPS))
    return jnp.concatenate(outs, axis=1)


def _mix_kernel(hf_ref, hb_ref, om_ref, hd_ref, x_ref, g1_ref, nsc2_ref, sh2_ref, gm_ref, gd_ref,
                wout_ref, wr_ref, br_ref, x1_ref, h2_ref, chl_ref, grpt_ref, *, diff_scale, parts):
    n = x_ref.shape[0] // parts
    for p in range(parts):
        _mix_rows(slice(p * n, (p + 1) * n), hf_ref, hb_ref, om_ref, hd_ref, x_ref, g1_ref, nsc2_ref, sh2_ref,
                  gm_ref, gd_ref, wout_ref, wr_ref, br_ref, x1_ref, h2_ref, chl_ref, grpt_ref, diff_scale)


def _mix_rows(rows, hf_ref, hb_ref, om_ref, hd_ref, x_ref, g1_ref, nsc2_ref, sh2_ref, gm_ref, gd_ref,
              wout_ref, wr_ref, br_ref, x1_ref, h2_ref, chl_ref, grpt_ref, diff_scale):
    hm = _head_norm(hf_ref[rows, :].astype(F32) + hb_ref[rows, :].astype(F32))
    hm = hm * gm_ref[...] * _sigmoid(om_ref[rows, :].astype(F32))
    hd = _head_norm(hd_ref[rows, :].astype(F32)) * gd_ref[...] * diff_scale
    mix = jnp.dot(jnp.concatenate([hm.astype(BF16), hd.astype(BF16)], axis=1), wout_ref[...],
                  preferred_element_type=F32)
    x1 = x_ref[rows, :] + g1_ref[0] * mix
    x1_ref[rows, :] = x1
    ms = jnp.mean(x1 * x1, axis=-1, keepdims=True)
    h2 = x1 * lax.rsqrt(ms + EPS) * nsc2_ref[0] + sh2_ref[0]
    h2_hi = h2.astype(BF16)
    h2_ref[rows, :] = h2_hi

    h2_lo = (h2 - h2_hi.astype(F32)).astype(BF16)
    both = (jnp.dot(h2_hi, wr_ref[...], preferred_element_type=F32)
            + jnp.dot(h2_lo, wr_ref[...], preferred_element_type=F32))
    logits = both[:, :LANES] + both[:, LANES:] + br_ref[...]
    lane = lax.broadcasted_iota(jnp.int32, logits.shape, 1).astype(F32)
    big = float(LANES)
    is_grp = (lane >= N_EXPERTS) & (lane < N_EXPERTS + N_GROUPS)
    lg = jnp.where(is_grp, logits, -jnp.inf)
    mx = jnp.max(lg, axis=-1, keepdims=True)
    p_grp = 1.0 / jnp.sum(jnp.exp(lg - mx), axis=-1, keepdims=True)
    grp = jnp.min(jnp.where(lg == mx, lane, big), axis=-1, keepdims=True) - N_EXPERTS
    base = grp * EXPERTS_PER_GROUP
    sel = (lane >= base) & (lane < base + EXPERTS_PER_GROUP)
    le = jnp.where(sel, logits, -jnp.inf)
    ee = jnp.exp(le - jnp.max(le, axis=-1, keepdims=True))
    pe = ee / jnp.sum(ee, axis=-1, keepdims=True)
    top1 = jnp.max(pe, axis=-1, keepdims=True)
    idx1 = jnp.min(jnp.where(sel & (pe == top1), lane, big), axis=-1, keepdims=True)
    rest = sel & (lane != idx1)
    top2 = jnp.max(jnp.where(rest, pe, -1.0), axis=-1, keepdims=True)
    idx2 = jnp.min(jnp.where(rest & (pe == top2), lane, big), axis=-1, keepdims=True)
    denom = top1 + top2
    w1 = top1 / denom * p_grp
    w2 = top2 / denom * p_grp
    chl = jnp.zeros_like(logits)
    for j in range(EXPERTS_PER_GROUP):
        cj = jnp.where(idx1 == base + j, w1, 0.0) + jnp.where(idx2 == base + j, w2, 0.0)
        hi = cj.astype(BF16).astype(F32)
        chl = chl + jnp.where(lane == j, hi, 0.0) + jnp.where(lane == EXPERTS_PER_GROUP + j, cj - hi, 0.0)
    chl_ref[rows, :] = chl.astype(BF16)
    grpt_ref[:, rows] = jnp.transpose(jnp.broadcast_to(grp, logits.shape))[:SUBLANES, :]


def _mix(hf, hb, qvo, hd, x2d, g1, nsc2, sh2, row0, tiles_per_row, gm, gd, wout, wr, br, diff_scale, tm):
    t, d = x2d.shape
    row_map = lambda i: (row0 + i // tiles_per_row, 0, 0)
    tok = lambda i: (i, 0)
    full = lambda i: (0, 0)
    return pl.pallas_call(
        functools.partial(_mix_kernel, diff_scale=diff_scale, parts=2),
        grid=(t // tm,),
        in_specs=[pl.BlockSpec((tm, WIDTH), tok),
                  pl.BlockSpec((tm, WIDTH), tok),
                  pl.BlockSpec((tm, WIDTH), lambda i: (i, 2)),
                  pl.BlockSpec((tm, WIDTH), tok),
                  pl.BlockSpec((tm, d), tok),
                  pl.BlockSpec((1, 1, d), row_map),
                  pl.BlockSpec((1, 1, d), row_map),
                  pl.BlockSpec((1, 1, d), row_map),
                  pl.BlockSpec((1, WIDTH), full),
                  pl.BlockSpec((1, WIDTH), full),
                  pl.BlockSpec(wout.shape, full),
                  pl.BlockSpec(wr.shape, full),
                  pl.BlockSpec(br.shape, full)],
        out_specs=[pl.BlockSpec((tm, d), tok),
                   pl.BlockSpec((tm, d), tok),
                   pl.BlockSpec((tm, LANES), tok),
                   pl.BlockSpec((SUBLANES, tm), lambda i: (0, i))],
        out_shape=[jax.ShapeDtypeStruct((t, d), F32),
                   jax.ShapeDtypeStruct((t, d), BF16),
                   jax.ShapeDtypeStruct((t, LANES), BF16),
                   jax.ShapeDtypeStruct((SUBLANES, t), F32)],
        compiler_params=_params(("parallel",)),
        name="mix_router",
    )(hf, hb, qvo, hd, x2d, g1, nsc2, sh2, gm, gd, wout, wr, br)


def _moe_kernel(h2_ref, chl_ref, grpt_ref, before_ref, wgu_ref, wdn_ref, x1_ref, g2_ref, nrm_ref, y_ref,
                acc_ref, rank_ref, *, hidden, blk):
    g = pl.program_id(1)
    tm = h2_ref.shape[0]

    @pl.when(g == 0)
    def _():
        acc_ref[...] = jnp.zeros_like(acc_ref)
        gid = lax.broadcasted_iota(jnp.int32, (SUBLANES, tm), 0).astype(F32)
        member = grpt_ref[...] == gid
        rank = jnp.dot(jnp.where(member, 1.0, 0.0).astype(BF16), before_ref[...], preferred_element_type=F32)
        rank_ref[...] = jnp.where(member, rank, -1.0)

    rank_g = rank_ref[pl.ds(g, 1), :]
    n_g = jnp.max(rank_g) + 1.0

    sizes, start = [], 0
    while start < tm:
        size = min(blk // 4 if len(sizes) == 1 else blk, tm - start)
        sizes.append((start, size))
        start += size

    for start, size in sizes:
        @pl.when(n_g > start)
        def _():
            slot = lax.broadcasted_iota(jnp.int32, (size, tm), 0).astype(F32) + float(start)
            pick = jnp.where(rank_g == slot, 1.0, 0.0).astype(BF16)
            xg = jnp.dot(pick, h2_ref[...], preferred_element_type=F32).astype(BF16)
            cg = jnp.dot(pick, chl_ref[...], preferred_element_type=F32)
            gus = [jnp.dot(xg, wgu_ref[e], preferred_element_type=F32) for e in range(EXPERTS_PER_GROUP)]
            out = jnp.zeros((size, acc_ref.shape[1]), F32)
            for e, gu in enumerate(gus):
                a, u = gu[:, :hidden], gu[:, hidden:]
                ce = cg[:, e:e + 1] + cg[:, EXPERTS_PER_GROUP + e:EXPERTS_PER_GROUP + e + 1]
                act = (a * _sigmoid(a) * u * ce).astype(BF16)
                out = out + jnp.dot(act, wdn_ref[e], preferred_element_type=F32)
            acc_ref[...] += lax.dot_general(pick, out.astype(BF16), (((0,), (0,)), ((), ())),
                                            preferred_element_type=F32)

    @pl.when(g == pl.num_programs(1) - 1)
    def _():
        x2 = x1_ref[...] + g2_ref[0] * acc_ref[...]
        ms = jnp.mean(x2 * x2, axis=-1, keepdims=True)
        y_ref[...] = x2 * lax.rsqrt(ms + EPS) * nrm_ref[...]


def _moe(h2, chl, grpt, wgu, wdn, x1, g2, row0, tiles_per_row, nrm, tm, blk):
    t, d = x1.shape
    _, _, two_h = wgu.shape
    hidden = two_h // 2
    row_map = lambda i, g: (row0 + i // tiles_per_row, 0, 0)
    tok = lambda i, g: (i, 0)
    before = jnp.triu(jnp.ones((tm, tm), BF16), 1)
    return pl.pallas_call(
        functools.partial(_moe_kernel, hidden=hidden, blk=blk),
        grid=(t // tm, N_GROUPS),
        in_specs=[pl.BlockSpec((tm, d), tok),
                  pl.BlockSpec((tm, LANES), tok),
                  pl.BlockSpec((SUBLANES, tm), lambda i, g: (0, i)),
                  pl.BlockSpec((tm, tm), lambda i, g: (0, 0)),
                  pl.BlockSpec((EXPERTS_PER_GROUP, d, two_h), lambda i, g: (g, 0, 0)),
                  pl.BlockSpec((EXPERTS_PER_GROUP, hidden, d), lambda i, g: (g, 0, 0)),
                  pl.BlockSpec((tm, d), tok),
                  pl.BlockSpec((1, 1, d), row_map),
                  pl.BlockSpec((1, d), lambda i, g: (0, 0))],
        out_specs=pl.BlockSpec((tm, d), tok),
        out_shape=jax.ShapeDtypeStruct((t, d), F32),
        scratch_shapes=[pltpu.VMEM((tm, d), F32), pltpu.VMEM((SUBLANES, tm), F32)],
        compiler_params=_params(("parallel", "arbitrary")),
        name="moe_experts",
    )(h2, chl, grpt, before, wgu, wdn, x1, g2, nrm)


def _layer(x, mods, row0, per_request, weights, states, cache, rope_tabs, emit_cache, tm, hp):
    b, s, d = x.shape
    sh1, sc1, g1, sh2, sc2, g2 = mods
    (nrm_mix, nrm_ffn, w_in_r, w_in_t, b_gate, gm, gd, lp, wout, wr, br, wgu, wdn, nrm_final, lam_init) = weights
    x2d = x.reshape(b * s, d)
    tiles_per_row = (s // tm) if per_request else (b * s // tm)
    outs = _proj(x2d, nrm_mix, sc1, sh1, row0, tiles_per_row, w_in_r, w_in_t, b_gate, rope_tabs, emit_cache, tm)
    qvo, kmt, qd, kd, vdt, gt = outs[:6]
    c0, n0, m0 = states
    hf, hb, cf, nf, mf = _mlstm(qvo.reshape(b, s, 3 * WIDTH), kmt, gt, c0, n0, m0, hp)
    hd = _attn(lp, qd.reshape(b, s, WIDTH), kd.reshape(b, s, WIDTH), vdt, cache, lam_init)
    x1, h2, chl, grpt = _mix(hf.reshape(b * s, WIDTH), hb.reshape(b * s, WIDTH), qvo, hd.reshape(b * s, WIDTH),
                             x2d, g1, nrm_ffn[None] * (1.0 + sc2), sh2, row0, tiles_per_row, gm, gd, wout, wr, br,
                             1.0 - lam_init, tm)
    tm_moe = 2 * tm
    y = _moe(h2, chl, grpt, wgu, wdn, x1, g2, row0, tiles_per_row // 2 if per_request else b * s // tm_moe,
             nrm_final, tm_moe, MOE_BLOCK)
    new_kv = outs[6:] if emit_cache else None
    return y.reshape(b, s, d), new_kv, (cf, nf, mf)


def kernel(x_prompt, x_sample, cache_k, cache_v, state_C, state_n, state_m, c, c_ctx, w_ada, b_ada, norm_mix, norm_ffn, w_in, b_gate, mlstm_norm, diff_norm, diff_lambda, w_out, w_route_group, b_route_group, w_route_expert, b_route_expert, w_gate_up, w_down, final_norm):
    depth = w_in.shape[0]
    assert depth == 1, "the final norm is fused into the last layer's expert kernel"
    bp, sp, d = x_prompt.shape
    bs, ss, _ = x_sample.shape
    rows = 16
    assert bs < rows
    cvec = jnp.zeros((rows, d), F32).at[:bs].set(c).at[bs].set(c_ctx)
    rope_tabs = _rope_tables(ss)
    xp, xs = x_prompt, x_sample
    new_k, new_v, new_c, new_n, new_m = [], [], [], [], []
    for l in range(depth):
        lam_init = 0.8 - 0.6 * math.exp(-0.3 * l)
        mod = _ada(cvec, w_ada[l], b_ada[l])
        mods = [mod[:, i * d:(i + 1) * d].reshape(rows, 1, d) for i in range(6)]
        wl = w_in[l]
        off_dq = 4 * WIDTH + N_GATES
        w_in_r = jnp.concatenate(
            [wl[:, :WIDTH], wl[:, 2 * WIDTH:4 * WIDTH], wl[:, off_dq:]], axis=1).astype(BF16)
        w_in_t = jnp.concatenate(
            [wl[:, WIDTH:2 * WIDTH], wl[:, off_dq + 2 * WIDTH:], wl[:, 4 * WIDTH:off_dq],
             jnp.zeros((d, LANES - N_GATES), F32)], axis=1).T.astype(BF16)
        bg = b_gate[l].reshape(N_GATES, 1)
        wr = jnp.concatenate([w_route_expert[l], w_route_group[l],
                              jnp.zeros((d, LANES - N_EXPERTS - N_GROUPS), F32)], axis=1)
        wr_hi = wr.astype(BF16)
        wr = jnp.concatenate([wr_hi, (wr - wr_hi.astype(F32)).astype(BF16)], axis=1)
        br = jnp.concatenate([b_route_expert[l], b_route_group[l],
                              jnp.zeros((LANES - N_EXPERTS - N_GROUPS,), F32)]).reshape(1, LANES)
        weights = (norm_mix[l].reshape(1, d), norm_ffn[l].reshape(1, d), w_in_r, w_in_t, bg,
                   mlstm_norm[l].reshape(1, WIDTH), diff_norm[l].reshape(1, WIDTH), diff_lambda[l],
                   w_out[l].astype(BF16), wr, br, w_gate_up[l].astype(BF16), w_down[l].astype(BF16),
                   final_norm.reshape(1, d), lam_init)

        zero_states = (jnp.zeros((bp, 2, N_HEADS, HEAD_DIM, HEAD_DIM), F32),
                       jnp.zeros((bp, 2, N_HEADS, HEAD_DIM), F32),
                       jnp.zeros((bp, 2, N_HEADS), F32))
        xp, (nk, nv), (cf, nf, mf) = _layer(xp, mods, bs, False, weights, zero_states, None, None, True,
                                            tm=512, hp=4)
        new_k.append(nk.reshape(bp, sp, N_HEADS, 2, QK_DIM))
        new_v.append(nv.reshape(bp, sp, N_HEADS, HEAD_DIM))
        new_c.append(cf)
        new_n.append(nf)
        new_m.append(mf)

        states = (state_C[:, l], state_n[:, l], state_m[:, l])
        cache = (cache_k[:, l].reshape(bs, -1, WIDTH).astype(BF16),
                 cache_v[:, l].reshape(bs, -1, WIDTH).transpose(0, 2, 1).astype(BF16))
        xs, _, _ = _layer(xs, mods, 0, True, weights, states, cache, rope_tabs, False,
                          tm=512, hp=4)

    return (xp, xs, jnp.stack(new_k, axis=1), jnp.stack(new_v, axis=1),
            jnp.stack(new_c, axis=1), jnp.stack(new_n, axis=1), jnp.stack(new_m, axis=1))
```

```python
import functools
import math

import jax
import jax.numpy as jnp
from jax import lax
from jax.experimental import pallas as pl
from jax.experimental.pallas import tpu as pltpu

F32 = jnp.float32
BF16 = jnp.bfloat16
HIGHEST = lax.Precision.HIGHEST

EPS = 1e-6
GRID_W = 64
ROPE_THETA = 10000.0
N_HEADS = 4
HEAD_DIM = 128
QK_DIM = 64
WIDTH = N_HEADS * HEAD_DIM
CHUNK = 128
N_GATES = 4 * N_HEADS
N_GROUPS = 4
EXPERTS_PER_GROUP = 4
N_EXPERTS = 16
LANES = 128
SUBLANES = 8
MOE_BLOCK = 256
MOE_TILE = 1024
MIX_PART = 256
ATTN_SUB = 128
ATTN_KEY_CHUNK = 1024
ATTN_SHORT = 512
LOG2E = math.log2(math.e)
VMEM_LIMIT = 56 * 1024 * 1024


def _params(sem):
    return pltpu.CompilerParams(dimension_semantics=sem, vmem_limit_bytes=VMEM_LIMIT)


def _log_sigmoid(x):
    return jnp.minimum(x, 0.0) - jnp.log1p(jnp.exp(-jnp.abs(x)))


def _sigmoid(x):
    return 1.0 / (1.0 + jnp.exp(-x))


def _ada_kernel(c_ref, w_ref, b_ref, o_ref):
    c = c_ref[...]
    s = c * _sigmoid(c)
    o_ref[...] = jnp.dot(s, w_ref[...], preferred_element_type=F32, precision=HIGHEST) + b_ref[...]


def _ada(cvec, w_ada, b_ada):
    rows, d = cvec.shape
    n = w_ada.shape[1]
    tn = 1024
    return pl.pallas_call(
        _ada_kernel,
        grid=(n // tn,),
        in_specs=[pl.BlockSpec((rows, d), lambda j: (0, 0)),
                  pl.BlockSpec((d, tn), lambda j: (0, j)),
                  pl.BlockSpec((1, tn), lambda j: (0, j))],
        out_specs=pl.BlockSpec((rows, tn), lambda j: (0, j)),
        out_shape=jax.ShapeDtypeStruct((rows, n), F32),
        compiler_params=_params(("arbitrary",)),
        name="ada",
    )(cvec, w_ada, b_ada.reshape(1, n))


def _proj_kernel(*refs, rope, emit_cache):
    x_ref, nrm_ref, sc_ref, sh_ref, w_ref, wt_ref, bg_ref = refs[:7]
    refs = refs[7:]
    if rope:
        cos_ref, sa_ref, sb_ref = refs[:3]
        refs = refs[3:]
    m_ref, kmt_ref, q_ref, k_ref, vt_ref, gt_ref = refs[:6]
    refs = refs[6:]
    if emit_cache:
        newk_ref, newv_ref = refs

    x = x_ref[...]
    ms = jnp.mean(x * x, axis=-1, keepdims=True)
    y = x * lax.rsqrt(ms + EPS) * nrm_ref[...]
    h = (y * (1.0 + sc_ref[0]) + sh_ref[0]).astype(BF16)

    def section(j):
        return jnp.dot(h, w_ref[:, j * WIDTH:(j + 1) * WIDTH], preferred_element_type=F32)

    def rotate(p):
        cos, sa, sb = cos_ref[...], sa_ref[...], sb_ref[...]
        outs = []
        for j in range(N_HEADS):
            xb = p[:, j * LANES:(j + 1) * LANES]
            outs.append(xb * cos + pltpu.roll(xb, LANES - 16, 1) * sa + pltpu.roll(xb, 16, 1) * sb)
        return jnp.concatenate(outs, axis=1)

    def section_t(j, n):
        return lax.dot_general(wt_ref[j * WIDTH:j * WIDTH + n, :], h, (((1,), (1,)), ((), ())),
                               preferred_element_type=F32)

    for j in range(3):
        m_ref[:, j * WIDTH:(j + 1) * WIDTH] = section(j).astype(BF16)
    kmt_ref[...] = (section_t(0, WIDTH) * (HEAD_DIM ** -0.5)).astype(BF16)
    pq = section(3)
    if rope:
        pq = rotate(pq)
    q_ref[...] = (pq * (LOG2E * QK_DIM ** -0.5)).astype(BF16)
    pk = section(4)
    if emit_cache:
        newk_ref[...] = pk
    if rope:
        pk = rotate(pk)
    k_ref[...] = pk.astype(BF16)
    if emit_cache:
        newv_ref[...] = section(5)
    vt_ref[...] = section_t(1, WIDTH).astype(BF16)
    gt_ref[...] = section_t(2, LANES)[:N_GATES, :] + bg_ref[...]


def _proj(x2d, nrm, sc, sh, row0, tiles_per_row, w_r, w_t, b_gate, rope_tabs, emit_cache, tm):
    t, d = x2d.shape
    nt = t // tm
    rope = rope_tabs is not None
    row_map = lambda i: (row0 + i // tiles_per_row, 0, 0)
    tok = lambda i: (i, 0)
    feat = lambda i: (0, i)
    in_specs = [pl.BlockSpec((tm, d), tok),
                pl.BlockSpec((1, d), lambda i: (0, 0)),
                pl.BlockSpec((1, 1, d), row_map),
                pl.BlockSpec((1, 1, d), row_map),
                pl.BlockSpec(w_r.shape, lambda i: (0, 0)),
                pl.BlockSpec(w_t.shape, lambda i: (0, 0)),
                pl.BlockSpec((N_GATES, 1), lambda i: (0, 0))]
    args = [x2d, nrm, sc, sh, w_r, w_t, b_gate]
    if rope:
        tps = rope_tabs[0].shape[0] // tm
        for tab in rope_tabs:
            in_specs.append(pl.BlockSpec((tm, LANES), lambda i: (i % tps, 0)))
            args.append(tab)
    out_shape = [jax.ShapeDtypeStruct((t, 3 * WIDTH), BF16),
                 jax.ShapeDtypeStruct((WIDTH, t), BF16),
                 jax.ShapeDtypeStruct((t, WIDTH), BF16),
                 jax.ShapeDtypeStruct((t, WIDTH), BF16),
                 jax.ShapeDtypeStruct((WIDTH, t), BF16),
                 jax.ShapeDtypeStruct((N_GATES, t), F32)]
    out_specs = [pl.BlockSpec((tm, 3 * WIDTH), tok),
                 pl.BlockSpec((WIDTH, tm), feat),
                 pl.BlockSpec((tm, WIDTH), tok),
                 pl.BlockSpec((tm, WIDTH), tok),
                 pl.BlockSpec((WIDTH, tm), feat),
                 pl.BlockSpec((N_GATES, tm), feat)]
    if emit_cache:
        out_shape += [jax.ShapeDtypeStruct((t, WIDTH), F32)] * 2
        out_specs += [pl.BlockSpec((tm, WIDTH), tok)] * 2
    return pl.pallas_call(
        functools.partial(_proj_kernel, rope=rope, emit_cache=emit_cache),
        grid=(nt,),
        in_specs=in_specs,
        out_specs=out_specs,
        out_shape=out_shape,
        compiler_params=_params(("parallel",)),
        name="proj_rope" if rope else "proj_ctx",
    )(*args)


def _rope_tables(seq):
    t = jnp.arange(seq)
    row = (t // GRID_W).astype(F32)[:, None]
    col = (t % GRID_W).astype(F32)[:, None]
    lane = jnp.arange(LANES)
    freqs = ROPE_THETA ** (-(lane % 16).astype(F32) / 16.0)
    pos = jnp.where(((lane % 64) < 32)[None, :], row, col)
    ang = pos * freqs[None, :]
    cos, sin = jnp.cos(ang), jnp.sin(ang)
    first = ((lane % 32) < 16)[None, :]
    return cos, jnp.where(first, -sin, 0.0), jnp.where(first, 0.0, sin)


def _lane_scan(x, pos, op, forward):
    n = x.shape[-1]
    shift = 1
    while shift < n:
        if forward:
            y = pltpu.roll(x, shift, 1)
            ok = pos >= shift
        else:
            y = pltpu.roll(x, n - shift, 1)
            ok = pos < n - shift
        x = jnp.where(ok, op(x, y), x)
        shift *= 2
    return x


def _mlstm_kernel(q_ref, kt_ref, v_ref, g_ref, c0_ref, n0_ref, m0_ref,
                  hf_ref, hb_ref, cf_ref, nf_ref, mf_ref,
                  r_ref, mp_ref, mn_ref, dc_ref, tab_ref, *st_refs, nc, hp):
    L = CHUNK
    rows = nc * hp
    pos = lax.broadcasted_iota(jnp.int32, (rows, L), 1)
    tab_ref[...] = jnp.zeros_like(tab_ref)
    for d in range(2):
        i_pre = g_ref[0, 0, (2 * d) * rows:(2 * d + 1) * rows, :]
        lf = _log_sigmoid(g_ref[0, 0, (2 * d + 1) * rows:(2 * d + 2) * rows, :])
        pre = _lane_scan(lf, pos, jnp.add, True)
        tot = jnp.broadcast_to(jnp.sum(lf, axis=-1, keepdims=True), lf.shape)
        b = pre if d == 0 else tot - pre + lf
        r = i_pre - b
        r_max = _lane_scan(r, pos, jnp.maximum, d == 0)
        gk = tot - b + i_pre
        g_max = jnp.broadcast_to(jnp.max(gk, axis=-1, keepdims=True), gk.shape)
        m = m0_ref[0, d, :, 0, :]
        for c in (range(nc) if d == 0 else reversed(range(nc))):
            sl = slice(c * hp, (c + 1) * hp)
            mp_ref[d, sl, :] = m
            m = jnp.maximum(tot[sl] + m, g_max[sl])
            mn_ref[d, sl, :] = m
        mf_ref[0, d, :, 0, :] = m
        m_prev = mp_ref[d]
        m_new = mn_ref[d]
        m_row = jnp.maximum(m_prev, r_max)
        r_ref[d] = r
        dc_ref[d] = jnp.exp(tot + m_prev - m_new)
        w_k = jnp.exp(gk - m_new)
        floor = jnp.exp(-(b + m_row))
        for c in range(nc):
            src = slice(c * hp, (c + 1) * hp)
            slot = c if d == 0 else nc - 1 - c
            for qi, val in enumerate((m_row, floor, w_k)):
                base = (2 * qi + d) * hp
                tab_ref[slot, base:base + hp, :] = val[src]

    for d in range(2):
        for hh in range(hp):
            st_refs[d * hp + hh][:, :HEAD_DIM] = jnp.transpose(c0_ref[0, d, hh])
            st_refs[d * hp + hh][:, HEAD_DIM:] = jnp.transpose(
                jnp.broadcast_to(n0_ref[0, d, hh], (HEAD_DIM, HEAD_DIM)))

    rr = lax.broadcasted_iota(jnp.int32, (L, L), 0)
    cc = lax.broadcasted_iota(jnp.int32, (L, L), 1)
    masks = (cc <= rr, cc >= rr)
    ones = jnp.ones((L, HEAD_DIM), BF16)

    def body(c, carry):
        tab = jnp.transpose(tab_ref[c])
        units = [(d, hh) for d in range(2) for hh in range(hp)]

        def operands(d, hh):
            ci = c if d == 0 else nc - 1 - c
            off = pl.multiple_of(ci * L, L)
            lanes = slice(hh * HEAD_DIM, (hh + 1) * HEAD_DIM)
            return ci * hp + hh, off, lanes

        def column(d, hh, which):
            j = (2 * which + d) * hp + hh
            return jnp.broadcast_to(tab[:, j:j + 1], (L, HEAD_DIM))

        qk, inter, upd = {}, {}, {}
        for d, hh in units:
            row, off, lanes = operands(d, hh)
            q = q_ref[0, pl.ds(off, L), lanes]
            kt = kt_ref[lanes, pl.ds(off, L)]
            qk[d, hh] = jnp.dot(q, kt, preferred_element_type=F32)
            inter[d, hh] = jnp.dot(q, st_refs[d * hp + hh][...].astype(BF16), preferred_element_type=F32)
        for d, hh in units:
            row, off, lanes = operands(d, hh)
            v = v_ref[0, pl.ds(off, L), lanes]
            kt = kt_ref[lanes, pl.ds(off, L)]
            wk_col = column(d, hh, 2)
            vw = (v.astype(F32) * wk_col).astype(BF16)
            upd[d, hh] = jnp.dot(kt, jnp.concatenate([vw, wk_col.astype(BF16)], axis=1),
                                 preferred_element_type=F32)
        for d, hh in units:
            row, off, lanes = operands(d, hh)
            m_col = column(d, hh, 0)
            r_row = r_ref[d, pl.ds(row, 1), :]
            m_prev = mp_ref[d, pl.ds(row, 1), :]
            v = v_ref[0, pl.ds(off, L), lanes]
            w_intra = jnp.exp(jnp.where(masks[d], r_row - m_col, -jnp.inf))
            s = (qk[d, hh] * w_intra).astype(BF16)
            intra = jnp.dot(s, jnp.concatenate([v, ones], axis=1), preferred_element_type=F32)
            w_state = jnp.exp(m_prev - m_col)
            both = intra + jnp.concatenate([w_state, w_state], axis=1) * inter[d, hh]
            den = jnp.maximum(jnp.abs(both[:, HEAD_DIM:]), column(d, hh, 1))
            out_ref = hf_ref if d == 0 else hb_ref
            out_ref[0, pl.ds(off, L), lanes] = (both[:, :HEAD_DIM] / den).astype(BF16)
        for d, hh in units:
            row, off, lanes = operands(d, hh)
            decay = dc_ref[d, pl.ds(row, 1), :]
            st = st_refs[d * hp + hh]
            st[...] = jnp.concatenate([decay, decay], axis=1) * st[...] + upd[d, hh]
        return carry

    lax.fori_loop(0, nc, body, 0, unroll=2)

    for d in range(2):
        for hh in range(hp):
            state = st_refs[d * hp + hh][...]
            cf_ref[0, d, hh] = jnp.transpose(state[:, :HEAD_DIM])
            nf_ref[0, d, hh] = jnp.transpose(state[:, HEAD_DIM:])[0:1, :]


def _mlstm(qvo, kt, gt, c0, n0, m0, hp):
    b, s, _ = qvo.shape
    nc = s // CHUNK
    hg = N_HEADS // hp
    rows = nc * hp
    g6 = gt.reshape(4, hg, hp, b, nc, CHUNK).transpose(3, 1, 0, 4, 2, 5).reshape(b, hg, 4 * rows, CHUNK)
    n0r = n0.reshape(b, 2, N_HEADS, 1, HEAD_DIM)
    m0r = jnp.broadcast_to(m0[..., None, None], (b, 2, N_HEADS, 1, HEAD_DIM))
    blk = hp * HEAD_DIM
    nblk = WIDTH // blk
    state_spec = lambda w: pl.BlockSpec((1, 2, hp, w, HEAD_DIM), lambda i, j: (i, 0, j, 0, 0))
    in_specs = [pl.BlockSpec((1, s, blk), lambda i, j: (i, 0, j)),
                pl.BlockSpec((blk, s), lambda i, j: (j, i)),
                pl.BlockSpec((1, s, blk), lambda i, j: (i, 0, nblk + j)),
                pl.BlockSpec((1, 1, 4 * rows, CHUNK), lambda i, j: (i, j, 0, 0)),
                state_spec(HEAD_DIM), state_spec(1), state_spec(1)]
    out_specs = [pl.BlockSpec((1, s, blk), lambda i, j: (i, 0, j)),
                 pl.BlockSpec((1, s, blk), lambda i, j: (i, 0, j)),
                 state_spec(HEAD_DIM), state_spec(1), state_spec(1)]
    out_shape = [jax.ShapeDtypeStruct((b, s, WIDTH), BF16),
                 jax.ShapeDtypeStruct((b, s, WIDTH), BF16),
                 jax.ShapeDtypeStruct((b, 2, N_HEADS, HEAD_DIM, HEAD_DIM), F32),
                 jax.ShapeDtypeStruct((b, 2, N_HEADS, 1, HEAD_DIM), F32),
                 jax.ShapeDtypeStruct((b, 2, N_HEADS, 1, HEAD_DIM), F32)]
    scratch = [pltpu.VMEM((2, rows, CHUNK), F32),
               pltpu.VMEM((2, rows, CHUNK), F32),
               pltpu.VMEM((2, rows, CHUNK), F32),
               pltpu.VMEM((2, rows, CHUNK), F32),
               pltpu.VMEM((nc, LANES, CHUNK), F32)]
    scratch += [pltpu.VMEM((HEAD_DIM, 2 * HEAD_DIM), F32) for _ in range(2 * hp)]
    hf, hb, cf, nf, mf = pl.pallas_call(
        functools.partial(_mlstm_kernel, nc=nc, hp=hp),
        grid=(b, hg),
        in_specs=in_specs,
        out_specs=out_specs,
        out_shape=out_shape,
        scratch_shapes=scratch,
        compiler_params=_params(("parallel", "parallel")),
        name="mlstm",
    )(qvo, kt, qvo, g6, c0, n0r, m0r)
    return hf, hb, cf, nf[:, :, :, 0, :], mf[:, :, :, 0, 0]


def _attn_kernel(*refs, sub, key_chunk, n_new, n_cache, lam_init):
    if n_cache:
        lp_ref, q_ref, k_ref, kc_ref, vt_ref, vct_ref, o_ref, s_even_ref, s_odd_ref = refs
    else:
        lp_ref, q_ref, k_ref, vt_ref, o_ref, s_even_ref, s_odd_ref = refs
    lp = lp_ref[...]
    lam = (jnp.exp(jnp.sum(lp[0:1] * lp[1:2], axis=-1, keepdims=True))
           - jnp.exp(jnp.sum(lp[2:3] * lp[3:4], axis=-1, keepdims=True)) + lam_init)
    lane = lax.broadcasted_iota(jnp.int32, (sub, HEAD_DIM), 1)
    n_sub = n_new // sub
    nt = (((1,), (1,)), ((), ()))

    bounds = list(range(0, n_new, key_chunk)) + [n_new]
    chunks = [(False, c0, c0, c1 - c0) for c0, c1 in zip(bounds[:-1], bounds[1:])]
    if n_cache:
        chunks.append((True, 0, n_new, n_cache))

    def keys_of(chunk):
        cached, c0, _, n = chunk
        return kc_ref[0, c0:c0 + n, :] if cached else k_ref[0, c0:c0 + n, :]

    def values_of(chunk):
        cached, c0, _, n = chunk
        return vct_ref[0, :, c0:c0 + n] if cached else vt_ref[:, c0:c0 + n]

    def queries(idx):
        r0 = pl.multiple_of(idx * sub, sub)
        q = q_ref[0, pl.ds(r0, sub), :]
        zero = jnp.zeros_like(q)
        return jnp.concatenate([jnp.where(lane < QK_DIM, q, zero), jnp.where(lane >= QK_DIM, q, zero)], axis=0)

    def scores(qq, chunk, s_ref):
        s = lax.dot_general(keys_of(chunk), qq, nt, preferred_element_type=F32)
        s_ref[chunk[2]:chunk[2] + chunk[3], :] = s
        return jnp.max(s, axis=0, keepdims=True)

    def step(idx_next, s_next_ref, idx, s_ref, m):
        qq = queries(idx_next)
        m_next, l, o = None, None, None
        for chunk in chunks:
            mc = scores(qq, chunk, s_next_ref)
            m_next = mc if m_next is None else jnp.maximum(m_next, mc)
            e = jnp.exp2(s_ref[chunk[2]:chunk[2] + chunk[3], :] - m)
            lc = jnp.sum(e, axis=0, keepdims=True)
            oc = jnp.dot(values_of(chunk), e.astype(BF16), preferred_element_type=F32)
            l = lc if l is None else l + lc
            o = oc if o is None else o + oc
        o = o / l
        out_t = o[:, :sub] - lam * o[:, sub:]
        r0 = pl.multiple_of(idx * sub, sub)
        o_ref[0, pl.ds(r0, sub), :] = jnp.transpose(out_t).astype(BF16)
        return m_next

    def pair(i, m_even):
        a = 2 * i
        m_odd = step(a + 1, s_odd_ref, a, s_even_ref, m_even)
        return step(jnp.minimum(a + 2, n_sub - 1), s_even_ref, a + 1, s_odd_ref, m_odd)

    qq0 = queries(0)
    m0 = None
    for chunk in chunks:
        mc = scores(qq0, chunk, s_even_ref)
        m0 = mc if m0 is None else jnp.maximum(m0, mc)
    lax.fori_loop(0, n_sub // 2, pair, m0)


def _attn_short_kernel(lp_ref, q_ref, k_ref, vt_ref, o_ref, *, lam_init):
    lp = lp_ref[...]
    lam = (jnp.exp(jnp.sum(lp[0:1] * lp[1:2], axis=-1, keepdims=True))
           - jnp.exp(jnp.sum(lp[2:3] * lp[3:4], axis=-1, keepdims=True)) + lam_init)
    s_len = q_ref.shape[1]
    lane = lax.broadcasted_iota(jnp.int32, (s_len, HEAD_DIM), 1)
    nt = (((1,), (1,)), ((), ()))
    for hh in range(N_HEADS):
        lanes = slice(hh * HEAD_DIM, (hh + 1) * HEAD_DIM)
        q = q_ref[0, :, lanes]
        zero = jnp.zeros_like(q)
        qq = jnp.concatenate([jnp.where(lane < QK_DIM, q, zero), jnp.where(lane >= QK_DIM, q, zero)], axis=0)
        s = lax.dot_general(k_ref[0, :, lanes], qq, nt, preferred_element_type=F32)
        e = jnp.exp2(s - jnp.max(s, axis=0, keepdims=True))
        o = jnp.dot(vt_ref[lanes, :], e.astype(BF16), preferred_element_type=F32)
        o = o / jnp.sum(e, axis=0, keepdims=True)
        out_t = o[:, :s_len] - lam * o[:, s_len:]
        o_ref[0, :, lanes] = jnp.transpose(out_t).astype(BF16)


def _attn_short(lp, q, k, vt, lam_init):
    b, s, _ = q.shape
    return pl.pallas_call(
        functools.partial(_attn_short_kernel, lam_init=lam_init),
        grid=(b,),
        in_specs=[pl.BlockSpec(lp.shape, lambda i: (0, 0)),
                  pl.BlockSpec((1, s, WIDTH), lambda i: (i, 0, 0)),
                  pl.BlockSpec((1, s, WIDTH), lambda i: (i, 0, 0)),
                  pl.BlockSpec((WIDTH, s), lambda i: (0, i))],
        out_specs=pl.BlockSpec((1, s, WIDTH), lambda i: (i, 0, 0)),
        out_shape=jax.ShapeDtypeStruct((b, s, WIDTH), BF16),
        compiler_params=_params(("parallel",)),
        name="diff_attn_short",
    )(lp, q, k, vt)


def _attn(lp, q, k, vt, cache, lam_init):
    b, s, _ = q.shape
    if cache is None and s <= ATTN_SHORT:
        return _attn_short(lp, q, k, vt, lam_init)
    assert s % (2 * ATTN_SUB) == 0
    n_cache = 0 if cache is None else cache[0].shape[1]
    in_specs = [pl.BlockSpec(lp.shape, lambda i, h: (0, 0)),
                pl.BlockSpec((1, s, HEAD_DIM), lambda i, h: (i, 0, h)),
                pl.BlockSpec((1, s, HEAD_DIM), lambda i, h: (i, 0, h))]
    args = [lp, q, k]
    if n_cache:
        in_specs.append(pl.BlockSpec((1, n_cache, HEAD_DIM), lambda i, h: (i, 0, h)))
        args.append(cache[0])
    in_specs.append(pl.BlockSpec((HEAD_DIM, s), lambda i, h: (h, i)))
    args.append(vt)
    if n_cache:
        in_specs.append(pl.BlockSpec((1, HEAD_DIM, n_cache), lambda i, h: (i, h, 0)))
        args.append(cache[1])
    keys = s + n_cache
    return pl.pallas_call(
        functools.partial(_attn_kernel, sub=ATTN_SUB, key_chunk=min(ATTN_KEY_CHUNK, s), n_new=s, n_cache=n_cache,
                          lam_init=lam_init),
        grid=(b, N_HEADS),
        in_specs=in_specs,
        out_specs=pl.BlockSpec((1, s, HEAD_DIM), lambda i, h: (i, 0, h)),
        out_shape=jax.ShapeDtypeStruct((b, s, WIDTH), BF16),
        scratch_shapes=[pltpu.VMEM((keys, 2 * ATTN_SUB), F32), pltpu.VMEM((keys, 2 * ATTN_SUB), F32)],
        compiler_params=_params(("parallel", "parallel")),
        name="diff_attn",
    )(*args)


def _head_norm(x):
    outs = []
    for j in range(N_HEADS):
        xb = x[:, j * HEAD_DIM:(j + 1) * HEAD_DIM]
        outs.append(xb * lax.rsqrt(jnp.mean(xb * xb, axis=-1, keepdims=True) + EPS))
    return jnp.concatenate(outs, axis=1)


def _mix_kernel(hf_ref, hb_ref, om_ref, hd_ref, x_ref, g1_ref, nsc2_ref, sh2_ref, gm_ref, gd_ref,
                wout_ref, wr_ref, br_ref, x1_ref, h2_ref, chl_ref, grpt_ref, *, diff_scale, parts):
    n = x_ref.shape[0] // parts
    for p in range(parts):
        _mix_rows(slice(p * n, (p + 1) * n), hf_ref, hb_ref, om_ref, hd_ref, x_ref, g1_ref, nsc2_ref, sh2_ref,
                  gm_ref, gd_ref, wout_ref, wr_ref, br_ref, x1_ref, h2_ref, chl_ref, grpt_ref, diff_scale)


def _mix_rows(rows, hf_ref, hb_ref, om_ref, hd_ref, x_ref, g1_ref, nsc2_ref, sh2_ref, gm_ref, gd_ref,
              wout_ref, wr_ref, br_ref, x1_ref, h2_ref, chl_ref, grpt_ref, diff_scale):
    hm = _head_norm(hf_ref[rows, :].astype(F32) + hb_ref[rows, :].astype(F32))
    hm = hm * gm_ref[...] * _sigmoid(om_ref[rows, :].astype(F32))
    hd = _head_norm(hd_ref[rows, :].astype(F32)) * gd_ref[...] * diff_scale
    mix = jnp.dot(jnp.concatenate([hm.astype(BF16), hd.astype(BF16)], axis=1), wout_ref[...],
                  preferred_element_type=F32)
    x1 = x_ref[rows, :] + g1_ref[0] * mix
    x1_ref[rows, :] = x1
    ms = jnp.mean(x1 * x1, axis=-1, keepdims=True)
    h2 = x1 * lax.rsqrt(ms + EPS) * nsc2_ref[0] + sh2_ref[0]
    h2_hi = h2.astype(BF16)
    h2_ref[rows, :] = h2_hi

    h2_lo = (h2 - h2_hi.astype(F32)).astype(BF16)
    both = (jnp.dot(h2_hi, wr_ref[...], preferred_element_type=F32)
            + jnp.dot(h2_lo, wr_ref[...], preferred_element_type=F32))
    logits = both[:, :LANES] + both[:, LANES:] + br_ref[...]
    lane = lax.broadcasted_iota(jnp.int32, logits.shape, 1).astype(F32)
    big = float(LANES)
    is_grp = (lane >= N_EXPERTS) & (lane < N_EXPERTS + N_GROUPS)
    lg = jnp.where(is_grp, logits, -jnp.inf)
    mx = jnp.max(lg, axis=-1, keepdims=True)
    p_grp = 1.0 / jnp.sum(jnp.exp(lg - mx), axis=-1, keepdims=True)
    grp = jnp.min(jnp.where(lg == mx, lane, big), axis=-1, keepdims=True) - N_EXPERTS
    base = grp * EXPERTS_PER_GROUP
    sel = (lane >= base) & (lane < base + EXPERTS_PER_GROUP)
    le = jnp.where(sel, logits, -jnp.inf)
    ee = jnp.exp(le - jnp.max(le, axis=-1, keepdims=True))
    pe = ee / jnp.sum(ee, axis=-1, keepdims=True)
    top1 = jnp.max(pe, axis=-1, keepdims=True)
    idx1 = jnp.min(jnp.where(sel & (pe == top1), lane, big), axis=-1, keepdims=True)
    rest = sel & (lane != idx1)
    top2 = jnp.max(jnp.where(rest, pe, -1.0), axis=-1, keepdims=True)
    idx2 = jnp.min(jnp.where(rest & (pe == top2), lane, big), axis=-1, keepdims=True)
    denom = top1 + top2
    w1 = top1 / denom * p_grp
    w2 = top2 / denom * p_grp
    chl = jnp.zeros_like(logits)
    for j in range(EXPERTS_PER_GROUP):
        cj = jnp.where(idx1 == base + j, w1, 0.0) + jnp.where(idx2 == base + j, w2, 0.0)
        hi = cj.astype(BF16).astype(F32)
        chl = chl + jnp.where(lane == j, hi, 0.0) + jnp.where(lane == EXPERTS_PER_GROUP + j, cj - hi, 0.0)
    chl_ref[rows, :] = chl.astype(BF16)
    grpt_ref[:, rows] = jnp.transpose(jnp.broadcast_to(grp, logits.shape))[:SUBLANES, :]


def _mix(hf, hb, qvo, hd, x2d, g1, nsc2, sh2, row0, tiles_per_row, gm, gd, wout, wr, br, diff_scale, tm):
    t, d = x2d.shape
    row_map = lambda i: (row0 + i // tiles_per_row, 0, 0)
    tok = lambda i: (i, 0)
    full = lambda i: (0, 0)
    return pl.pallas_call(
        functools.partial(_mix_kernel, diff_scale=diff_scale, parts=tm // MIX_PART),
        grid=(t // tm,),
        in_specs=[pl.BlockSpec((tm, WIDTH), tok),
                  pl.BlockSpec((tm, WIDTH), tok),
                  pl.BlockSpec((tm, WIDTH), lambda i: (i, 2)),
                  pl.BlockSpec((tm, WIDTH), tok),
                  pl.BlockSpec((tm, d), tok),
                  pl.BlockSpec((1, 1, d), row_map),
                  pl.BlockSpec((1, 1, d), row_map),
                  pl.BlockSpec((1, 1, d), row_map),
                  pl.BlockSpec((1, WIDTH), full),
                  pl.BlockSpec((1, WIDTH), full),
                  pl.BlockSpec(wout.shape, full),
                  pl.BlockSpec(wr.shape, full),
                  pl.BlockSpec(br.shape, full)],
        out_specs=[pl.BlockSpec((tm, d), tok),
                   pl.BlockSpec((tm, d), tok),
                   pl.BlockSpec((tm, LANES), tok),
                   pl.BlockSpec((SUBLANES, tm), lambda i: (0, i))],
        out_shape=[jax.ShapeDtypeStruct((t, d), F32),
                   jax.ShapeDtypeStruct((t, d), BF16),
                   jax.ShapeDtypeStruct((t, LANES), BF16),
                   jax.ShapeDtypeStruct((SUBLANES, t), F32)],
        compiler_params=_params(("parallel",)),
        name="mix_router",
    )(hf, hb, qvo, hd, x2d, g1, nsc2, sh2, gm, gd, wout, wr, br)


def _moe_kernel(h2_ref, chl_ref, grpt_ref, before_ref, wgu_ref, wdn_ref, x1_ref, g2_ref, nrm_ref, y_ref,
                acc_ref, rank_ref, *, hidden, blk):
    g = pl.program_id(1)
    tm = h2_ref.shape[0]

    @pl.when(g == 0)
    def _():
        acc_ref[...] = jnp.zeros_like(acc_ref)
        gid = lax.broadcasted_iota(jnp.int32, (SUBLANES, tm), 0).astype(F32)
        member = grpt_ref[...] == gid
        rank = jnp.dot(jnp.where(member, 1.0, 0.0).astype(BF16), before_ref[...], preferred_element_type=F32)
        rank_ref[...] = jnp.where(member, rank, -1.0)

    rank_g = rank_ref[pl.ds(g, 1), :]
    n_g = jnp.max(rank_g) + 1.0

    sizes, start = [], 0
    while start < tm:
        size = min(blk // 4 if len(sizes) == 1 else blk, tm - start)
        sizes.append((start, size))
        start += size

    for start, size in sizes:
        @pl.when(n_g > start)
        def _():
            slot = lax.broadcasted_iota(jnp.int32, (size, tm), 0).astype(F32) + float(start)
            pick = jnp.where(rank_g == slot, 1.0, 0.0).astype(BF16)
            xg = jnp.dot(pick, h2_ref[...], preferred_element_type=F32).astype(BF16)
            cg = jnp.dot(pick, chl_ref[...], preferred_element_type=F32)
            gus = [jnp.dot(xg, wgu_ref[e], preferred_element_type=F32) for e in range(EXPERTS_PER_GROUP)]
            out = jnp.zeros((size, acc_ref.shape[1]), F32)
            for e, gu in enumerate(gus):
                a, u = gu[:, :hidden], gu[:, hidden:]
                ce = cg[:, e:e + 1] + cg[:, EXPERTS_PER_GROUP + e:EXPERTS_PER_GROUP + e + 1]
                act = (a * _sigmoid(a) * u * ce).astype(BF16)
                out = out + jnp.dot(act, wdn_ref[e], preferred_element_type=F32)
            acc_ref[...] += lax.dot_general(pick, out.astype(BF16), (((0,), (0,)), ((), ())),
                                            preferred_element_type=F32)

    @pl.when(g == pl.num_programs(1) - 1)
    def _():
        x2 = x1_ref[...] + g2_ref[0] * acc_ref[...]
        ms = jnp.mean(x2 * x2, axis=-1, keepdims=True)
        y_ref[...] = x2 * lax.rsqrt(ms + EPS) * nrm_ref[...]


def _moe(h2, chl, grpt, wgu, wdn, x1, g2, row0, tiles_per_row, nrm, tm, blk):
    t, d = x1.shape
    _, _, two_h = wgu.shape
    hidden = two_h // 2
    row_map = lambda i, g: (row0 + i // tiles_per_row, 0, 0)
    tok = lambda i, g: (i, 0)
    before = jnp.triu(jnp.ones((tm, tm), BF16), 1)
    return pl.pallas_call(
        functools.partial(_moe_kernel, hidden=hidden, blk=blk),
        grid=(t // tm, N_GROUPS),
        in_specs=[pl.BlockSpec((tm, d), tok),
                  pl.BlockSpec((tm, LANES), tok),
                  pl.BlockSpec((SUBLANES, tm), lambda i, g: (0, i)),
                  pl.BlockSpec((tm, tm), lambda i, g: (0, 0)),
                  pl.BlockSpec((EXPERTS_PER_GROUP, d, two_h), lambda i, g: (g, 0, 0)),
                  pl.BlockSpec((EXPERTS_PER_GROUP, hidden, d), lambda i, g: (g, 0, 0)),
                  pl.BlockSpec((tm, d), tok),
                  pl.BlockSpec((1, 1, d), row_map),
                  pl.BlockSpec((1, d), lambda i, g: (0, 0))],
        out_specs=pl.BlockSpec((tm, d), tok),
        out_shape=jax.ShapeDtypeStruct((t, d), F32),
        scratch_shapes=[pltpu.VMEM((tm, d), F32), pltpu.VMEM((SUBLANES, tm), F32)],
        compiler_params=_params(("parallel", "arbitrary")),
        name="moe_experts",
    )(h2, chl, grpt, before, wgu, wdn, x1, g2, nrm)


def _layer(x, mods, row0, per_request, weights, states, cache, rope_tabs, emit_cache, tm, hp):
    b, s, d = x.shape
    sh1, sc1, g1, sh2, sc2, g2 = mods
    (nrm_mix, nrm_ffn, w_in_r, w_in_t, b_gate, gm, gd, lp, wout, wr, br, wgu, wdn, nrm_final, lam_init) = weights
    x2d = x.reshape(b * s, d)
    tiles_per_row = (s // tm) if per_request else (b * s // tm)
    outs = _proj(x2d, nrm_mix, sc1, sh1, row0, tiles_per_row, w_in_r, w_in_t, b_gate, rope_tabs, emit_cache, tm)
    qvo, kmt, qd, kd, vdt, gt = outs[:6]
    c0, n0, m0 = states
    hf, hb, cf, nf, mf = _mlstm(qvo.reshape(b, s, 3 * WIDTH), kmt, gt, c0, n0, m0, hp)
    hd = _attn(lp, qd.reshape(b, s, WIDTH), kd.reshape(b, s, WIDTH), vdt, cache, lam_init)
    x1, h2, chl, grpt = _mix(hf.reshape(b * s, WIDTH), hb.reshape(b * s, WIDTH), qvo, hd.reshape(b * s, WIDTH),
                             x2d, g1, nrm_ffn[None] * (1.0 + sc2), sh2, row0, tiles_per_row, gm, gd, wout, wr, br,
                             1.0 - lam_init, tm)
    y = _moe(h2, chl, grpt, wgu, wdn, x1, g2, row0, (s if per_request else b * s) // MOE_TILE,
             nrm_final, MOE_TILE, MOE_BLOCK)
    new_kv = outs[6:] if emit_cache else None
    return y.reshape(b, s, d), new_kv, (cf, nf, mf)


def kernel(x_prompt, x_sample, cache_k, cache_v, state_C, state_n, state_m, c, c_ctx, w_ada, b_ada, norm_mix, norm_ffn, w_in, b_gate, mlstm_norm, diff_norm, diff_lambda, w_out, w_route_group, b_route_group, w_route_expert, b_route_expert, w_gate_up, w_down, final_norm):
    depth = w_in.shape[0]
    assert depth == 1, "the final norm is fused into the last layer's expert kernel"
    bp, sp, d = x_prompt.shape
    bs, ss, _ = x_sample.shape
    rows = 16
    assert bs < rows
    cvec = jnp.zeros((rows, d), F32).at[:bs].set(c).at[bs].set(c_ctx)
    rope_tabs = _rope_tables(ss)
    xp, xs = x_prompt, x_sample
    new_k, new_v, new_c, new_n, new_m = [], [], [], [], []
    for l in range(depth):
        lam_init = 0.8 - 0.6 * math.exp(-0.3 * l)
        mod = _ada(cvec, w_ada[l], b_ada[l])
        mods = [mod[:, i * d:(i + 1) * d].reshape(rows, 1, d) for i in range(6)]
        wl = w_in[l]
        off_dq = 4 * WIDTH + N_GATES
        w_in_r = jnp.concatenate(
            [wl[:, :WIDTH], wl[:, 2 * WIDTH:4 * WIDTH], wl[:, off_dq:]], axis=1).astype(BF16)
        w_in_t = jnp.concatenate(
            [wl[:, WIDTH:2 * WIDTH], wl[:, off_dq + 2 * WIDTH:], wl[:, 4 * WIDTH:off_dq],
             jnp.zeros((d, LANES - N_GATES), F32)], axis=1).T.astype(BF16)
        bg = b_gate[l].reshape(N_GATES, 1)
        wr = jnp.concatenate([w_route_expert[l], w_route_group[l],
                              jnp.zeros((d, LANES - N_EXPERTS - N_GROUPS), F32)], axis=1)
        wr_hi = wr.astype(BF16)
        wr = jnp.concatenate([wr_hi, (wr - wr_hi.astype(F32)).astype(BF16)], axis=1)
        br = jnp.concatenate([b_route_expert[l], b_route_group[l],
                              jnp.zeros((LANES - N_EXPERTS - N_GROUPS,), F32)]).reshape(1, LANES)
        weights = (norm_mix[l].reshape(1, d), norm_ffn[l].reshape(1, d), w_in_r, w_in_t, bg,
                   mlstm_norm[l].reshape(1, WIDTH), diff_norm[l].reshape(1, WIDTH), diff_lambda[l],
                   w_out[l].astype(BF16), wr, br, w_gate_up[l].astype(BF16), w_down[l].astype(BF16),
                   final_norm.reshape(1, d), lam_init)

        zero_states = (jnp.zeros((bp, 2, N_HEADS, HEAD_DIM, HEAD_DIM), F32),
                       jnp.zeros((bp, 2, N_HEADS, HEAD_DIM), F32),
                       jnp.zeros((bp, 2, N_HEADS), F32))
        xp, (nk, nv), (cf, nf, mf) = _layer(xp, mods, bs, False, weights, zero_states, None, None, True,
                                            tm=512, hp=4)
        new_k.append(nk.reshape(bp, sp, N_HEADS, 2, QK_DIM))
        new_v.append(nv.reshape(bp, sp, N_HEADS, HEAD_DIM))
        new_c.append(cf)
        new_n.append(nf)
        new_m.append(mf)

        states = (state_C[:, l], state_n[:, l], state_m[:, l])
        cache = (cache_k[:, l].reshape(bs, -1, WIDTH).astype(BF16),
                 cache_v[:, l].reshape(bs, -1, WIDTH).transpose(0, 2, 1).astype(BF16))
        xs, _, _ = _layer(xs, mods, 0, True, weights, states, cache, rope_tabs, False,
                          tm=1024, hp=4)

    return (xp, xs, jnp.stack(new_k, axis=1), jnp.stack(new_v, axis=1),
            jnp.stack(new_c, axis=1), jnp.stack(new_n, axis=1), jnp.stack(new_m, axis=1))
```

```python
import functools
import math

import jax
import jax.numpy as jnp
from jax import lax
from jax.experimental import pallas as pl
from jax.experimental.pallas import tpu as pltpu

F32 = jnp.float32
BF16 = jnp.bfloat16
HIGHEST = lax.Precision.HIGHEST

EPS = 1e-6
GRID_W = 64
ROPE_THETA = 10000.0
N_HEADS = 4
HEAD_DIM = 128
QK_DIM = 64
WIDTH = N_HEADS * HEAD_DIM
CHUNK = 128
N_GATES = 4 * N_HEADS
N_GROUPS = 4
EXPERTS_PER_GROUP = 4
N_EXPERTS = 16
LANES = 128
SUBLANES = 8
MOE_BLOCK = 256
MOE_TILE = 1024
MIX_PART = 256
ATTN_SUB = 128
ATTN_KEY_CHUNK = 1024
ATTN_SHORT = 512
LOG2E = math.log2(math.e)
VMEM_LIMIT = 56 * 1024 * 1024


def _params(sem):
    return pltpu.CompilerParams(dimension_semantics=sem, vmem_limit_bytes=VMEM_LIMIT)


def _log_sigmoid(x):
    return jnp.minimum(x, 0.0) - jnp.log1p(jnp.exp(-jnp.abs(x)))


def _sigmoid(x):
    return 1.0 / (1.0 + jnp.exp(-x))


def _ada_kernel(c_ref, w_ref, b_ref, o_ref):
    c = c_ref[...]
    s = c * _sigmoid(c)
    o_ref[...] = jnp.dot(s, w_ref[...], preferred_element_type=F32, precision=HIGHEST) + b_ref[...]


def _ada(cvec, w_ada, b_ada):
    rows, d = cvec.shape
    n = w_ada.shape[1]
    tn = 1024
    return pl.pallas_call(
        _ada_kernel,
        grid=(n // tn,),
        in_specs=[pl.BlockSpec((rows, d), lambda j: (0, 0)),
                  pl.BlockSpec((d, tn), lambda j: (0, j)),
                  pl.BlockSpec((1, tn), lambda j: (0, j))],
        out_specs=pl.BlockSpec((rows, tn), lambda j: (0, j)),
        out_shape=jax.ShapeDtypeStruct((rows, n), F32),
        compiler_params=_params(("arbitrary",)),
        name="ada",
    )(cvec, w_ada, b_ada.reshape(1, n))


def _proj_kernel(*refs, rope, emit_cache):
    x_ref, nrm_ref, sc_ref, sh_ref, w_ref, wt_ref, bg_ref = refs[:7]
    refs = refs[7:]
    if rope:
        cos_ref, sa_ref, sb_ref = refs[:3]
        refs = refs[3:]
    m_ref, kmt_ref, q_ref, k_ref, vt_ref, gt_ref = refs[:6]
    refs = refs[6:]
    if emit_cache:
        newk_ref, newv_ref = refs

    x = x_ref[...]
    ms = jnp.mean(x * x, axis=-1, keepdims=True)
    y = x * lax.rsqrt(ms + EPS) * nrm_ref[...]
    h = (y * (1.0 + sc_ref[0]) + sh_ref[0]).astype(BF16)

    def section(j):
        return jnp.dot(h, w_ref[:, j * WIDTH:(j + 1) * WIDTH], preferred_element_type=F32)

    def rotate(p):
        cos, sa, sb = cos_ref[...], sa_ref[...], sb_ref[...]
        outs = []
        for j in range(N_HEADS):
            xb = p[:, j * LANES:(j + 1) * LANES]
            outs.append(xb * cos + pltpu.roll(xb, LANES - 16, 1) * sa + pltpu.roll(xb, 16, 1) * sb)
        return jnp.concatenate(outs, axis=1)

    def section_t(j, n):
        return lax.dot_general(wt_ref[j * WIDTH:j * WIDTH + n, :], h, (((1,), (1,)), ((), ())),
                               preferred_element_type=F32)

    for j in range(3):
        m_ref[:, j * WIDTH:(j + 1) * WIDTH] = section(j).astype(BF16)
    kmt_ref[...] = (section_t(0, WIDTH) * (HEAD_DIM ** -0.5)).astype(BF16)
    pq = section(3)
    if rope:
        pq = rotate(pq)
    q_ref[...] = (pq * (LOG2E * QK_DIM ** -0.5)).astype(BF16)
    pk = section(4)
    if emit_cache:
        newk_ref[...] = pk
    if rope:
        pk = rotate(pk)
    k_ref[...] = pk.astype(BF16)
    if emit_cache:
        newv_ref[...] = section(5)
    vt_ref[...] = section_t(1, WIDTH).astype(BF16)
    gt_ref[...] = section_t(2, LANES)[:N_GATES, :] + bg_ref[...]


def _proj(x2d, nrm, sc, sh, row0, tiles_per_row, w_r, w_t, b_gate, rope_tabs, emit_cache, tm):
    t, d = x2d.shape
    nt = t // tm
    rope = rope_tabs is not None
    row_map = lambda i: (row0 + i // tiles_per_row, 0, 0)
    tok = lambda i: (i, 0)
    feat = lambda i: (0, i)
    in_specs = [pl.BlockSpec((tm, d), tok),
                pl.BlockSpec((1, d), lambda i: (0, 0)),
                pl.BlockSpec((1, 1, d), row_map),
                pl.BlockSpec((1, 1, d), row_map),
                pl.BlockSpec(w_r.shape, lambda i: (0, 0)),
                pl.BlockSpec(w_t.shape, lambda i: (0, 0)),
                pl.BlockSpec((N_GATES, 1), lambda i: (0, 0))]
    args = [x2d, nrm, sc, sh, w_r, w_t, b_gate]
    if rope:
        tps = rope_tabs[0].shape[0] // tm
        for tab in rope_tabs:
            in_specs.append(pl.BlockSpec((tm, LANES), lambda i: (i % tps, 0)))
            args.append(tab)
    out_shape = [jax.ShapeDtypeStruct((t, 3 * WIDTH), BF16),
                 jax.ShapeDtypeStruct((WIDTH, t), BF16),
                 jax.ShapeDtypeStruct((t, WIDTH), BF16),
                 jax.ShapeDtypeStruct((t, WIDTH), BF16),
                 jax.ShapeDtypeStruct((WIDTH, t), BF16),
                 jax.ShapeDtypeStruct((N_GATES, t), F32)]
    out_specs = [pl.BlockSpec((tm, 3 * WIDTH), tok),
                 pl.BlockSpec((WIDTH, tm), feat),
                 pl.BlockSpec((tm, WIDTH), tok),
                 pl.BlockSpec((tm, WIDTH), tok),
                 pl.BlockSpec((WIDTH, tm), feat),
                 pl.BlockSpec((N_GATES, tm), feat)]
    if emit_cache:
        out_shape += [jax.ShapeDtypeStruct((t, WIDTH), F32)] * 2
        out_specs += [pl.BlockSpec((tm, WIDTH), tok)] * 2
    return pl.pallas_call(
        functools.partial(_proj_kernel, rope=rope, emit_cache=emit_cache),
        grid=(nt,),
        in_specs=in_specs,
        out_specs=out_specs,
        out_shape=out_shape,
        compiler_params=_params(("parallel",)),
        name="proj_rope" if rope else "proj_ctx",
    )(*args)


def _rope_tables(seq):
    t = jnp.arange(seq)
    row = (t // GRID_W).astype(F32)[:, None]
    col = (t % GRID_W).astype(F32)[:, None]
    lane = jnp.arange(LANES)
    freqs = ROPE_THETA ** (-(lane % 16).astype(F32) / 16.0)
    pos = jnp.where(((lane % 64) < 32)[None, :], row, col)
    ang = pos * freqs[None, :]
    cos, sin = jnp.cos(ang), jnp.sin(ang)
    first = ((lane % 32) < 16)[None, :]
    return cos, jnp.where(first, -sin, 0.0), jnp.where(first, 0.0, sin)


def _lane_scan(x, pos, op, forward):
    n = x.shape[-1]
    shift = 1
    while shift < n:
        if forward:
            y = pltpu.roll(x, shift, 1)
            ok = pos >= shift
        else:
            y = pltpu.roll(x, n - shift, 1)
            ok = pos < n - shift
        x = jnp.where(ok, op(x, y), x)
        shift *= 2
    return x


def _mlstm_kernel(q_ref, kt_ref, v_ref, g_ref, *refs, nc, hp, zero_state):
    if not zero_state:
        c0_ref, n0_ref, m0_ref = refs[:3]
        refs = refs[3:]
    hf_ref, hb_ref, cf_ref, nf_ref, mf_ref, r_ref, mp_ref, mn_ref, dc_ref, tab_ref = refs[:10]
    st_refs = refs[10:]
    _mlstm_body(q_ref, kt_ref, v_ref, g_ref, None if zero_state else (c0_ref, n0_ref, m0_ref),
                hf_ref, hb_ref, cf_ref, nf_ref, mf_ref, r_ref, mp_ref, mn_ref, dc_ref, tab_ref, st_refs, nc, hp)


def _mlstm_body(q_ref, kt_ref, v_ref, g_ref, init, hf_ref, hb_ref, cf_ref, nf_ref, mf_ref,
                r_ref, mp_ref, mn_ref, dc_ref, tab_ref, st_refs, nc, hp):
    L = CHUNK
    rows = nc * hp
    pos = lax.broadcasted_iota(jnp.int32, (rows, L), 1)
    tab_ref[...] = jnp.zeros_like(tab_ref)
    for d in range(2):
        i_pre = g_ref[0, 0, (2 * d) * rows:(2 * d + 1) * rows, :]
        lf = _log_sigmoid(g_ref[0, 0, (2 * d + 1) * rows:(2 * d + 2) * rows, :])
        pre = _lane_scan(lf, pos, jnp.add, True)
        tot = jnp.broadcast_to(jnp.sum(lf, axis=-1, keepdims=True), lf.shape)
        b = pre if d == 0 else tot - pre + lf
        r = i_pre - b
        r_max = _lane_scan(r, pos, jnp.maximum, d == 0)
        gk = tot - b + i_pre
        g_max = jnp.broadcast_to(jnp.max(gk, axis=-1, keepdims=True), gk.shape)
        m = jnp.zeros((hp, L), F32) if init is None else init[2][0, d, :, 0, :]
        for c in (range(nc) if d == 0 else reversed(range(nc))):
            sl = slice(c * hp, (c + 1) * hp)
            mp_ref[d, sl, :] = m
            m = jnp.maximum(tot[sl] + m, g_max[sl])
            mn_ref[d, sl, :] = m
        mf_ref[0, d, :, 0, :] = m
        m_prev = mp_ref[d]
        m_new = mn_ref[d]
        m_row = jnp.maximum(m_prev, r_max)
        r_ref[d] = r
        dc_ref[d] = jnp.exp(tot + m_prev - m_new)
        w_k = jnp.exp(gk - m_new)
        floor = jnp.exp(-(b + m_row))
        for c in range(nc):
            src = slice(c * hp, (c + 1) * hp)
            slot = c if d == 0 else nc - 1 - c
            for qi, val in enumerate((m_row, floor, w_k)):
                base = (2 * qi + d) * hp
                tab_ref[slot, base:base + hp, :] = val[src]

    for d in range(2):
        for hh in range(hp):
            if init is None:
                st_refs[d * hp + hh][...] = jnp.zeros((HEAD_DIM, 2 * HEAD_DIM), F32)
                continue
            st_refs[d * hp + hh][:, :HEAD_DIM] = jnp.transpose(init[0][0, d, hh])
            st_refs[d * hp + hh][:, HEAD_DIM:] = jnp.transpose(
                jnp.broadcast_to(init[1][0, d, hh], (HEAD_DIM, HEAD_DIM)))

    rr = lax.broadcasted_iota(jnp.int32, (L, L), 0)
    cc = lax.broadcasted_iota(jnp.int32, (L, L), 1)
    masks = (cc <= rr, cc >= rr)
    ones = jnp.ones((L, HEAD_DIM), BF16)

    def body(c, carry):
        tab = jnp.transpose(tab_ref[c])
        units = [(d, hh) for d in range(2) for hh in range(hp)]

        def operands(d, hh):
            ci = c if d == 0 else nc - 1 - c
            off = pl.multiple_of(ci * L, L)
            lanes = slice(hh * HEAD_DIM, (hh + 1) * HEAD_DIM)
            return ci * hp + hh, off, lanes

        def column(d, hh, which):
            j = (2 * which + d) * hp + hh
            return jnp.broadcast_to(tab[:, j:j + 1], (L, HEAD_DIM))

        qk, inter, upd = {}, {}, {}
        for d, hh in units:
            row, off, lanes = operands(d, hh)
            q = q_ref[0, pl.ds(off, L), lanes]
            kt = kt_ref[lanes, pl.ds(off, L)]
            qk[d, hh] = jnp.dot(q, kt, preferred_element_type=F32)
            inter[d, hh] = jnp.dot(q, st_refs[d * hp + hh][...].astype(BF16), preferred_element_type=F32)
        for d, hh in units:
            row, off, lanes = operands(d, hh)
            v = v_ref[0, pl.ds(off, L), lanes]
            kt = kt_ref[lanes, pl.ds(off, L)]
            wk_col = column(d, hh, 2)
            vw = (v.astype(F32) * wk_col).astype(BF16)
            upd[d, hh] = jnp.dot(kt, jnp.concatenate([vw, wk_col.astype(BF16)], axis=1),
                                 preferred_element_type=F32)
        for d, hh in units:
            row, off, lanes = operands(d, hh)
            m_col = column(d, hh, 0)
            r_row = r_ref[d, pl.ds(row, 1), :]
            m_prev = mp_ref[d, pl.ds(row, 1), :]
            v = v_ref[0, pl.ds(off, L), lanes]
            w_intra = jnp.exp(jnp.where(masks[d], r_row - m_col, -jnp.inf))
            s = (qk[d, hh] * w_intra).astype(BF16)
            intra = jnp.dot(s, jnp.concatenate([v, ones], axis=1), preferred_element_type=F32)
            w_state = jnp.exp(m_prev - m_col)
            both = intra + jnp.concatenate([w_state, w_state], axis=1) * inter[d, hh]
            den = jnp.maximum(jnp.abs(both[:, HEAD_DIM:]), column(d, hh, 1))
            out_ref = hf_ref if d == 0 else hb_ref
            out_ref[0, pl.ds(off, L), lanes] = (both[:, :HEAD_DIM] / den).astype(BF16)
        for d, hh in units:
            row, off, lanes = operands(d, hh)
            decay = dc_ref[d, pl.ds(row, 1), :]
            st = st_refs[d * hp + hh]
            st[...] = jnp.concatenate([decay, decay], axis=1) * st[...] + upd[d, hh]
        return carry

    lax.fori_loop(0, nc, body, 0, unroll=2)

    for d in range(2):
        for hh in range(hp):
            state = st_refs[d * hp + hh][...]
            cf_ref[0, d, hh] = jnp.transpose(state[:, :HEAD_DIM])
            nf_ref[0, d, hh] = jnp.transpose(state[:, HEAD_DIM:])[0:1, :]


def _mlstm(qvo, kt, gt, states, hp):
    b, s, _ = qvo.shape
    nc = s // CHUNK
    hg = N_HEADS // hp
    rows = nc * hp
    g6 = gt.reshape(4, hg, hp, b, nc, CHUNK).transpose(3, 1, 0, 4, 2, 5).reshape(b, hg, 4 * rows, CHUNK)
    blk = hp * HEAD_DIM
    nblk = WIDTH // blk
    state_spec = lambda w: pl.BlockSpec((1, 2, hp, w, HEAD_DIM), lambda i, j: (i, 0, j, 0, 0))
    in_specs = [pl.BlockSpec((1, s, blk), lambda i, j: (i, 0, j)),
                pl.BlockSpec((blk, s), lambda i, j: (j, i)),
                pl.BlockSpec((1, s, blk), lambda i, j: (i, 0, nblk + j)),
                pl.BlockSpec((1, 1, 4 * rows, CHUNK), lambda i, j: (i, j, 0, 0))]
    args = [qvo, kt, qvo, g6]
    if states is not None:
        c0, n0, m0 = states
        in_specs += [state_spec(HEAD_DIM), state_spec(1), state_spec(1)]
        args += [c0, n0.reshape(b, 2, N_HEADS, 1, HEAD_DIM),
                 jnp.broadcast_to(m0[..., None, None], (b, 2, N_HEADS, 1, HEAD_DIM))]
    out_specs = [pl.BlockSpec((1, s, blk), lambda i, j: (i, 0, j)),
                 pl.BlockSpec((1, s, blk), lambda i, j: (i, 0, j)),
                 state_spec(HEAD_DIM), state_spec(1), state_spec(1)]
    out_shape = [jax.ShapeDtypeStruct((b, s, WIDTH), BF16),
                 jax.ShapeDtypeStruct((b, s, WIDTH), BF16),
                 jax.ShapeDtypeStruct((b, 2, N_HEADS, HEAD_DIM, HEAD_DIM), F32),
                 jax.ShapeDtypeStruct((b, 2, N_HEADS, 1, HEAD_DIM), F32),
                 jax.ShapeDtypeStruct((b, 2, N_HEADS, 1, HEAD_DIM), F32)]
    scratch = [pltpu.VMEM((2, rows, CHUNK), F32),
               pltpu.VMEM((2, rows, CHUNK), F32),
               pltpu.VMEM((2, rows, CHUNK), F32),
               pltpu.VMEM((2, rows, CHUNK), F32),
               pltpu.VMEM((nc, LANES, CHUNK), F32)]
    scratch += [pltpu.VMEM((HEAD_DIM, 2 * HEAD_DIM), F32) for _ in range(2 * hp)]
    hf, hb, cf, nf, mf = pl.pallas_call(
        functools.partial(_mlstm_kernel, nc=nc, hp=hp, zero_state=states is None),
        grid=(b, hg),
        in_specs=in_specs,
        out_specs=out_specs,
        out_shape=out_shape,
        scratch_shapes=scratch,
        compiler_params=_params(("parallel", "parallel")),
        name="mlstm",
    )(*args)
    return hf, hb, cf, nf[:, :, :, 0, :], mf[:, :, :, 0, 0]


def _attn_kernel(*refs, sub, key_chunk, n_new, n_cache, lam_init):
    if n_cache:
        lp_ref, q_ref, k_ref, kc_ref, vt_ref, vct_ref, o_ref, s_even_ref, s_odd_ref = refs
    else:
        lp_ref, q_ref, k_ref, vt_ref, o_ref, s_even_ref, s_odd_ref = refs
    lp = lp_ref[...]
    lam = (jnp.exp(jnp.sum(lp[0:1] * lp[1:2], axis=-1, keepdims=True))
           - jnp.exp(jnp.sum(lp[2:3] * lp[3:4], axis=-1, keepdims=True)) + lam_init)
    lane = lax.broadcasted_iota(jnp.int32, (sub, HEAD_DIM), 1)
    n_sub = n_new // sub
    nt = (((1,), (1,)), ((), ()))

    bounds = list(range(0, n_new, key_chunk)) + [n_new]
    chunks = [(False, c0, c0, c1 - c0) for c0, c1 in zip(bounds[:-1], bounds[1:])]
    if n_cache:
        chunks.append((True, 0, n_new, n_cache))

    def keys_of(chunk):
        cached, c0, _, n = chunk
        return kc_ref[0, c0:c0 + n, :] if cached else k_ref[0, c0:c0 + n, :]

    def values_of(chunk):
        cached, c0, _, n = chunk
        return vct_ref[0, :, c0:c0 + n] if cached else vt_ref[:, c0:c0 + n]

    def queries(idx):
        r0 = pl.multiple_of(idx * sub, sub)
        q = q_ref[0, pl.ds(r0, sub), :]
        zero = jnp.zeros_like(q)
        return jnp.concatenate([jnp.where(lane < QK_DIM, q, zero), jnp.where(lane >= QK_DIM, q, zero)], axis=0)

    def scores(qq, chunk, s_ref):
        s = lax.dot_general(keys_of(chunk), qq, nt, preferred_element_type=F32)
        s_ref[chunk[2]:chunk[2] + chunk[3], :] = s
        return jnp.max(s, axis=0, keepdims=True)

    def step(idx_next, s_next_ref, idx, s_ref, m):
        qq = None if idx_next is None else queries(idx_next)
        m_next, l, o = None, None, None
        for chunk in chunks:
            if qq is not None:
                mc = scores(qq, chunk, s_next_ref)
                m_next = mc if m_next is None else jnp.maximum(m_next, mc)
            e = jnp.exp2(s_ref[chunk[2]:chunk[2] + chunk[3], :] - m)
            lc = jnp.sum(e, axis=0, keepdims=True)
            oc = jnp.dot(values_of(chunk), e.astype(BF16), preferred_element_type=F32)
            l = lc if l is None else l + lc
            o = oc if o is None else o + oc
        o = o / l
        out_t = o[:, :sub] - lam * o[:, sub:]
        r0 = pl.multiple_of(idx * sub, sub)
        o_ref[0, pl.ds(r0, sub), :] = jnp.transpose(out_t).astype(BF16)
        return m_next

    def pair(i, m_even):
        a = 2 * i
        m_odd = step(a + 1, s_odd_ref, a, s_even_ref, m_even)
        return step(a + 2, s_even_ref, a + 1, s_odd_ref, m_odd)

    qq0 = queries(0)
    m0 = None
    for chunk in chunks:
        mc = scores(qq0, chunk, s_even_ref)
        m0 = mc if m0 is None else jnp.maximum(m0, mc)
    m_even = lax.fori_loop(0, n_sub // 2 - 1, pair, m0)
    m_odd = step(n_sub - 1, s_odd_ref, n_sub - 2, s_even_ref, m_even)
    step(None, None, n_sub - 1, s_odd_ref, m_odd)


def _attn_short_kernel(lp_ref, q_ref, k_ref, vt_ref, o_ref, *, lam_init):
    lp = lp_ref[...]
    lam = (jnp.exp(jnp.sum(lp[0:1] * lp[1:2], axis=-1, keepdims=True))
           - jnp.exp(jnp.sum(lp[2:3] * lp[3:4], axis=-1, keepdims=True)) + lam_init)
    s_len = q_ref.shape[1]
    lane = lax.broadcasted_iota(jnp.int32, (s_len, HEAD_DIM), 1)
    nt = (((1,), (1,)), ((), ()))
    for hh in range(N_HEADS):
        lanes = slice(hh * HEAD_DIM, (hh + 1) * HEAD_DIM)
        q = q_ref[0, :, lanes]
        zero = jnp.zeros_like(q)
        qq = jnp.concatenate([jnp.where(lane < QK_DIM, q, zero), jnp.where(lane >= QK_DIM, q, zero)], axis=0)
        s = lax.dot_general(k_ref[0, :, lanes], qq, nt, preferred_element_type=F32)
        e = jnp.exp2(s - jnp.max(s, axis=0, keepdims=True))
        o = jnp.dot(vt_ref[lanes, :], e.astype(BF16), preferred_element_type=F32)
        o = o / jnp.sum(e, axis=0, keepdims=True)
        out_t = o[:, :s_len] - lam * o[:, s_len:]
        o_ref[0, :, lanes] = jnp.transpose(out_t).astype(BF16)


def _attn_short(lp, q, k, vt, lam_init):
    b, s, _ = q.shape
    return pl.pallas_call(
        functools.partial(_attn_short_kernel, lam_init=lam_init),
        grid=(b,),
        in_specs=[pl.BlockSpec(lp.shape, lambda i: (0, 0)),
                  pl.BlockSpec((1, s, WIDTH), lambda i: (i, 0, 0)),
                  pl.BlockSpec((1, s, WIDTH), lambda i: (i, 0, 0)),
                  pl.BlockSpec((WIDTH, s), lambda i: (0, i))],
        out_specs=pl.BlockSpec((1, s, WIDTH), lambda i: (i, 0, 0)),
        out_shape=jax.ShapeDtypeStruct((b, s, WIDTH), BF16),
        compiler_params=_params(("parallel",)),
        name="diff_attn_short",
    )(lp, q, k, vt)


def _attn(lp, q, k, vt, cache, lam_init):
    b, s, _ = q.shape
    if cache is None and s <= ATTN_SHORT:
        return _attn_short(lp, q, k, vt, lam_init)
    assert s % (2 * ATTN_SUB) == 0
    n_cache = 0 if cache is None else cache[0].shape[1]
    in_specs = [pl.BlockSpec(lp.shape, lambda i, h: (0, 0)),
                pl.BlockSpec((1, s, HEAD_DIM), lambda i, h: (i, 0, h)),
                pl.BlockSpec((1, s, HEAD_DIM), lambda i, h: (i, 0, h))]
    args = [lp, q, k]
    if n_cache:
        in_specs.append(pl.BlockSpec((1, n_cache, HEAD_DIM), lambda i, h: (i, 0, h)))
        args.append(cache[0])
    in_specs.append(pl.BlockSpec((HEAD_DIM, s), lambda i, h: (h, i)))
    args.append(vt)
    if n_cache:
        in_specs.append(pl.BlockSpec((1, HEAD_DIM, n_cache), lambda i, h: (i, h, 0)))
        args.append(cache[1])
    keys = s + n_cache
    return pl.pallas_call(
        functools.partial(_attn_kernel, sub=ATTN_SUB, key_chunk=min(ATTN_KEY_CHUNK, s), n_new=s, n_cache=n_cache,
                          lam_init=lam_init),
        grid=(b, N_HEADS),
        in_specs=in_specs,
        out_specs=pl.BlockSpec((1, s, HEAD_DIM), lambda i, h: (i, 0, h)),
        out_shape=jax.ShapeDtypeStruct((b, s, WIDTH), BF16),
        scratch_shapes=[pltpu.VMEM((keys, 2 * ATTN_SUB), F32), pltpu.VMEM((keys, 2 * ATTN_SUB), F32)],
        compiler_params=_params(("parallel", "parallel")),
        name="diff_attn",
    )(*args)


def _head_norm(x):
    outs = []
    for j in range(N_HEADS):
        xb = x[:, j * HEAD_DIM:(j + 1) * HEAD_DIM]
        outs.append(xb * lax.rsqrt(jnp.mean(xb * xb, axis=-1, keepdims=True) + EPS))
    return jnp.concatenate(outs, axis=1)


def _mix_kernel(hf_ref, hb_ref, om_ref, hd_ref, x_ref, g1_ref, nsc2_ref, sh2_ref, gm_ref, gd_ref,
                wout_ref, wr_ref, br_ref, x1_ref, h2_ref, chl_ref, grpt_ref, *, diff_scale, parts):
    n = x_ref.shape[0] // parts
    for p in range(parts):
        _mix_rows(slice(p * n, (p + 1) * n), hf_ref, hb_ref, om_ref, hd_ref, x_ref, g1_ref, nsc2_ref, sh2_ref,
                  gm_ref, gd_ref, wout_ref, wr_ref, br_ref, x1_ref, h2_ref, chl_ref, grpt_ref, diff_scale)


def _mix_rows(rows, hf_ref, hb_ref, om_ref, hd_ref, x_ref, g1_ref, nsc2_ref, sh2_ref, gm_ref, gd_ref,
              wout_ref, wr_ref, br_ref, x1_ref, h2_ref, chl_ref, grpt_ref, diff_scale):
    hm = _head_norm(hf_ref[rows, :].astype(F32) + hb_ref[rows, :].astype(F32))
    hm = hm * gm_ref[...] * _sigmoid(om_ref[rows, :].astype(F32))
    hd = _head_norm(hd_ref[rows, :].astype(F32)) * gd_ref[...] * diff_scale
    mix = jnp.dot(jnp.concatenate([hm.astype(BF16), hd.astype(BF16)], axis=1), wout_ref[...],
                  preferred_element_type=F32)
    x1 = x_ref[rows, :] + g1_ref[0] * mix
    x1_ref[rows, :] = x1
    ms = jnp.mean(x1 * x1, axis=-1, keepdims=True)
    h2 = x1 * lax.rsqrt(ms + EPS) * nsc2_ref[0] + sh2_ref[0]
    h2_hi = h2.astype(BF16)
    h2_ref[rows, :] = h2_hi

    h2_lo = (h2 - h2_hi.astype(F32)).astype(BF16)
    both = (jnp.dot(h2_hi, wr_ref[...], preferred_element_type=F32)
            + jnp.dot(h2_lo, wr_ref[...], preferred_element_type=F32))
    logits = both[:, :LANES] + both[:, LANES:] + br_ref[...]
    lane = lax.broadcasted_iota(jnp.int32, logits.shape, 1).astype(F32)
    big = float(LANES)
    is_grp = (lane >= N_EXPERTS) & (lane < N_EXPERTS + N_GROUPS)
    lg = jnp.where(is_grp, logits, -jnp.inf)
    mx = jnp.max(lg, axis=-1, keepdims=True)
    p_grp = 1.0 / jnp.sum(jnp.exp(lg - mx), axis=-1, keepdims=True)
    grp = jnp.min(jnp.where(lg == mx, lane, big), axis=-1, keepdims=True) - N_EXPERTS
    base = grp * EXPERTS_PER_GROUP
    sel = (lane >= base) & (lane < base + EXPERTS_PER_GROUP)
    le = jnp.where(sel, logits, -jnp.inf)
    ee = jnp.exp(le - jnp.max(le, axis=-1, keepdims=True))
    pe = ee / jnp.sum(ee, axis=-1, keepdims=True)
    top1 = jnp.max(pe, axis=-1, keepdims=True)
    idx1 = jnp.min(jnp.where(sel & (pe == top1), lane, big), axis=-1, keepdims=True)
    rest = sel & (lane != idx1)
    top2 = jnp.max(jnp.where(rest, pe, -1.0), axis=-1, keepdims=True)
    idx2 = jnp.min(jnp.where(rest & (pe == top2), lane, big), axis=-1, keepdims=True)
    denom = top1 + top2
    w1 = top1 / denom * p_grp
    w2 = top2 / denom * p_grp
    chl = jnp.zeros_like(logits)
    for j in range(EXPERTS_PER_GROUP):
        cj = jnp.where(idx1 == base + j, w1, 0.0) + jnp.where(idx2 == base + j, w2, 0.0)
        hi = cj.astype(BF16).astype(F32)
        chl = chl + jnp.where(lane == j, hi, 0.0) + jnp.where(lane == EXPERTS_PER_GROUP + j, cj - hi, 0.0)
    chl_ref[rows, :] = chl.astype(BF16)
    grpt_ref[:, rows] = jnp.transpose(jnp.broadcast_to(grp, logits.shape))[:SUBLANES, :]


def _mix(hf, hb, qvo, hd, x2d, g1, nsc2, sh2, row0, tiles_per_row, gm, gd, wout, wr, br, diff_scale, tm):
    t, d = x2d.shape
    row_map = lambda i: (row0 + i // tiles_per_row, 0, 0)
    tok = lambda i: (i, 0)
    full = lambda i: (0, 0)
    return pl.pallas_call(
        functools.partial(_mix_kernel, diff_scale=diff_scale, parts=tm // MIX_PART),
        grid=(t // tm,),
        in_specs=[pl.BlockSpec((tm, WIDTH), tok),
                  pl.BlockSpec((tm, WIDTH), tok),
                  pl.BlockSpec((tm, WIDTH), lambda i: (i, 2)),
                  pl.BlockSpec((tm, WIDTH), tok),
                  pl.BlockSpec((tm, d), tok),
                  pl.BlockSpec((1, 1, d), row_map),
                  pl.BlockSpec((1, 1, d), row_map),
                  pl.BlockSpec((1, 1, d), row_map),
                  pl.BlockSpec((1, WIDTH), full),
                  pl.BlockSpec((1, WIDTH), full),
                  pl.BlockSpec(wout.shape, full),
                  pl.BlockSpec(wr.shape, full),
                  pl.BlockSpec(br.shape, full)],
        out_specs=[pl.BlockSpec((tm, d), tok),
                   pl.BlockSpec((tm, d), tok),
                   pl.BlockSpec((tm, LANES), tok),
                   pl.BlockSpec((SUBLANES, tm), lambda i: (0, i))],
        out_shape=[jax.ShapeDtypeStruct((t, d), F32),
                   jax.ShapeDtypeStruct((t, d), BF16),
                   jax.ShapeDtypeStruct((t, LANES), BF16),
                   jax.ShapeDtypeStruct((SUBLANES, t), F32)],
        compiler_params=_params(("parallel",)),
        name="mix_router",
    )(hf, hb, qvo, hd, x2d, g1, nsc2, sh2, gm, gd, wout, wr, br)


def _moe_kernel(h2_ref, chl_ref, grpt_ref, before_ref, wgu_ref, wdn_ref, x1_ref, g2_ref, nrm_ref, y_ref,
                acc_ref, rank_ref, *, hidden, blk):
    g = pl.program_id(1)
    tm = h2_ref.shape[0]

    @pl.when(g == 0)
    def _():
        acc_ref[...] = jnp.zeros_like(acc_ref)
        gid = lax.broadcasted_iota(jnp.int32, (SUBLANES, tm), 0).astype(F32)
        member = grpt_ref[...] == gid
        rank = jnp.dot(jnp.where(member, 1.0, 0.0).astype(BF16), before_ref[...], preferred_element_type=F32)
        rank_ref[...] = jnp.where(member, rank, -1.0)

    rank_g = rank_ref[pl.ds(g, 1), :]
    n_g = jnp.max(rank_g) + 1.0

    sizes, start = [], 0
    while start < tm:
        size = min(blk // 4 if len(sizes) == 1 else blk, tm - start)
        sizes.append((start, size))
        start += size

    for start, size in sizes:
        @pl.when(n_g > start)
        def _():
            slot = lax.broadcasted_iota(jnp.int32, (size, tm), 0).astype(F32) + float(start)
            pick = jnp.where(rank_g == slot, 1.0, 0.0).astype(BF16)
            xg = jnp.dot(pick, h2_ref[...], preferred_element_type=F32).astype(BF16)
            cg = jnp.dot(pick, chl_ref[...], preferred_element_type=F32)
            gus = [jnp.dot(xg, wgu_ref[e], preferred_element_type=F32) for e in range(EXPERTS_PER_GROUP)]
            out = jnp.zeros((size, acc_ref.shape[1]), F32)
            for e, gu in enumerate(gus):
                a, u = gu[:, :hidden], gu[:, hidden:]
                ce = cg[:, e:e + 1] + cg[:, EXPERTS_PER_GROUP + e:EXPERTS_PER_GROUP + e + 1]
                act = (a * _sigmoid(a) * u * ce).astype(BF16)
                out = out + jnp.dot(act, wdn_ref[e], preferred_element_type=F32)
            acc_ref[...] += lax.dot_general(pick, out.astype(BF16), (((0,), (0,)), ((), ())),
                                            preferred_element_type=F32)

    @pl.when(g == pl.num_programs(1) - 1)
    def _():
        x2 = x1_ref[...] + g2_ref[0] * acc_ref[...]
        ms = jnp.mean(x2 * x2, axis=-1, keepdims=True)
        y_ref[...] = x2 * lax.rsqrt(ms + EPS) * nrm_ref[...]


def _moe(h2, chl, grpt, wgu, wdn, x1, g2, row0, tiles_per_row, nrm, tm, blk):
    t, d = x1.shape
    _, _, two_h = wgu.shape
    hidden = two_h // 2
    row_map = lambda i, g: (row0 + i // tiles_per_row, 0, 0)
    tok = lambda i, g: (i, 0)
    before = jnp.triu(jnp.ones((tm, tm), BF16), 1)
    return pl.pallas_call(
        functools.partial(_moe_kernel, hidden=hidden, blk=blk),
        grid=(t // tm, N_GROUPS),
        in_specs=[pl.BlockSpec((tm, d), tok),
                  pl.BlockSpec((tm, LANES), tok),
                  pl.BlockSpec((SUBLANES, tm), lambda i, g: (0, i)),
                  pl.BlockSpec((tm, tm), lambda i, g: (0, 0)),
                  pl.BlockSpec((EXPERTS_PER_GROUP, d, two_h), lambda i, g: (g, 0, 0)),
                  pl.BlockSpec((EXPERTS_PER_GROUP, hidden, d), lambda i, g: (g, 0, 0)),
                  pl.BlockSpec((tm, d), tok),
                  pl.BlockSpec((1, 1, d), row_map),
                  pl.BlockSpec((1, d), lambda i, g: (0, 0))],
        out_specs=pl.BlockSpec((tm, d), tok),
        out_shape=jax.ShapeDtypeStruct((t, d), F32),
        scratch_shapes=[pltpu.VMEM((tm, d), F32), pltpu.VMEM((SUBLANES, tm), F32)],
        compiler_params=_params(("parallel", "arbitrary")),
        name="moe_experts",
    )(h2, chl, grpt, before, wgu, wdn, x1, g2, nrm)


def _layer(x, mods, row0, per_request, weights, states, cache, rope_tabs, emit_cache, tm, hp):
    b, s, d = x.shape
    sh1, sc1, g1, sh2, sc2, g2 = mods
    (nrm_mix, nrm_ffn, w_in_r, w_in_t, b_gate, gm, gd, lp, wout, wr, br, wgu, wdn, nrm_final, lam_init) = weights
    x2d = x.reshape(b * s, d)
    tiles_per_row = (s // tm) if per_request else (b * s // tm)
    outs = _proj(x2d, nrm_mix, sc1, sh1, row0, tiles_per_row, w_in_r, w_in_t, b_gate, rope_tabs, emit_cache, tm)
    qvo, kmt, qd, kd, vdt, gt = outs[:6]
    hf, hb, cf, nf, mf = _mlstm(qvo.reshape(b, s, 3 * WIDTH), kmt, gt, states, hp)
    hd = _attn(lp, qd.reshape(b, s, WIDTH), kd.reshape(b, s, WIDTH), vdt, cache, lam_init)
    x1, h2, chl, grpt = _mix(hf.reshape(b * s, WIDTH), hb.reshape(b * s, WIDTH), qvo, hd.reshape(b * s, WIDTH),
                             x2d, g1, nrm_ffn[None] * (1.0 + sc2), sh2, row0, tiles_per_row, gm, gd, wout, wr, br,
                             1.0 - lam_init, tm)
    y = _moe(h2, chl, grpt, wgu, wdn, x1, g2, row0, (s if per_request else b * s) // MOE_TILE,
             nrm_final, MOE_TILE, MOE_BLOCK)
    new_kv = outs[6:] if emit_cache else None
    return y.reshape(b, s, d), new_kv, (cf, nf, mf)


def kernel(x_prompt, x_sample, cache_k, cache_v, state_C, state_n, state_m, c, c_ctx, w_ada, b_ada, norm_mix, norm_ffn, w_in, b_gate, mlstm_norm, diff_norm, diff_lambda, w_out, w_route_group, b_route_group, w_route_expert, b_route_expert, w_gate_up, w_down, final_norm):
    depth = w_in.shape[0]
    assert depth == 1, "the final norm is fused into the last layer's expert kernel"
    bp, sp, d = x_prompt.shape
    bs, ss, _ = x_sample.shape
    rows = 16
    assert bs < rows
    cvec = jnp.zeros((rows, d), F32).at[:bs].set(c).at[bs].set(c_ctx)
    rope_tabs = _rope_tables(ss)
    xp, xs = x_prompt, x_sample
    new_k, new_v, new_c, new_n, new_m = [], [], [], [], []
    for l in range(depth):
        lam_init = 0.8 - 0.6 * math.exp(-0.3 * l)
        mod = _ada(cvec, w_ada[l], b_ada[l])
        mods = [mod[:, i * d:(i + 1) * d].reshape(rows, 1, d) for i in range(6)]
        wl = w_in[l]
        off_dq = 4 * WIDTH + N_GATES
        w_in_r = jnp.concatenate(
            [wl[:, :WIDTH], wl[:, 2 * WIDTH:4 * WIDTH], wl[:, off_dq:]], axis=1).astype(BF16)
        w_in_t = jnp.concatenate(
            [wl[:, WIDTH:2 * WIDTH], wl[:, off_dq + 2 * WIDTH:], wl[:, 4 * WIDTH:off_dq],
             jnp.zeros((d, LANES - N_GATES), F32)], axis=1).T.astype(BF16)
        bg = b_gate[l].reshape(N_GATES, 1)
        wr = jnp.concatenate([w_route_expert[l], w_route_group[l],
                              jnp.zeros((d, LANES - N_EXPERTS - N_GROUPS), F32)], axis=1)
        wr_hi = wr.astype(BF16)
        wr = jnp.concatenate([wr_hi, (wr - wr_hi.astype(F32)).astype(BF16)], axis=1)
        br = jnp.concatenate([b_route_expert[l], b_route_group[l],
                              jnp.zeros((LANES - N_EXPERTS - N_GROUPS,), F32)]).reshape(1, LANES)
        weights = (norm_mix[l].reshape(1, d), norm_ffn[l].reshape(1, d), w_in_r, w_in_t, bg,
                   mlstm_norm[l].reshape(1, WIDTH), diff_norm[l].reshape(1, WIDTH), diff_lambda[l],
                   w_out[l].astype(BF16), wr, br, w_gate_up[l].astype(BF16), w_down[l].astype(BF16),
                   final_norm.reshape(1, d), lam_init)

        xp, (nk, nv), (cf, nf, mf) = _layer(xp, mods, bs, False, weights, None, None, None, True,
                                            tm=1024, hp=4)
        new_k.append(nk.reshape(bp, sp, N_HEADS, 2, QK_DIM))
        new_v.append(nv.reshape(bp, sp, N_HEADS, HEAD_DIM))
        new_c.append(cf)
        new_n.append(nf)
        new_m.append(mf)

        states = (state_C[:, l], state_n[:, l], state_m[:, l])
        cache = (cache_k[:, l].reshape(bs, -1, WIDTH).astype(BF16),
                 cache_v[:, l].reshape(bs, -1, WIDTH).transpose(0, 2, 1).astype(BF16))
        xs, _, _ = _layer(xs, mods, 0, True, weights, states, cache, rope_tabs, False,
                          tm=1024, hp=4)

    return (xp, xs, jnp.stack(new_k, axis=1), jnp.stack(new_v, axis=1),
            jnp.stack(new_c, axis=1), jnp.stack(new_n, axis=1), jnp.stack(new_m, axis=1))
```

```python
import functools
import math

import jax
import jax.numpy as jnp
from jax import lax
from jax.experimental import pallas as pl
from jax.experimental.pallas import tpu as pltpu

F32 = jnp.float32
BF16 = jnp.bfloat16
HIGHEST = lax.Precision.HIGHEST

EPS = 1e-6
GRID_W = 64
ROPE_THETA = 10000.0
N_HEADS = 4
HEAD_DIM = 128
QK_DIM = 64
WIDTH = N_HEADS * HEAD_DIM
CHUNK = 128
N_GATES = 4 * N_HEADS
N_GROUPS = 4
EXPERTS_PER_GROUP = 4
N_EXPERTS = 16
LANES = 128
SUBLANES = 8
MOE_BLOCK = 256
MOE_TILE = 1024
MIX_PART = 256
ATTN_SUB = 128
ATTN_KEY_CHUNK = 1024
ATTN_SHORT = 512
LOG2E = math.log2(math.e)
VMEM_LIMIT = 56 * 1024 * 1024


def _params(sem):
    return pltpu.CompilerParams(dimension_semantics=sem, vmem_limit_bytes=VMEM_LIMIT)


def _log_sigmoid(x):
    return jnp.minimum(x, 0.0) - jnp.log1p(jnp.exp(-jnp.abs(x)))


def _sigmoid(x):
    return 1.0 / (1.0 + jnp.exp(-x))


def _ada_kernel(c_ref, w_ref, b_ref, o_ref):
    c = c_ref[...]
    s = c * _sigmoid(c)
    o_ref[...] = jnp.dot(s, w_ref[...], preferred_element_type=F32, precision=HIGHEST) + b_ref[...]


def _ada(cvec, w_ada, b_ada):
    rows, d = cvec.shape
    n = w_ada.shape[1]
    tn = 1024
    return pl.pallas_call(
        _ada_kernel,
        grid=(n // tn,),
        in_specs=[pl.BlockSpec((rows, d), lambda j: (0, 0)),
                  pl.BlockSpec((d, tn), lambda j: (0, j)),
                  pl.BlockSpec((1, tn), lambda j: (0, j))],
        out_specs=pl.BlockSpec((rows, tn), lambda j: (0, j)),
        out_shape=jax.ShapeDtypeStruct((rows, n), F32),
        compiler_params=_params(("arbitrary",)),
        name="ada",
    )(cvec, w_ada, b_ada.reshape(1, n))


def _proj_kernel(*refs, rope, emit_cache):
    x_ref, nrm_ref, sc_ref, sh_ref, w_ref, wt_ref, bg_ref = refs[:7]
    refs = refs[7:]
    if rope:
        cos_ref, sa_ref, sb_ref = refs[:3]
        refs = refs[3:]
    m_ref, kmt_ref, q_ref, k_ref, vt_ref, gt_ref = refs[:6]
    refs = refs[6:]
    if emit_cache:
        newk_ref, newv_ref = refs

    x = x_ref[...]
    ms = jnp.mean(x * x, axis=-1, keepdims=True)
    y = x * lax.rsqrt(ms + EPS) * nrm_ref[...]
    h = (y * (1.0 + sc_ref[0]) + sh_ref[0]).astype(BF16)

    def section(j):
        return jnp.dot(h, w_ref[:, j * WIDTH:(j + 1) * WIDTH], preferred_element_type=F32)

    def rotate(p):
        cos, sa, sb = cos_ref[...], sa_ref[...], sb_ref[...]
        outs = []
        for j in range(N_HEADS):
            xb = p[:, j * LANES:(j + 1) * LANES]
            outs.append(xb * cos + pltpu.roll(xb, LANES - 16, 1) * sa + pltpu.roll(xb, 16, 1) * sb)
        return jnp.concatenate(outs, axis=1)

    def section_t(j, n):
        return lax.dot_general(wt_ref[j * WIDTH:j * WIDTH + n, :], h, (((1,), (1,)), ((), ())),
                               preferred_element_type=F32)

    for j in range(3):
        m_ref[:, j * WIDTH:(j + 1) * WIDTH] = section(j).astype(BF16)
    kmt_ref[...] = (section_t(0, WIDTH) * (HEAD_DIM ** -0.5)).astype(BF16)
    pq = section(3)
    if rope:
        pq = rotate(pq)
    q_ref[...] = (pq * (LOG2E * QK_DIM ** -0.5)).astype(BF16)
    pk = section(4)
    if emit_cache:
        newk_ref[...] = pk
    if rope:
        pk = rotate(pk)
    k_ref[...] = pk.astype(BF16)
    if emit_cache:
        newv_ref[...] = section(5)
    vt_ref[...] = section_t(1, WIDTH).astype(BF16)
    gt_ref[...] = section_t(2, LANES)[:N_GATES, :] + bg_ref[...]


def _proj(x2d, nrm, sc, sh, row0, tiles_per_row, w_r, w_t, b_gate, rope_tabs, emit_cache, tm):
    t, d = x2d.shape
    nt = t // tm
    rope = rope_tabs is not None
    row_map = lambda i: (row0 + i // tiles_per_row, 0, 0)
    tok = lambda i: (i, 0)
    feat = lambda i: (0, i)
    in_specs = [pl.BlockSpec((tm, d), tok),
                pl.BlockSpec((1, d), lambda i: (0, 0)),
                pl.BlockSpec((1, 1, d), row_map),
                pl.BlockSpec((1, 1, d), row_map),
                pl.BlockSpec(w_r.shape, lambda i: (0, 0)),
                pl.BlockSpec(w_t.shape, lambda i: (0, 0)),
                pl.BlockSpec((N_GATES, 1), lambda i: (0, 0))]
    args = [x2d, nrm, sc, sh, w_r, w_t, b_gate]
    if rope:
        tps = rope_tabs[0].shape[0] // tm
        for tab in rope_tabs:
            in_specs.append(pl.BlockSpec((tm, LANES), lambda i: (i % tps, 0)))
            args.append(tab)
    out_shape = [jax.ShapeDtypeStruct((t, 3 * WIDTH), BF16),
                 jax.ShapeDtypeStruct((WIDTH, t), BF16),
                 jax.ShapeDtypeStruct((t, WIDTH), BF16),
                 jax.ShapeDtypeStruct((t, WIDTH), BF16),
                 jax.ShapeDtypeStruct((WIDTH, t), BF16),
                 jax.ShapeDtypeStruct((N_GATES, t), F32)]
    out_specs = [pl.BlockSpec((tm, 3 * WIDTH), tok),
                 pl.BlockSpec((WIDTH, tm), feat),
                 pl.BlockSpec((tm, WIDTH), tok),
                 pl.BlockSpec((tm, WIDTH), tok),
                 pl.BlockSpec((WIDTH, tm), feat),
                 pl.BlockSpec((N_GATES, tm), feat)]
    if emit_cache:
        out_shape += [jax.ShapeDtypeStruct((t, WIDTH), F32)] * 2
        out_specs += [pl.BlockSpec((tm, WIDTH), tok)] * 2
    return pl.pallas_call(
        functools.partial(_proj_kernel, rope=rope, emit_cache=emit_cache),
        grid=(nt,),
        in_specs=in_specs,
        out_specs=out_specs,
        out_shape=out_shape,
        compiler_params=_params(("parallel",)),
        name="proj_rope" if rope else "proj_ctx",
    )(*args)


def _rope_tables(seq):
    t = jnp.arange(seq)
    row = (t // GRID_W).astype(F32)[:, None]
    col = (t % GRID_W).astype(F32)[:, None]
    lane = jnp.arange(LANES)
    freqs = ROPE_THETA ** (-(lane % 16).astype(F32) / 16.0)
    pos = jnp.where(((lane % 64) < 32)[None, :], row, col)
    ang = pos * freqs[None, :]
    cos, sin = jnp.cos(ang), jnp.sin(ang)
    first = ((lane % 32) < 16)[None, :]
    return cos, jnp.where(first, -sin, 0.0), jnp.where(first, 0.0, sin)


def _lane_scan(x, pos, op, forward):
    n = x.shape[-1]
    shift = 1
    while shift < n:
        if forward:
            y = pltpu.roll(x, shift, 1)
            ok = pos >= shift
        else:
            y = pltpu.roll(x, n - shift, 1)
            ok = pos < n - shift
        x = jnp.where(ok, op(x, y), x)
        shift *= 2
    return x


def _mlstm_kernel(q_ref, kt_ref, v_ref, g_ref, *refs, nc, hp, zero_state):
    if not zero_state:
        c0_ref, n0_ref, m0_ref = refs[:3]
        refs = refs[3:]
    hf_ref, hb_ref, cf_ref, nf_ref, mf_ref, r_ref, mp_ref, mn_ref, dc_ref, tab_ref = refs[:10]
    st_refs = refs[10:]
    _mlstm_body(q_ref, kt_ref, v_ref, g_ref, None if zero_state else (c0_ref, n0_ref, m0_ref),
                hf_ref, hb_ref, cf_ref, nf_ref, mf_ref, r_ref, mp_ref, mn_ref, dc_ref, tab_ref, st_refs, nc, hp)


def _mlstm_body(q_ref, kt_ref, v_ref, g_ref, init, hf_ref, hb_ref, cf_ref, nf_ref, mf_ref,
                r_ref, mp_ref, mn_ref, dc_ref, tab_ref, st_refs, nc, hp):
    L = CHUNK
    rows = nc * hp
    pos = lax.broadcasted_iota(jnp.int32, (rows, L), 1)
    tab_ref[...] = jnp.zeros_like(tab_ref)
    for d in range(2):
        i_pre = g_ref[0, 0, (2 * d) * rows:(2 * d + 1) * rows, :]
        lf = _log_sigmoid(g_ref[0, 0, (2 * d + 1) * rows:(2 * d + 2) * rows, :])
        pre = _lane_scan(lf, pos, jnp.add, True)
        tot = jnp.broadcast_to(jnp.sum(lf, axis=-1, keepdims=True), lf.shape)
        b = pre if d == 0 else tot - pre + lf
        r = i_pre - b
        r_max = _lane_scan(r, pos, jnp.maximum, d == 0)
        gk = tot - b + i_pre
        g_max = jnp.broadcast_to(jnp.max(gk, axis=-1, keepdims=True), gk.shape)
        m = jnp.zeros((hp, L), F32) if init is None else init[2][0, d, :, 0, :]
        for c in (range(nc) if d == 0 else reversed(range(nc))):
            sl = slice(c * hp, (c + 1) * hp)
            mp_ref[d, sl, :] = m
            m = jnp.maximum(tot[sl] + m, g_max[sl])
            mn_ref[d, sl, :] = m
        mf_ref[0, d, :, 0, :] = m
        m_prev = mp_ref[d]
        m_new = mn_ref[d]
        m_row = jnp.maximum(m_prev, r_max)
        r_ref[d] = r
        dc_ref[d] = jnp.exp(tot + m_prev - m_new)
        w_k = jnp.exp(gk - m_new)
        floor = jnp.exp(-(b + m_row))
        for c in range(nc):
            src = slice(c * hp, (c + 1) * hp)
            slot = c if d == 0 else nc - 1 - c
            for qi, val in enumerate((m_row, floor, w_k)):
                base = (2 * qi + d) * hp
                tab_ref[slot, base:base + hp, :] = val[src]

    for d in range(2):
        for hh in range(hp):
            if init is None:
                st_refs[d * hp + hh][...] = jnp.zeros((HEAD_DIM, 2 * HEAD_DIM), F32)
                continue
            st_refs[d * hp + hh][:, :HEAD_DIM] = jnp.transpose(init[0][0, d, hh])
            st_refs[d * hp + hh][:, HEAD_DIM:] = jnp.transpose(
                jnp.broadcast_to(init[1][0, d, hh], (HEAD_DIM, HEAD_DIM)))

    rr = lax.broadcasted_iota(jnp.int32, (L, L), 0)
    cc = lax.broadcasted_iota(jnp.int32, (L, L), 1)
    masks = (cc <= rr, cc >= rr)
    ones = jnp.ones((L, HEAD_DIM), BF16)

    def body(c, carry):
        tab = jnp.transpose(tab_ref[c])
        units = [(d, hh) for d in range(2) for hh in range(hp)]

        def operands(d, hh):
            ci = c if d == 0 else nc - 1 - c
            off = pl.multiple_of(ci * L, L)
            lanes = slice(hh * HEAD_DIM, (hh + 1) * HEAD_DIM)
            return ci * hp + hh, off, lanes

        def column(d, hh, which):
            j = (2 * which + d) * hp + hh
            return jnp.broadcast_to(tab[:, j:j + 1], (L, HEAD_DIM))

        qk, inter, upd = {}, {}, {}
        for d, hh in units:
            row, off, lanes = operands(d, hh)
            q = q_ref[0, pl.ds(off, L), lanes]
            kt = kt_ref[lanes, pl.ds(off, L)]
            qk[d, hh] = jnp.dot(q, kt, preferred_element_type=F32)
            inter[d, hh] = jnp.dot(q, st_refs[d * hp + hh][...].astype(BF16), preferred_element_type=F32)
        for d, hh in units:
            row, off, lanes = operands(d, hh)
            v = v_ref[0, pl.ds(off, L), lanes]
            kt = kt_ref[lanes, pl.ds(off, L)]
            wk_col = column(d, hh, 2)
            vw = (v.astype(F32) * wk_col).astype(BF16)
            upd[d, hh] = jnp.dot(kt, jnp.concatenate([vw, wk_col.astype(BF16)], axis=1),
                                 preferred_element_type=F32)
        for d, hh in units:
            row, off, lanes = operands(d, hh)
            m_col = column(d, hh, 0)
            r_row = r_ref[d, pl.ds(row, 1), :]
            m_prev = mp_ref[d, pl.ds(row, 1), :]
            v = v_ref[0, pl.ds(off, L), lanes]
            w_intra = jnp.exp(jnp.where(masks[d], r_row - m_col, -jnp.inf))
            s = (qk[d, hh] * w_intra).astype(BF16)
            intra = jnp.dot(s, jnp.concatenate([v, ones], axis=1), preferred_element_type=F32)
            w_state = jnp.exp(m_prev - m_col)
            both = intra + jnp.concatenate([w_state, w_state], axis=1) * inter[d, hh]
            den = jnp.maximum(jnp.abs(both[:, HEAD_DIM:]), column(d, hh, 1))
            out_ref = hf_ref if d == 0 else hb_ref
            out_ref[0, pl.ds(off, L), lanes] = (both[:, :HEAD_DIM] / den).astype(BF16)
        for d, hh in units:
            row, off, lanes = operands(d, hh)
            decay = dc_ref[d, pl.ds(row, 1), :]
            st = st_refs[d * hp + hh]
            st[...] = jnp.concatenate([decay, decay], axis=1) * st[...] + upd[d, hh]
        return carry

    lax.fori_loop(0, nc, body, 0, unroll=2)

    for d in range(2):
        for hh in range(hp):
            state = st_refs[d * hp + hh][...]
            cf_ref[0, d, hh] = jnp.transpose(state[:, :HEAD_DIM])
            nf_ref[0, d, hh] = jnp.transpose(state[:, HEAD_DIM:])[0:1, :]


def _mlstm(qvo, kt, gt, states, hp):
    b, s, _ = qvo.shape
    nc = s // CHUNK
    hg = N_HEADS // hp
    rows = nc * hp
    g6 = gt.reshape(4, hg, hp, b, nc, CHUNK).transpose(3, 1, 0, 4, 2, 5).reshape(b, hg, 4 * rows, CHUNK)
    blk = hp * HEAD_DIM
    nblk = WIDTH // blk
    state_spec = lambda w: pl.BlockSpec((1, 2, hp, w, HEAD_DIM), lambda i, j: (i, 0, j, 0, 0))
    in_specs = [pl.BlockSpec((1, s, blk), lambda i, j: (i, 0, j)),
                pl.BlockSpec((blk, s), lambda i, j: (j, i)),
                pl.BlockSpec((1, s, blk), lambda i, j: (i, 0, nblk + j)),
                pl.BlockSpec((1, 1, 4 * rows, CHUNK), lambda i, j: (i, j, 0, 0))]
    args = [qvo, kt, qvo, g6]
    if states is not None:
        c0, n0, m0 = states
        in_specs += [state_spec(HEAD_DIM), state_spec(1), state_spec(1)]
        args += [c0, n0.reshape(b, 2, N_HEADS, 1, HEAD_DIM),
                 jnp.broadcast_to(m0[..., None, None], (b, 2, N_HEADS, 1, HEAD_DIM))]
    out_specs = [pl.BlockSpec((1, s, blk), lambda i, j: (i, 0, j)),
                 pl.BlockSpec((1, s, blk), lambda i, j: (i, 0, j)),
                 state_spec(HEAD_DIM), state_spec(1), state_spec(1)]
    out_shape = [jax.ShapeDtypeStruct((b, s, WIDTH), BF16),
                 jax.ShapeDtypeStruct((b, s, WIDTH), BF16),
                 jax.ShapeDtypeStruct((b, 2, N_HEADS, HEAD_DIM, HEAD_DIM), F32),
                 jax.ShapeDtypeStruct((b, 2, N_HEADS, 1, HEAD_DIM), F32),
                 jax.ShapeDtypeStruct((b, 2, N_HEADS, 1, HEAD_DIM), F32)]
    scratch = [pltpu.VMEM((2, rows, CHUNK), F32),
               pltpu.VMEM((2, rows, CHUNK), F32),
               pltpu.VMEM((2, rows, CHUNK), F32),
               pltpu.VMEM((2, rows, CHUNK), F32),
               pltpu.VMEM((nc, LANES, CHUNK), F32)]
    scratch += [pltpu.VMEM((HEAD_DIM, 2 * HEAD_DIM), F32) for _ in range(2 * hp)]
    hf, hb, cf, nf, mf = pl.pallas_call(
        functools.partial(_mlstm_kernel, nc=nc, hp=hp, zero_state=states is None),
        grid=(b, hg),
        in_specs=in_specs,
        out_specs=out_specs,
        out_shape=out_shape,
        scratch_shapes=scratch,
        compiler_params=_params(("parallel", "parallel")),
        name="mlstm",
    )(*args)
    return hf, hb, cf, nf[:, :, :, 0, :], mf[:, :, :, 0, 0]


def _attn_kernel(*refs, sub, key_chunk, n_new, n_cache, lam_init):
    if n_cache:
        lp_ref, q_ref, k_ref, kc_ref, vt_ref, vct_ref, o_ref, s_even_ref, s_odd_ref = refs
    else:
        lp_ref, q_ref, k_ref, vt_ref, o_ref, s_even_ref, s_odd_ref = refs
    lp = lp_ref[...]
    lam = (jnp.exp(jnp.sum(lp[0:1] * lp[1:2], axis=-1, keepdims=True))
           - jnp.exp(jnp.sum(lp[2:3] * lp[3:4], axis=-1, keepdims=True)) + lam_init)
    lane = lax.broadcasted_iota(jnp.int32, (sub, HEAD_DIM), 1)
    n_sub = n_new // sub
    nt = (((1,), (1,)), ((), ()))

    bounds = list(range(0, n_new, key_chunk)) + [n_new]
    chunks = [(False, c0, c0, c1 - c0) for c0, c1 in zip(bounds[:-1], bounds[1:])]
    if n_cache:
        chunks.append((True, 0, n_new, n_cache))

    def keys_of(chunk):
        cached, c0, _, n = chunk
        return kc_ref[0, c0:c0 + n, :] if cached else k_ref[0, c0:c0 + n, :]

    def values_of(chunk):
        cached, c0, _, n = chunk
        return vct_ref[0, :, c0:c0 + n] if cached else vt_ref[:, c0:c0 + n]

    def queries(idx):
        r0 = pl.multiple_of(idx * sub, sub)
        q = q_ref[0, pl.ds(r0, sub), :]
        zero = jnp.zeros_like(q)
        return jnp.concatenate([jnp.where(lane < QK_DIM, q, zero), jnp.where(lane >= QK_DIM, q, zero)], axis=0)

    def scores(qq, chunk, s_ref):
        s = lax.dot_general(keys_of(chunk), qq, nt, preferred_element_type=F32)
        s_ref[chunk[2]:chunk[2] + chunk[3], :] = s
        return jnp.max(s, axis=0, keepdims=True)

    def step(idx_next, s_next_ref, idx, s_ref, m):
        qq = None if idx_next is None else queries(idx_next)
        m_next, l, o = None, None, None
        for chunk in chunks:
            if qq is not None:
                mc = scores(qq, chunk, s_next_ref)
                m_next = mc if m_next is None else jnp.maximum(m_next, mc)
            e = jnp.exp2(s_ref[chunk[2]:chunk[2] + chunk[3], :] - m)
            lc = jnp.sum(e, axis=0, keepdims=True)
            oc = jnp.dot(values_of(chunk), e.astype(BF16), preferred_element_type=F32)
            l = lc if l is None else l + lc
            o = oc if o is None else o + oc
        o = o / l
        out_t = o[:, :sub] - lam * o[:, sub:]
        r0 = pl.multiple_of(idx * sub, sub)
        o_ref[0, pl.ds(r0, sub), :] = jnp.transpose(out_t).astype(BF16)
        return m_next

    def pair(i, m_even):
        a = 2 * i
        m_odd = step(a + 1, s_odd_ref, a, s_even_ref, m_even)
        return step(a + 2, s_even_ref, a + 1, s_odd_ref, m_odd)

    qq0 = queries(0)
    m0 = None
    for chunk in chunks:
        mc = scores(qq0, chunk, s_even_ref)
        m0 = mc if m0 is None else jnp.maximum(m0, mc)
    m_even = lax.fori_loop(0, n_sub // 2 - 1, pair, m0)
    m_odd = step(n_sub - 1, s_odd_ref, n_sub - 2, s_even_ref, m_even)
    step(None, None, n_sub - 1, s_odd_ref, m_odd)


def _attn_short_kernel(lp_ref, q_ref, k_ref, vt_ref, o_ref, *, lam_init):
    lp = lp_ref[...]
    lam = (jnp.exp(jnp.sum(lp[0:1] * lp[1:2], axis=-1, keepdims=True))
           - jnp.exp(jnp.sum(lp[2:3] * lp[3:4], axis=-1, keepdims=True)) + lam_init)
    s_len = q_ref.shape[1]
    lane = lax.broadcasted_iota(jnp.int32, (s_len, HEAD_DIM), 1)
    nt = (((1,), (1,)), ((), ()))
    for hh in range(N_HEADS):
        lanes = slice(hh * HEAD_DIM, (hh + 1) * HEAD_DIM)
        q = q_ref[0, :, lanes]
        zero = jnp.zeros_like(q)
        qq = jnp.concatenate([jnp.where(lane < QK_DIM, q, zero), jnp.where(lane >= QK_DIM, q, zero)], axis=0)
        s = lax.dot_general(k_ref[0, :, lanes], qq, nt, preferred_element_type=F32)
        e = jnp.exp2(s - jnp.max(s, axis=0, keepdims=True))
        o = jnp.dot(vt_ref[lanes, :], e.astype(BF16), preferred_element_type=F32)
        o = o / jnp.sum(e, axis=0, keepdims=True)
        out_t = o[:, :s_len] - lam * o[:, s_len:]
        o_ref[0, :, lanes] = jnp.transpose(out_t).astype(BF16)


def _attn_short(lp, q, k, vt, lam_init):
    b, s, _ = q.shape
    return pl.pallas_call(
        functools.partial(_attn_short_kernel, lam_init=lam_init),
        grid=(b,),
        in_specs=[pl.BlockSpec(lp.shape, lambda i: (0, 0)),
                  pl.BlockSpec((1, s, WIDTH), lambda i: (i, 0, 0)),
                  pl.BlockSpec((1, s, WIDTH), lambda i: (i, 0, 0)),
                  pl.BlockSpec((WIDTH, s), lambda i: (0, i))],
        out_specs=pl.BlockSpec((1, s, WIDTH), lambda i: (i, 0, 0)),
        out_shape=jax.ShapeDtypeStruct((b, s, WIDTH), BF16),
        compiler_params=_params(("parallel",)),
        name="diff_attn_short",
    )(lp, q, k, vt)


def _attn(lp, q, k, vt, cache, lam_init):
    b, s, _ = q.shape
    if cache is None and s <= ATTN_SHORT:
        return _attn_short(lp, q, k, vt, lam_init)
    assert s % (2 * ATTN_SUB) == 0
    n_cache = 0 if cache is None else cache[0].shape[1]
    in_specs = [pl.BlockSpec(lp.shape, lambda i, h: (0, 0)),
                pl.BlockSpec((1, s, HEAD_DIM), lambda i, h: (i, 0, h)),
                pl.BlockSpec((1, s, HEAD_DIM), lambda i, h: (i, 0, h))]
    args = [lp, q, k]
    if n_cache:
        in_specs.append(pl.BlockSpec((1, n_cache, HEAD_DIM), lambda i, h: (i, 0, h)))
        args.append(cache[0])
    in_specs.append(pl.BlockSpec((HEAD_DIM, s), lambda i, h: (h, i)))
    args.append(vt)
    if n_cache:
        in_specs.append(pl.BlockSpec((1, HEAD_DIM, n_cache), lambda i, h: (i, h, 0)))
        args.append(cache[1])
    keys = s + n_cache
    return pl.pallas_call(
        functools.partial(_attn_kernel, sub=ATTN_SUB, key_chunk=min(ATTN_KEY_CHUNK, s), n_new=s, n_cache=n_cache,
                          lam_init=lam_init),
        grid=(b, N_HEADS),
        in_specs=in_specs,
        out_specs=pl.BlockSpec((1, s, HEAD_DIM), lambda i, h: (i, 0, h)),
        out_shape=jax.ShapeDtypeStruct((b, s, WIDTH), BF16),
        scratch_shapes=[pltpu.VMEM((keys, 2 * ATTN_SUB), F32), pltpu.VMEM((keys, 2 * ATTN_SUB), F32)],
        compiler_params=_params(("parallel", "parallel")),
        name="diff_attn",
    )(*args)


def _head_norm(x):
    outs = []
    for j in range(N_HEADS):
        xb = x[:, j * HEAD_DIM:(j + 1) * HEAD_DIM]
        outs.append(xb * lax.rsqrt(jnp.mean(xb * xb, axis=-1, keepdims=True) + EPS))
    return jnp.concatenate(outs, axis=1)


def _mix_kernel(hf_ref, hb_ref, om_ref, hd_ref, x_ref, g1_ref, nsc2_ref, sh2_ref, gm_ref, gd_ref,
                wout_ref, wr_ref, br_ref, x1_ref, h2_ref, chl_ref, grpt_ref, *, diff_scale, parts):
    n = x_ref.shape[0] // parts
    for p in range(parts):
        _mix_rows(slice(p * n, (p + 1) * n), hf_ref, hb_ref, om_ref, hd_ref, x_ref, g1_ref, nsc2_ref, sh2_ref,
                  gm_ref, gd_ref, wout_ref, wr_ref, br_ref, x1_ref, h2_ref, chl_ref, grpt_ref, diff_scale)


def _mix_rows(rows, hf_ref, hb_ref, om_ref, hd_ref, x_ref, g1_ref, nsc2_ref, sh2_ref, gm_ref, gd_ref,
              wout_ref, wr_ref, br_ref, x1_ref, h2_ref, chl_ref, grpt_ref, diff_scale):
    hm = _head_norm(hf_ref[rows, :].astype(F32) + hb_ref[rows, :].astype(F32))
    hm = hm * gm_ref[...] * _sigmoid(om_ref[rows, :].astype(F32))
    hd = _head_norm(hd_ref[rows, :].astype(F32)) * gd_ref[...] * diff_scale
    mix = jnp.dot(jnp.concatenate([hm.astype(BF16), hd.astype(BF16)], axis=1), wout_ref[...],
                  preferred_element_type=F32)
    x1 = x_ref[rows, :] + g1_ref[0] * mix
    x1_ref[rows, :] = x1
    ms = jnp.mean(x1 * x1, axis=-1, keepdims=True)
    h2 = x1 * lax.rsqrt(ms + EPS) * nsc2_ref[0] + sh2_ref[0]
    h2_hi = h2.astype(BF16)
    h2_ref[rows, :] = h2_hi

    h2_lo = (h2 - h2_hi.astype(F32)).astype(BF16)
    both = (jnp.dot(h2_hi, wr_ref[...], preferred_element_type=F32)
            + jnp.dot(h2_lo, wr_ref[...], preferred_element_type=F32))
    logits = both[:, :LANES] + both[:, LANES:] + br_ref[...]
    lane = lax.broadcasted_iota(jnp.int32, logits.shape, 1).astype(F32)
    big = float(LANES)
    is_grp = (lane >= N_EXPERTS) & (lane < N_EXPERTS + N_GROUPS)
    lg = jnp.where(is_grp, logits, -jnp.inf)
    mx = jnp.max(lg, axis=-1, keepdims=True)
    p_grp = 1.0 / jnp.sum(jnp.exp(lg - mx), axis=-1, keepdims=True)
    grp = jnp.min(jnp.where(lg == mx, lane, big), axis=-1, keepdims=True) - N_EXPERTS
    base = grp * EXPERTS_PER_GROUP
    sel = (lane >= base) & (lane < base + EXPERTS_PER_GROUP)
    le = jnp.where(sel, logits, -jnp.inf)
    ee = jnp.exp(le - jnp.max(le, axis=-1, keepdims=True))
    pe = ee / jnp.sum(ee, axis=-1, keepdims=True)
    top1 = jnp.max(pe, axis=-1, keepdims=True)
    idx1 = jnp.min(jnp.where(sel & (pe == top1), lane, big), axis=-1, keepdims=True)
    rest = sel & (lane != idx1)
    top2 = jnp.max(jnp.where(rest, pe, -1.0), axis=-1, keepdims=True)
    idx2 = jnp.min(jnp.where(rest & (pe == top2), lane, big), axis=-1, keepdims=True)
    denom = top1 + top2
    w1 = top1 / denom * p_grp
    w2 = top2 / denom * p_grp
    chl = jnp.zeros_like(logits)
    for j in range(EXPERTS_PER_GROUP):
        cj = jnp.where(idx1 == base + j, w1, 0.0) + jnp.where(idx2 == base + j, w2, 0.0)
        hi = cj.astype(BF16).astype(F32)
        chl = chl + jnp.where(lane == j, hi, 0.0) + jnp.where(lane == EXPERTS_PER_GROUP + j, cj - hi, 0.0)
    chl_ref[rows, :] = chl.astype(BF16)
    grpt_ref[:, rows] = jnp.transpose(jnp.broadcast_to(grp, logits.shape))[:SUBLANES, :]


def _mix(hf, hb, qvo, hd, x2d, g1, nsc2, sh2, row0, tiles_per_row, gm, gd, wout, wr, br, diff_scale, tm):
    t, d = x2d.shape
    row_map = lambda i: (row0 + i // tiles_per_row, 0, 0)
    tok = lambda i: (i, 0)
    full = lambda i: (0, 0)
    return pl.pallas_call(
        functools.partial(_mix_kernel, diff_scale=diff_scale, parts=tm // MIX_PART),
        grid=(t // tm,),
        in_specs=[pl.BlockSpec((tm, WIDTH), tok),
                  pl.BlockSpec((tm, WIDTH), tok),
                  pl.BlockSpec((tm, WIDTH), lambda i: (i, 2)),
                  pl.BlockSpec((tm, WIDTH), tok),
                  pl.BlockSpec((tm, d), tok),
                  pl.BlockSpec((1, 1, d), row_map),
                  pl.BlockSpec((1, 1, d), row_map),
                  pl.BlockSpec((1, 1, d), row_map),
                  pl.BlockSpec((1, WIDTH), full),
                  pl.BlockSpec((1, WIDTH), full),
                  pl.BlockSpec(wout.shape, full),
                  pl.BlockSpec(wr.shape, full),
                  pl.BlockSpec(br.shape, full)],
        out_specs=[pl.BlockSpec((tm, d), tok),
                   pl.BlockSpec((tm, d), tok),
                   pl.BlockSpec((tm, LANES), tok),
                   pl.BlockSpec((SUBLANES, tm), lambda i: (0, i))],
        out_shape=[jax.ShapeDtypeStruct((t, d), F32),
                   jax.ShapeDtypeStruct((t, d), BF16),
                   jax.ShapeDtypeStruct((t, LANES), BF16),
                   jax.ShapeDtypeStruct((SUBLANES, t), F32)],
        compiler_params=_params(("parallel",)),
        name="mix_router",
    )(hf, hb, qvo, hd, x2d, g1, nsc2, sh2, gm, gd, wout, wr, br)


def _moe_kernel(h2_ref, chl_ref, grpt_ref, before_ref, wgu_ref, wdn_ref, x1_ref, g2_ref, nrm_ref, y_ref,
                acc_ref, rank_ref, *, hidden, blk):
    g = pl.program_id(1)
    tm = h2_ref.shape[0]

    @pl.when(g == 0)
    def _():
        acc_ref[...] = jnp.zeros_like(acc_ref)
        gid = lax.broadcasted_iota(jnp.int32, (SUBLANES, tm), 0).astype(F32)
        member = grpt_ref[...] == gid
        rank = jnp.dot(jnp.where(member, 1.0, 0.0).astype(BF16), before_ref[...], preferred_element_type=F32)
        rank_ref[...] = jnp.where(member, rank, -1.0)

    rank_g = rank_ref[pl.ds(g, 1), :]
    n_g = jnp.max(rank_g) + 1.0

    tall = blk + blk // 4
    sizes, start = [], 0
    while start < tm:
        size = min(blk // 4 if len(sizes) == 1 else blk, tm - start)
        sizes.append((start, size))
        start += size
    cases = [(0, blk, (n_g > 0) & (n_g <= blk)), (0, tall, (n_g > blk) & (n_g <= tall))]
    cases += [(start, size, n_g > max(start, tall)) for start, size in sizes]

    for start, size, cond in cases:
        @pl.when(cond)
        def _():
            slot = lax.broadcasted_iota(jnp.int32, (size, tm), 0).astype(F32) + float(start)
            pick = jnp.where(rank_g == slot, 1.0, 0.0).astype(BF16)
            xg = jnp.dot(pick, h2_ref[...], preferred_element_type=F32).astype(BF16)
            cg = jnp.dot(pick, chl_ref[...], preferred_element_type=F32)
            gus = [jnp.dot(xg, wgu_ref[e], preferred_element_type=F32) for e in range(EXPERTS_PER_GROUP)]
            out = jnp.zeros((size, acc_ref.shape[1]), F32)
            for e, gu in enumerate(gus):
                a, u = gu[:, :hidden], gu[:, hidden:]
                ce = cg[:, e:e + 1] + cg[:, EXPERTS_PER_GROUP + e:EXPERTS_PER_GROUP + e + 1]
                act = (a * _sigmoid(a) * u * ce).astype(BF16)
                out = out + jnp.dot(act, wdn_ref[e], preferred_element_type=F32)
            acc_ref[...] += lax.dot_general(pick, out.astype(BF16), (((0,), (0,)), ((), ())),
                                            preferred_element_type=F32)

    @pl.when(g == pl.num_programs(1) - 1)
    def _():
        x2 = x1_ref[...] + g2_ref[0] * acc_ref[...]
        ms = jnp.mean(x2 * x2, axis=-1, keepdims=True)
        y_ref[...] = x2 * lax.rsqrt(ms + EPS) * nrm_ref[...]


def _moe(h2, chl, grpt, wgu, wdn, x1, g2, row0, tiles_per_row, nrm, tm, blk):
    t, d = x1.shape
    _, _, two_h = wgu.shape
    hidden = two_h // 2
    row_map = lambda i, g: (row0 + i // tiles_per_row, 0, 0)
    tok = lambda i, g: (i, 0)
    before = jnp.triu(jnp.ones((tm, tm), BF16), 1)
    return pl.pallas_call(
        functools.partial(_moe_kernel, hidden=hidden, blk=blk),
        grid=(t // tm, N_GROUPS),
        in_specs=[pl.BlockSpec((tm, d), tok),
                  pl.BlockSpec((tm, LANES), tok),
                  pl.BlockSpec((SUBLANES, tm), lambda i, g: (0, i)),
                  pl.BlockSpec((tm, tm), lambda i, g: (0, 0)),
                  pl.BlockSpec((EXPERTS_PER_GROUP, d, two_h), lambda i, g: (g, 0, 0)),
                  pl.BlockSpec((EXPERTS_PER_GROUP, hidden, d), lambda i, g: (g, 0, 0)),
                  pl.BlockSpec((tm, d), tok),
                  pl.BlockSpec((1, 1, d), row_map),
                  pl.BlockSpec((1, d), lambda i, g: (0, 0))],
        out_specs=pl.BlockSpec((tm, d), tok),
        out_shape=jax.ShapeDtypeStruct((t, d), F32),
        scratch_shapes=[pltpu.VMEM((tm, d), F32), pltpu.VMEM((SUBLANES, tm), F32)],
        compiler_params=_params(("parallel", "arbitrary")),
        name="moe_experts",
    )(h2, chl, grpt, before, wgu, wdn, x1, g2, nrm)


def _layer(x, mods, row0, per_request, weights, states, cache, rope_tabs, emit_cache, tm, hp):
    b, s, d = x.shape
    sh1, sc1, g1, sh2, sc2, g2 = mods
    (nrm_mix, nrm_ffn, w_in_r, w_in_t, b_gate, gm, gd, lp, wout, wr, br, wgu, wdn, nrm_final, lam_init) = weights
    x2d = x.reshape(b * s, d)
    tiles_per_row = (s // tm) if per_request else (b * s // tm)
    outs = _proj(x2d, nrm_mix, sc1, sh1, row0, tiles_per_row, w_in_r, w_in_t, b_gate, rope_tabs, emit_cache, tm)
    qvo, kmt, qd, kd, vdt, gt = outs[:6]
    hf, hb, cf, nf, mf = _mlstm(qvo.reshape(b, s, 3 * WIDTH), kmt, gt, states, hp)
    hd = _attn(lp, qd.reshape(b, s, WIDTH), kd.reshape(b, s, WIDTH), vdt, cache, lam_init)
    x1, h2, chl, grpt = _mix(hf.reshape(b * s, WIDTH), hb.reshape(b * s, WIDTH), qvo, hd.reshape(b * s, WIDTH),
                             x2d, g1, nrm_ffn[None] * (1.0 + sc2), sh2, row0, tiles_per_row, gm, gd, wout, wr, br,
                             1.0 - lam_init, tm)
    y = _moe(h2, chl, grpt, wgu, wdn, x1, g2, row0, (s if per_request else b * s) // MOE_TILE,
             nrm_final, MOE_TILE, MOE_BLOCK)
    new_kv = outs[6:] if emit_cache else None
    return y.reshape(b, s, d), new_kv, (cf, nf, mf)


def kernel(x_prompt, x_sample, cache_k, cache_v, state_C, state_n, state_m, c, c_ctx, w_ada, b_ada, norm_mix, norm_ffn, w_in, b_gate, mlstm_norm, diff_norm, diff_lambda, w_out, w_route_group, b_route_group, w_route_expert, b_route_expert, w_gate_up, w_down, final_norm):
    depth = w_in.shape[0]
    assert depth == 1, "the final norm is fused into the last layer's expert kernel"
    bp, sp, d = x_prompt.shape
    bs, ss, _ = x_sample.shape
    rows = 16
    assert bs < rows
    cvec = jnp.zeros((rows, d), F32).at[:bs].set(c).at[bs].set(c_ctx)
    rope_tabs = _rope_tables(ss)
    xp, xs = x_prompt, x_sample
    new_k, new_v, new_c, new_n, new_m = [], [], [], [], []
    for l in range(depth):
        lam_init = 0.8 - 0.6 * math.exp(-0.3 * l)
        mod = _ada(cvec, w_ada[l], b_ada[l])
        mods = [mod[:, i * d:(i + 1) * d].reshape(rows, 1, d) for i in range(6)]
        wl = w_in[l]
        off_dq = 4 * WIDTH + N_GATES
        w_in_r = jnp.concatenate(
            [wl[:, :WIDTH], wl[:, 2 * WIDTH:4 * WIDTH], wl[:, off_dq:]], axis=1).astype(BF16)
        w_in_t = jnp.concatenate(
            [wl[:, WIDTH:2 * WIDTH], wl[:, off_dq + 2 * WIDTH:], wl[:, 4 * WIDTH:off_dq],
             jnp.zeros((d, LANES - N_GATES), F32)], axis=1).T.astype(BF16)
        bg = b_gate[l].reshape(N_GATES, 1)
        wr = jnp.concatenate([w_route_expert[l], w_route_group[l],
                              jnp.zeros((d, LANES - N_EXPERTS - N_GROUPS), F32)], axis=1)
        wr_hi = wr.astype(BF16)
        wr = jnp.concatenate([wr_hi, (wr - wr_hi.astype(F32)).astype(BF16)], axis=1)
        br = jnp.concatenate([b_route_expert[l], b_route_group[l],
                              jnp.zeros((LANES - N_EXPERTS - N_GROUPS,), F32)]).reshape(1, LANES)
        weights = (norm_mix[l].reshape(1, d), norm_ffn[l].reshape(1, d), w_in_r, w_in_t, bg,
                   mlstm_norm[l].reshape(1, WIDTH), diff_norm[l].reshape(1, WIDTH), diff_lambda[l],
                   w_out[l].astype(BF16), wr, br, w_gate_up[l].astype(BF16), w_down[l].astype(BF16),
                   final_norm.reshape(1, d), lam_init)

        xp, (nk, nv), (cf, nf, mf) = _layer(xp, mods, bs, False, weights, None, None, None, True,
                                            tm=1024, hp=4)
        new_k.append(nk.reshape(bp, sp, N_HEADS, 2, QK_DIM))
        new_v.append(nv.reshape(bp, sp, N_HEADS, HEAD_DIM))
        new_c.append(cf)
        new_n.append(nf)
        new_m.append(mf)

        states = (state_C[:, l], state_n[:, l], state_m[:, l])
        cache = (cache_k[:, l].reshape(bs, -1, WIDTH).astype(BF16),
                 cache_v[:, l].reshape(bs, -1, WIDTH).transpose(0, 2, 1).astype(BF16))
        xs, _, _ = _layer(xs, mods, 0, True, weights, states, cache, rope_tabs, False,
                          tm=1024, hp=4)

    return (xp, xs, jnp.stack(new_k, axis=1), jnp.stack(new_v, axis=1),
            jnp.stack(new_c, axis=1), jnp.stack(new_n, axis=1), jnp.stack(new_m, axis=1))
```

```python
import functools
import math

import jax
import jax.numpy as jnp
from jax import lax
from jax.experimental import pallas as pl
from jax.experimental.pallas import tpu as pltpu

F32 = jnp.float32
BF16 = jnp.bfloat16
HIGHEST = lax.Precision.HIGHEST

EPS = 1e-6
GRID_W = 64
ROPE_THETA = 10000.0
N_HEADS = 4
HEAD_DIM = 128
QK_DIM = 64
WIDTH = N_HEADS * HEAD_DIM
CHUNK = 128
N_GATES = 4 * N_HEADS
N_GROUPS = 4
EXPERTS_PER_GROUP = 4
N_EXPERTS = 16
LANES = 128
SUBLANES = 8
MOE_BLOCK = 256
MOE_TILE = 1024
MIX_PART = 256
ATTN_SUB = 128
ATTN_KEY_CHUNK = 1024
ATTN_SHORT = 512
LOG2E = math.log2(math.e)
VMEM_LIMIT = 56 * 1024 * 1024


def _params(sem):
    return pltpu.CompilerParams(dimension_semantics=sem, vmem_limit_bytes=VMEM_LIMIT)


def _log_sigmoid(x):
    return jnp.minimum(x, 0.0) - jnp.log1p(jnp.exp(-jnp.abs(x)))


def _sigmoid(x):
    return 1.0 / (1.0 + jnp.exp(-x))


def _ada_kernel(c_ref, w_ref, b_ref, o_ref):
    c = c_ref[...]
    s = c * _sigmoid(c)
    o_ref[...] = jnp.dot(s, w_ref[...], preferred_element_type=F32, precision=HIGHEST) + b_ref[...]


def _ada(cvec, w_ada, b_ada):
    rows, d = cvec.shape
    n = w_ada.shape[1]
    tn = 1024
    return pl.pallas_call(
        _ada_kernel,
        grid=(n // tn,),
        in_specs=[pl.BlockSpec((rows, d), lambda j: (0, 0)),
                  pl.BlockSpec((d, tn), lambda j: (0, j)),
                  pl.BlockSpec((1, tn), lambda j: (0, j))],
        out_specs=pl.BlockSpec((rows, tn), lambda j: (0, j)),
        out_shape=jax.ShapeDtypeStruct((rows, n), F32),
        compiler_params=_params(("arbitrary",)),
        name="ada",
    )(cvec, w_ada, b_ada.reshape(1, n))


def _proj_kernel(*refs, rope, emit_cache):
    x_ref, nrm_ref, sc_ref, sh_ref, w_ref, wt_ref, bg_ref = refs[:7]
    refs = refs[7:]
    if rope:
        cos_ref, sa_ref, sb_ref = refs[:3]
        refs = refs[3:]
    m_ref, kmt_ref, q_ref, k_ref, vt_ref, gt_ref = refs[:6]
    refs = refs[6:]
    if emit_cache:
        newk_ref, newv_ref = refs

    x = x_ref[...]
    ms = jnp.mean(x * x, axis=-1, keepdims=True)
    y = x * lax.rsqrt(ms + EPS) * nrm_ref[...]
    h = (y * (1.0 + sc_ref[0]) + sh_ref[0]).astype(BF16)

    def section(j):
        return jnp.dot(h, w_ref[:, j * WIDTH:(j + 1) * WIDTH], preferred_element_type=F32)

    def rotate(p):
        cos, sa, sb = cos_ref[...], sa_ref[...], sb_ref[...]
        outs = []
        for j in range(N_HEADS):
            xb = p[:, j * LANES:(j + 1) * LANES]
            outs.append(xb * cos + pltpu.roll(xb, LANES - 16, 1) * sa + pltpu.roll(xb, 16, 1) * sb)
        return jnp.concatenate(outs, axis=1)

    def section_t(j, n):
        return lax.dot_general(wt_ref[j * WIDTH:j * WIDTH + n, :], h, (((1,), (1,)), ((), ())),
                               preferred_element_type=F32)

    for j in range(3):
        m_ref[:, j * WIDTH:(j + 1) * WIDTH] = section(j).astype(BF16)
    kmt_ref[...] = (section_t(0, WIDTH) * (HEAD_DIM ** -0.5)).astype(BF16)
    pq = section(3)
    if rope:
        pq = rotate(pq)
    q_ref[...] = (pq * (LOG2E * QK_DIM ** -0.5)).astype(BF16)
    pk = section(4)
    if emit_cache:
        newk_ref[...] = pk
    if rope:
        pk = rotate(pk)
    k_ref[...] = pk.astype(BF16)
    if emit_cache:
        newv_ref[...] = section(5)
    vt_ref[...] = section_t(1, WIDTH).astype(BF16)
    gt_ref[...] = section_t(2, LANES)[:N_GATES, :] + bg_ref[...]


def _proj(x2d, nrm, sc, sh, row0, tiles_per_row, w_r, w_t, b_gate, rope_tabs, emit_cache, tm):
    t, d = x2d.shape
    nt = t // tm
    rope = rope_tabs is not None
    row_map = lambda i: (row0 + i // tiles_per_row, 0, 0)
    tok = lambda i: (i, 0)
    feat = lambda i: (0, i)
    in_specs = [pl.BlockSpec((tm, d), tok),
                pl.BlockSpec((1, d), lambda i: (0, 0)),
                pl.BlockSpec((1, 1, d), row_map),
                pl.BlockSpec((1, 1, d), row_map),
                pl.BlockSpec(w_r.shape, lambda i: (0, 0)),
                pl.BlockSpec(w_t.shape, lambda i: (0, 0)),
                pl.BlockSpec((N_GATES, 1), lambda i: (0, 0))]
    args = [x2d, nrm, sc, sh, w_r, w_t, b_gate]
    if rope:
        tps = rope_tabs[0].shape[0] // tm
        for tab in rope_tabs:
            in_specs.append(pl.BlockSpec((tm, LANES), lambda i: (i % tps, 0)))
            args.append(tab)
    out_shape = [jax.ShapeDtypeStruct((t, 3 * WIDTH), BF16),
                 jax.ShapeDtypeStruct((WIDTH, t), BF16),
                 jax.ShapeDtypeStruct((t, WIDTH), BF16),
                 jax.ShapeDtypeStruct((t, WIDTH), BF16),
                 jax.ShapeDtypeStruct((WIDTH, t), BF16),
                 jax.ShapeDtypeStruct((N_GATES, t), F32)]
    out_specs = [pl.BlockSpec((tm, 3 * WIDTH), tok),
                 pl.BlockSpec((WIDTH, tm), feat),
                 pl.BlockSpec((tm, WIDTH), tok),
                 pl.BlockSpec((tm, WIDTH), tok),
                 pl.BlockSpec((WIDTH, tm), feat),
                 pl.BlockSpec((N_GATES, tm), feat)]
    if emit_cache:
        out_shape += [jax.ShapeDtypeStruct((t, WIDTH), F32)] * 2
        out_specs += [pl.BlockSpec((tm, WIDTH), tok)] * 2
    return pl.pallas_call(
        functools.partial(_proj_kernel, rope=rope, emit_cache=emit_cache),
        grid=(nt,),
        in_specs=in_specs,
        out_specs=out_specs,
        out_shape=out_shape,
        compiler_params=_params(("parallel",)),
        name="proj_rope" if rope else "proj_ctx",
    )(*args)


def _rope_tables(seq):
    t = jnp.arange(seq)
    row = (t // GRID_W).astype(F32)[:, None]
    col = (t % GRID_W).astype(F32)[:, None]
    lane = jnp.arange(LANES)
    freqs = ROPE_THETA ** (-(lane % 16).astype(F32) / 16.0)
    pos = jnp.where(((lane % 64) < 32)[None, :], row, col)
    ang = pos * freqs[None, :]
    cos, sin = jnp.cos(ang), jnp.sin(ang)
    first = ((lane % 32) < 16)[None, :]
    return cos, jnp.where(first, -sin, 0.0), jnp.where(first, 0.0, sin)


def _lane_scan(x, pos, op, forward):
    n = x.shape[-1]
    shift = 1
    while shift < n:
        if forward:
            y = pltpu.roll(x, shift, 1)
            ok = pos >= shift
        else:
            y = pltpu.roll(x, n - shift, 1)
            ok = pos < n - shift
        x = jnp.where(ok, op(x, y), x)
        shift *= 2
    return x


def _mlstm_kernel(q_ref, kt_ref, v_ref, g_ref, *refs, nc, hp, zero_state):
    if not zero_state:
        c0_ref, n0_ref, m0_ref = refs[:3]
        refs = refs[3:]
    hf_ref, hb_ref, cf_ref, nf_ref, mf_ref, r_ref, mp_ref, mn_ref, dc_ref, tab_ref = refs[:10]
    st_refs = refs[10:]
    _mlstm_body(q_ref, kt_ref, v_ref, g_ref, None if zero_state else (c0_ref, n0_ref, m0_ref),
                hf_ref, hb_ref, cf_ref, nf_ref, mf_ref, r_ref, mp_ref, mn_ref, dc_ref, tab_ref, st_refs, nc, hp)


def _mlstm_body(q_ref, kt_ref, v_ref, g_ref, init, hf_ref, hb_ref, cf_ref, nf_ref, mf_ref,
                r_ref, mp_ref, mn_ref, dc_ref, tab_ref, st_refs, nc, hp):
    L = CHUNK
    rows = nc * hp
    pos = lax.broadcasted_iota(jnp.int32, (rows, L), 1)
    tab_ref[...] = jnp.zeros_like(tab_ref)
    for d in range(2):
        i_pre = g_ref[0, 0, (2 * d) * rows:(2 * d + 1) * rows, :]
        lf = _log_sigmoid(g_ref[0, 0, (2 * d + 1) * rows:(2 * d + 2) * rows, :])
        pre = _lane_scan(lf, pos, jnp.add, True)
        tot = jnp.broadcast_to(jnp.sum(lf, axis=-1, keepdims=True), lf.shape)
        b = pre if d == 0 else tot - pre + lf
        r = i_pre - b
        r_max = _lane_scan(r, pos, jnp.maximum, d == 0)
        gk = tot - b + i_pre
        g_max = jnp.broadcast_to(jnp.max(gk, axis=-1, keepdims=True), gk.shape)
        m = jnp.zeros((hp, L), F32) if init is None else init[2][0, d, :, 0, :]
        for c in (range(nc) if d == 0 else reversed(range(nc))):
            sl = slice(c * hp, (c + 1) * hp)
            mp_ref[d, sl, :] = m
            m = jnp.maximum(tot[sl] + m, g_max[sl])
            mn_ref[d, sl, :] = m
        mf_ref[0, d, :, 0, :] = m
        m_prev = mp_ref[d]
        m_new = mn_ref[d]
        m_row = jnp.maximum(m_prev, r_max)
        r_ref[d] = r
        dc_ref[d] = jnp.exp(tot + m_prev - m_new)
        w_k = jnp.exp(gk - m_new)
        floor = jnp.exp(-(b + m_row))
        for c in range(nc):
            src = slice(c * hp, (c + 1) * hp)
            slot = c if d == 0 else nc - 1 - c
            for qi, val in enumerate((m_row, floor, w_k)):
                base = (2 * qi + d) * hp
                tab_ref[slot, base:base + hp, :] = val[src]

    for d in range(2):
        for hh in range(hp):
            if init is None:
                st_refs[d * hp + hh][...] = jnp.zeros((HEAD_DIM, 2 * HEAD_DIM), F32)
                continue
            st_refs[d * hp + hh][:, :HEAD_DIM] = jnp.transpose(init[0][0, d, hh])
            st_refs[d * hp + hh][:, HEAD_DIM:] = jnp.transpose(
                jnp.broadcast_to(init[1][0, d, hh], (HEAD_DIM, HEAD_DIM)))

    rr = lax.broadcasted_iota(jnp.int32, (L, L), 0)
    cc = lax.broadcasted_iota(jnp.int32, (L, L), 1)
    masks = (cc <= rr, cc >= rr)
    ones = jnp.ones((L, HEAD_DIM), BF16)

    def body(c, carry):
        tab = jnp.transpose(tab_ref[c])
        units = [(d, hh) for d in range(2) for hh in range(hp)]

        def operands(d, hh):
            ci = c if d == 0 else nc - 1 - c
            off = pl.multiple_of(ci * L, L)
            lanes = slice(hh * HEAD_DIM, (hh + 1) * HEAD_DIM)
            return ci * hp + hh, off, lanes

        def column(d, hh, which):
            j = (2 * which + d) * hp + hh
            return jnp.broadcast_to(tab[:, j:j + 1], (L, HEAD_DIM))

        qk, inter, upd = {}, {}, {}
        for d, hh in units:
            row, off, lanes = operands(d, hh)
            q = q_ref[0, pl.ds(off, L), lanes]
            kt = kt_ref[lanes, pl.ds(off, L)]
            qk[d, hh] = jnp.dot(q, kt, preferred_element_type=F32)
            inter[d, hh] = jnp.dot(q, st_refs[d * hp + hh][...].astype(BF16), preferred_element_type=F32)
        for d, hh in units:
            row, off, lanes = operands(d, hh)
            v = v_ref[0, pl.ds(off, L), lanes]
            kt = kt_ref[lanes, pl.ds(off, L)]
            wk_col = column(d, hh, 2)
            vw = (v.astype(F32) * wk_col).astype(BF16)
            upd[d, hh] = jnp.dot(kt, jnp.concatenate([vw, wk_col.astype(BF16)], axis=1),
                                 preferred_element_type=F32)
        for d, hh in units:
            row, off, lanes = operands(d, hh)
            m_col = column(d, hh, 0)
            r_row = r_ref[d, pl.ds(row, 1), :]
            m_prev = mp_ref[d, pl.ds(row, 1), :]
            v = v_ref[0, pl.ds(off, L), lanes]
            w_intra = jnp.exp(jnp.where(masks[d], r_row - m_col, -jnp.inf))
            s = (qk[d, hh] * w_intra).astype(BF16)
            intra = jnp.dot(s, jnp.concatenate([v, ones], axis=1), preferred_element_type=F32)
            w_state = jnp.exp(m_prev - m_col)
            both = intra + jnp.concatenate([w_state, w_state], axis=1) * inter[d, hh]
            den = jnp.maximum(jnp.abs(both[:, HEAD_DIM:]), column(d, hh, 1))
            out_ref = hf_ref if d == 0 else hb_ref
            out_ref[0, pl.ds(off, L), lanes] = (both[:, :HEAD_DIM] / den).astype(BF16)
        for d, hh in units:
            row, off, lanes = operands(d, hh)
            decay = dc_ref[d, pl.ds(row, 1), :]
            st = st_refs[d * hp + hh]
            st[...] = jnp.concatenate([decay, decay], axis=1) * st[...] + upd[d, hh]
        return carry

    lax.fori_loop(0, nc, body, 0, unroll=2)

    for d in range(2):
        for hh in range(hp):
            state = st_refs[d * hp + hh][...]
            cf_ref[0, d, hh] = jnp.transpose(state[:, :HEAD_DIM])
            nf_ref[0, d, hh] = jnp.transpose(state[:, HEAD_DIM:])[0:1, :]


def _mlstm(qvo, kt, gt, states, hp):
    b, s, _ = qvo.shape
    nc = s // CHUNK
    hg = N_HEADS // hp
    rows = nc * hp
    g6 = gt.reshape(4, hg, hp, b, nc, CHUNK).transpose(3, 1, 0, 4, 2, 5).reshape(b, hg, 4 * rows, CHUNK)
    blk = hp * HEAD_DIM
    nblk = WIDTH // blk
    state_spec = lambda w: pl.BlockSpec((1, 2, hp, w, HEAD_DIM), lambda i, j: (i, 0, j, 0, 0))
    in_specs = [pl.BlockSpec((1, s, blk), lambda i, j: (i, 0, j)),
                pl.BlockSpec((blk, s), lambda i, j: (j, i)),
                pl.BlockSpec((1, s, blk), lambda i, j: (i, 0, nblk + j)),
                pl.BlockSpec((1, 1, 4 * rows, CHUNK), lambda i, j: (i, j, 0, 0))]
    args = [qvo, kt, qvo, g6]
    if states is not None:
        c0, n0, m0 = states
        in_specs += [state_spec(HEAD_DIM), state_spec(1), state_spec(1)]
        args += [c0, n0.reshape(b, 2, N_HEADS, 1, HEAD_DIM),
                 jnp.broadcast_to(m0[..., None, None], (b, 2, N_HEADS, 1, HEAD_DIM))]
    out_specs = [pl.BlockSpec((1, s, blk), lambda i, j: (i, 0, j)),
                 pl.BlockSpec((1, s, blk), lambda i, j: (i, 0, j)),
                 state_spec(HEAD_DIM), state_spec(1), state_spec(1)]
    out_shape = [jax.ShapeDtypeStruct((b, s, WIDTH), BF16),
                 jax.ShapeDtypeStruct((b, s, WIDTH), BF16),
                 jax.ShapeDtypeStruct((b, 2, N_HEADS, HEAD_DIM, HEAD_DIM), F32),
                 jax.ShapeDtypeStruct((b, 2, N_HEADS, 1, HEAD_DIM), F32),
                 jax.ShapeDtypeStruct((b, 2, N_HEADS, 1, HEAD_DIM), F32)]
    scratch = [pltpu.VMEM((2, rows, CHUNK), F32),
               pltpu.VMEM((2, rows, CHUNK), F32),
               pltpu.VMEM((2, rows, CHUNK), F32),
               pltpu.VMEM((2, rows, CHUNK), F32),
               pltpu.VMEM((nc, LANES, CHUNK), F32)]
    scratch += [pltpu.VMEM((HEAD_DIM, 2 * HEAD_DIM), F32) for _ in range(2 * hp)]
    hf, hb, cf, nf, mf = pl.pallas_call(
        functools.partial(_mlstm_kernel, nc=nc, hp=hp, zero_state=states is None),
        grid=(b, hg),
        in_specs=in_specs,
        out_specs=out_specs,
        out_shape=out_shape,
        scratch_shapes=scratch,
        compiler_params=_params(("parallel", "parallel")),
        name="mlstm",
    )(*args)
    return hf, hb, cf, nf[:, :, :, 0, :], mf[:, :, :, 0, 0]


def _attn_kernel(*refs, sub, key_chunk, n_new, n_cache, lam_init):
    if n_cache:
        lp_ref, q_ref, k_ref, kc_ref, vt_ref, vct_ref, o_ref, s_even_ref, s_odd_ref = refs
    else:
        lp_ref, q_ref, k_ref, vt_ref, o_ref, s_even_ref, s_odd_ref = refs
    lp = lp_ref[...]
    lam = (jnp.exp(jnp.sum(lp[0:1] * lp[1:2], axis=-1, keepdims=True))
           - jnp.exp(jnp.sum(lp[2:3] * lp[3:4], axis=-1, keepdims=True)) + lam_init)
    lane = lax.broadcasted_iota(jnp.int32, (sub, HEAD_DIM), 1)
    n_sub = n_new // sub
    nt = (((1,), (1,)), ((), ()))

    bounds = list(range(0, n_new, key_chunk)) + [n_new]
    chunks = [(False, c0, c0, c1 - c0) for c0, c1 in zip(bounds[:-1], bounds[1:])]
    if n_cache:
        chunks.append((True, 0, n_new, n_cache))

    def keys_of(chunk):
        cached, c0, _, n = chunk
        return kc_ref[0, c0:c0 + n, :] if cached else k_ref[0, c0:c0 + n, :]

    def values_of(chunk):
        cached, c0, _, n = chunk
        return vct_ref[0, :, c0:c0 + n] if cached else vt_ref[:, c0:c0 + n]

    def queries(idx):
        r0 = pl.multiple_of(idx * sub, sub)
        q = q_ref[0, pl.ds(r0, sub), :]
        zero = jnp.zeros_like(q)
        return jnp.concatenate([jnp.where(lane < QK_DIM, q, zero), jnp.where(lane >= QK_DIM, q, zero)], axis=0)

    def scores(qq, chunk, s_ref):
        s = lax.dot_general(keys_of(chunk), qq, nt, preferred_element_type=F32)
        s_ref[chunk[2]:chunk[2] + chunk[3], :] = s
        return jnp.max(s, axis=0, keepdims=True)

    def step(idx_next, s_next_ref, idx, s_ref, m):
        qq = None if idx_next is None else queries(idx_next)
        m_next, l, o = None, None, None
        for chunk in chunks:
            if qq is not None:
                mc = scores(qq, chunk, s_next_ref)
                m_next = mc if m_next is None else jnp.maximum(m_next, mc)
            e = jnp.exp2(s_ref[chunk[2]:chunk[2] + chunk[3], :] - m)
            lc = jnp.sum(e, axis=0, keepdims=True)
            oc = jnp.dot(values_of(chunk), e.astype(BF16), preferred_element_type=F32)
            l = lc if l is None else l + lc
            o = oc if o is None else o + oc
        o = o / l
        out_t = o[:, :sub] - lam * o[:, sub:]
        r0 = pl.multiple_of(idx * sub, sub)
        o_ref[0, pl.ds(r0, sub), :] = jnp.transpose(out_t).astype(BF16)
        return m_next

    def pair(i, m_even):
        a = 2 * i
        m_odd = step(a + 1, s_odd_ref, a, s_even_ref, m_even)
        return step(a + 2, s_even_ref, a + 1, s_odd_ref, m_odd)

    qq0 = queries(0)
    m0 = None
    for chunk in chunks:
        mc = scores(qq0, chunk, s_even_ref)
        m0 = mc if m0 is None else jnp.maximum(m0, mc)
    m_even = lax.fori_loop(0, n_sub // 2 - 1, pair, m0)
    m_odd = step(n_sub - 1, s_odd_ref, n_sub - 2, s_even_ref, m_even)
    step(None, None, n_sub - 1, s_odd_ref, m_odd)


def _attn_short_kernel(lp_ref, q_ref, k_ref, vt_ref, o_ref, *, lam_init):
    lp = lp_ref[...]
    lam = (jnp.exp(jnp.sum(lp[0:1] * lp[1:2], axis=-1, keepdims=True))
           - jnp.exp(jnp.sum(lp[2:3] * lp[3:4], axis=-1, keepdims=True)) + lam_init)
    s_len = q_ref.shape[1]
    lane = lax.broadcasted_iota(jnp.int32, (s_len, HEAD_DIM), 1)
    nt = (((1,), (1,)), ((), ()))
    for hh in range(N_HEADS):
        lanes = slice(hh * HEAD_DIM, (hh + 1) * HEAD_DIM)
        q = q_ref[0, :, lanes]
        zero = jnp.zeros_like(q)
        qq = jnp.concatenate([jnp.where(lane < QK_DIM, q, zero), jnp.where(lane >= QK_DIM, q, zero)], axis=0)
        s = lax.dot_general(k_ref[0, :, lanes], qq, nt, preferred_element_type=F32)
        e = jnp.exp2(s - jnp.max(s, axis=0, keepdims=True))
        o = jnp.dot(vt_ref[lanes, :], e.astype(BF16), preferred_element_type=F32)
        o = o / jnp.sum(e, axis=0, keepdims=True)
        out_t = o[:, :s_len] - lam * o[:, s_len:]
        o_ref[0, :, lanes] = jnp.transpose(out_t).astype(BF16)


def _attn_short(lp, q, k, vt, lam_init):
    b, s, _ = q.shape
    return pl.pallas_call(
        functools.partial(_attn_short_kernel, lam_init=lam_init),
        grid=(b,),
        in_specs=[pl.BlockSpec(lp.shape, lambda i: (0, 0)),
                  pl.BlockSpec((1, s, WIDTH), lambda i: (i, 0, 0)),
                  pl.BlockSpec((1, s, WIDTH), lambda i: (i, 0, 0)),
                  pl.BlockSpec((WIDTH, s), lambda i: (0, i))],
        out_specs=pl.BlockSpec((1, s, WIDTH), lambda i: (i, 0, 0)),
        out_shape=jax.ShapeDtypeStruct((b, s, WIDTH), BF16),
        compiler_params=_params(("parallel",)),
        name="diff_attn_short",
    )(lp, q, k, vt)


def _attn(lp, q, k, vt, cache, lam_init):
    b, s, _ = q.shape
    if cache is None and s <= ATTN_SHORT:
        return _attn_short(lp, q, k, vt, lam_init)
    assert s % (2 * ATTN_SUB) == 0
    n_cache = 0 if cache is None else cache[0].shape[1]
    in_specs = [pl.BlockSpec(lp.shape, lambda i, h: (0, 0)),
                pl.BlockSpec((1, s, HEAD_DIM), lambda i, h: (i, 0, h)),
                pl.BlockSpec((1, s, HEAD_DIM), lambda i, h: (i, 0, h))]
    args = [lp, q, k]
    if n_cache:
        in_specs.append(pl.BlockSpec((1, n_cache, HEAD_DIM), lambda i, h: (i, 0, h)))
        args.append(cache[0])
    in_specs.append(pl.BlockSpec((HEAD_DIM, s), lambda i, h: (h, i)))
    args.append(vt)
    if n_cache:
        in_specs.append(pl.BlockSpec((1, HEAD_DIM, n_cache), lambda i, h: (i, h, 0)))
        args.append(cache[1])
    keys = s + n_cache
    return pl.pallas_call(
        functools.partial(_attn_kernel, sub=ATTN_SUB, key_chunk=min(ATTN_KEY_CHUNK, s), n_new=s, n_cache=n_cache,
                          lam_init=lam_init),
        grid=(b, N_HEADS),
        in_specs=in_specs,
        out_specs=pl.BlockSpec((1, s, HEAD_DIM), lambda i, h: (i, 0, h)),
        out_shape=jax.ShapeDtypeStruct((b, s, WIDTH), BF16),
        scratch_shapes=[pltpu.VMEM((keys, 2 * ATTN_SUB), F32), pltpu.VMEM((keys, 2 * ATTN_SUB), F32)],
        compiler_params=_params(("parallel", "parallel")),
        name="diff_attn",
    )(*args)


def _head_norm(x):
    outs = []
    for j in range(N_HEADS):
        xb = x[:, j * HEAD_DIM:(j + 1) * HEAD_DIM]
        outs.append(xb * lax.rsqrt(jnp.mean(xb * xb, axis=-1, keepdims=True) + EPS))
    return jnp.concatenate(outs, axis=1)


def _mix_kernel(hf_ref, hb_ref, om_ref, hd_ref, x_ref, g1_ref, nsc2_ref, sh2_ref, gm_ref, gd_ref,
                wout_ref, wr_ref, br_ref, x1_ref, h2_ref, chl_ref, grpt_ref, *, diff_scale, parts):
    n = x_ref.shape[0] // parts
    for p in range(parts):
        _mix_rows(slice(p * n, (p + 1) * n), hf_ref, hb_ref, om_ref, hd_ref, x_ref, g1_ref, nsc2_ref, sh2_ref,
                  gm_ref, gd_ref, wout_ref, wr_ref, br_ref, x1_ref, h2_ref, chl_ref, grpt_ref, diff_scale)


def _mix_rows(rows, hf_ref, hb_ref, om_ref, hd_ref, x_ref, g1_ref, nsc2_ref, sh2_ref, gm_ref, gd_ref,
              wout_ref, wr_ref, br_ref, x1_ref, h2_ref, chl_ref, grpt_ref, diff_scale):
    hm = _head_norm(hf_ref[rows, :].astype(F32) + hb_ref[rows, :].astype(F32))
    hm = hm * gm_ref[...] * _sigmoid(om_ref[rows, :].astype(F32))
    hd = _head_norm(hd_ref[rows, :].astype(F32)) * gd_ref[...] * diff_scale
    mix = jnp.dot(jnp.concatenate([hm.astype(BF16), hd.astype(BF16)], axis=1), wout_ref[...],
                  preferred_element_type=F32)
    x1 = x_ref[rows, :] + g1_ref[0] * mix
    x1_ref[rows, :] = x1
    ms = jnp.mean(x1 * x1, axis=-1, keepdims=True)
    h2 = x1 * lax.rsqrt(ms + EPS) * nsc2_ref[0] + sh2_ref[0]
    h2_hi = h2.astype(BF16)
    h2_ref[rows, :] = h2_hi

    h2_lo = (h2 - h2_hi.astype(F32)).astype(BF16)
    both = (jnp.dot(h2_hi, wr_ref[...], preferred_element_type=F32)
            + jnp.dot(h2_lo, wr_ref[...], preferred_element_type=F32))
    logits = both[:, :LANES] + both[:, LANES:] + br_ref[...]
    lane = lax.broadcasted_iota(jnp.int32, logits.shape, 1).astype(F32)
    big = float(LANES)
    is_grp = (lane >= N_EXPERTS) & (lane < N_EXPERTS + N_GROUPS)
    lg = jnp.where(is_grp, logits, -jnp.inf)
    mx = jnp.max(lg, axis=-1, keepdims=True)
    p_grp = 1.0 / jnp.sum(jnp.exp(lg - mx), axis=-1, keepdims=True)
    grp = jnp.min(jnp.where(lg == mx, lane, big), axis=-1, keepdims=True) - N_EXPERTS
    base = grp * EXPERTS_PER_GROUP
    sel = (lane >= base) & (lane < base + EXPERTS_PER_GROUP)
    le = jnp.where(sel, logits, -jnp.inf)
    ee = jnp.exp(le - jnp.max(le, axis=-1, keepdims=True))
    pe = ee / jnp.sum(ee, axis=-1, keepdims=True)
    top1 = jnp.max(pe, axis=-1, keepdims=True)
    idx1 = jnp.min(jnp.where(sel & (pe == top1), lane, big), axis=-1, keepdims=True)
    rest = sel & (lane != idx1)
    top2 = jnp.max(jnp.where(rest, pe, -1.0), axis=-1, keepdims=True)
    idx2 = jnp.min(jnp.where(rest & (pe == top2), lane, big), axis=-1, keepdims=True)
    denom = top1 + top2
    w1 = top1 / denom * p_grp
    w2 = top2 / denom * p_grp
    chl = jnp.zeros_like(logits)
    for j in range(EXPERTS_PER_GROUP):
        cj = jnp.where(idx1 == base + j, w1, 0.0) + jnp.where(idx2 == base + j, w2, 0.0)
        hi = cj.astype(BF16).astype(F32)
        chl = chl + jnp.where(lane == j, hi, 0.0) + jnp.where(lane == EXPERTS_PER_GROUP + j, cj - hi, 0.0)
    chl_ref[rows, :] = chl.astype(BF16)
    grpt_ref[:, rows] = jnp.transpose(jnp.broadcast_to(grp, logits.shape))[:SUBLANES, :]


def _mix(hf, hb, qvo, hd, x2d, g1, nsc2, sh2, row0, tiles_per_row, gm, gd, wout, wr, br, diff_scale, tm):
    t, d = x2d.shape
    row_map = lambda i: (row0 + i // tiles_per_row, 0, 0)
    tok = lambda i: (i, 0)
    full = lambda i: (0, 0)
    return pl.pallas_call(
        functools.partial(_mix_kernel, diff_scale=diff_scale, parts=tm // MIX_PART),
        grid=(t // tm,),
        in_specs=[pl.BlockSpec((tm, WIDTH), tok),
                  pl.BlockSpec((tm, WIDTH), tok),
                  pl.BlockSpec((tm, WIDTH), lambda i: (i, 2)),
                  pl.BlockSpec((tm, WIDTH), tok),
                  pl.BlockSpec((tm, d), tok),
                  pl.BlockSpec((1, 1, d), row_map),
                  pl.BlockSpec((1, 1, d), row_map),
                  pl.BlockSpec((1, 1, d), row_map),
                  pl.BlockSpec((1, WIDTH), full),
                  pl.BlockSpec((1, WIDTH), full),
                  pl.BlockSpec(wout.shape, full),
                  pl.BlockSpec(wr.shape, full),
                  pl.BlockSpec(br.shape, full)],
        out_specs=[pl.BlockSpec((tm, d), tok),
                   pl.BlockSpec((tm, d), tok),
                   pl.BlockSpec((tm, LANES), tok),
                   pl.BlockSpec((SUBLANES, tm), lambda i: (0, i))],
        out_shape=[jax.ShapeDtypeStruct((t, d), F32),
                   jax.ShapeDtypeStruct((t, d), BF16),
                   jax.ShapeDtypeStruct((t, LANES), BF16),
                   jax.ShapeDtypeStruct((SUBLANES, t), F32)],
        compiler_params=_params(("parallel",)),
        name="mix_router",
    )(hf, hb, qvo, hd, x2d, g1, nsc2, sh2, gm, gd, wout, wr, br)


def _moe_kernel(h2_ref, chl_ref, grpt_ref, before_ref, wgu_ref, wdn_ref, x1_ref, g2_ref, nrm_ref, y_ref,
                acc_ref, rank_ref, *, hidden, blk):
    g = pl.program_id(1)
    tm = h2_ref.shape[0]

    @pl.when(g == 0)
    def _():
        acc_ref[...] = jnp.zeros_like(acc_ref)
        gid = lax.broadcasted_iota(jnp.int32, (SUBLANES, tm), 0).astype(F32)
        member = grpt_ref[...] == gid
        rank = jnp.dot(jnp.where(member, 1.0, 0.0).astype(BF16), before_ref[...], preferred_element_type=F32)
        rank_ref[...] = jnp.where(member, rank, -1.0)

    rank_g = rank_ref[pl.ds(g, 1), :]
    n_g = jnp.max(rank_g) + 1.0

    sizes, start = [], 0
    while start < tm:
        size = min((blk, blk // 4, blk // 4, blk // 2)[len(sizes)] if len(sizes) < 4 else blk, tm - start)
        sizes.append((start, size))
        start += size

    for start, size in sizes:
        @pl.when(n_g > start)
        def _():
            slot = lax.broadcasted_iota(jnp.int32, (size, tm), 0).astype(F32) + float(start)
            pick = jnp.where(rank_g == slot, 1.0, 0.0).astype(BF16)
            xg = jnp.dot(pick, h2_ref[...], preferred_element_type=F32).astype(BF16)
            cg = jnp.dot(pick, chl_ref[...], preferred_element_type=F32)
            gus = [jnp.dot(xg, wgu_ref[e], preferred_element_type=F32) for e in range(EXPERTS_PER_GROUP)]
            out = jnp.zeros((size, acc_ref.shape[1]), F32)
            for e, gu in enumerate(gus):
                a, u = gu[:, :hidden], gu[:, hidden:]
                ce = cg[:, e:e + 1] + cg[:, EXPERTS_PER_GROUP + e:EXPERTS_PER_GROUP + e + 1]
                act = (a * _sigmoid(a) * u * ce).astype(BF16)
                out = out + jnp.dot(act, wdn_ref[e], preferred_element_type=F32)
            acc_ref[...] += lax.dot_general(pick, out.astype(BF16), (((0,), (0,)), ((), ())),
                                            preferred_element_type=F32)

    @pl.when(g == pl.num_programs(1) - 1)
    def _():
        x2 = x1_ref[...] + g2_ref[0] * acc_ref[...]
        ms = jnp.mean(x2 * x2, axis=-1, keepdims=True)
        y_ref[...] = x2 * lax.rsqrt(ms + EPS) * nrm_ref[...]


def _moe(h2, chl, grpt, wgu, wdn, x1, g2, row0, tiles_per_row, nrm, tm, blk):
    t, d = x1.shape
    _, _, two_h = wgu.shape
    hidden = two_h // 2
    row_map = lambda i, g: (row0 + i // tiles_per_row, 0, 0)
    tok = lambda i, g: (i, 0)
    before = jnp.triu(jnp.ones((tm, tm), BF16), 1)
    return pl.pallas_call(
        functools.partial(_moe_kernel, hidden=hidden, blk=blk),
        grid=(t // tm, N_GROUPS),
        in_specs=[pl.BlockSpec((tm, d), tok),
                  pl.BlockSpec((tm, LANES), tok),
                  pl.BlockSpec((SUBLANES, tm), lambda i, g: (0, i)),
                  pl.BlockSpec((tm, tm), lambda i, g: (0, 0)),
                  pl.BlockSpec((EXPERTS_PER_GROUP, d, two_h), lambda i, g: (g, 0, 0)),
                  pl.BlockSpec((EXPERTS_PER_GROUP, hidden, d), lambda i, g: (g, 0, 0)),
                  pl.BlockSpec((tm, d), tok),
                  pl.BlockSpec((1, 1, d), row_map),
                  pl.BlockSpec((1, d), lambda i, g: (0, 0))],
        out_specs=pl.BlockSpec((tm, d), tok),
        out_shape=jax.ShapeDtypeStruct((t, d), F32),
        scratch_shapes=[pltpu.VMEM((tm, d), F32), pltpu.VMEM((SUBLANES, tm), F32)],
        compiler_params=_params(("parallel", "arbitrary")),
        name="moe_experts",
    )(h2, chl, grpt, before, wgu, wdn, x1, g2, nrm)


def _layer(x, mods, row0, per_request, weights, states, cache, rope_tabs, emit_cache, tm, hp):
    b, s, d = x.shape
    sh1, sc1, g1, sh2, sc2, g2 = mods
    (nrm_mix, nrm_ffn, w_in_r, w_in_t, b_gate, gm, gd, lp, wout, wr, br, wgu, wdn, nrm_final, lam_init) = weights
    x2d = x.reshape(b * s, d)
    tiles_per_row = (s // tm) if per_request else (b * s // tm)
    outs = _proj(x2d, nrm_mix, sc1, sh1, row0, tiles_per_row, w_in_r, w_in_t, b_gate, rope_tabs, emit_cache, tm)
    qvo, kmt, qd, kd, vdt, gt = outs[:6]
    hf, hb, cf, nf, mf = _mlstm(qvo.reshape(b, s, 3 * WIDTH), kmt, gt, states, hp)
    hd = _attn(lp, qd.reshape(b, s, WIDTH), kd.reshape(b, s, WIDTH), vdt, cache, lam_init)
    x1, h2, chl, grpt = _mix(hf.reshape(b * s, WIDTH), hb.reshape(b * s, WIDTH), qvo, hd.reshape(b * s, WIDTH),
                             x2d, g1, nrm_ffn[None] * (1.0 + sc2), sh2, row0, tiles_per_row, gm, gd, wout, wr, br,
                             1.0 - lam_init, tm)
    y = _moe(h2, chl, grpt, wgu, wdn, x1, g2, row0, (s if per_request else b * s) // MOE_TILE,
             nrm_final, MOE_TILE, MOE_BLOCK)
    new_kv = outs[6:] if emit_cache else None
    return y.reshape(b, s, d), new_kv, (cf, nf, mf)


def kernel(x_prompt, x_sample, cache_k, cache_v, state_C, state_n, state_m, c, c_ctx, w_ada, b_ada, norm_mix, norm_ffn, w_in, b_gate, mlstm_norm, diff_norm, diff_lambda, w_out, w_route_group, b_route_group, w_route_expert, b_route_expert, w_gate_up, w_down, final_norm):
    depth = w_in.shape[0]
    assert depth == 1, "the final norm is fused into the last layer's expert kernel"
    bp, sp, d = x_prompt.shape
    bs, ss, _ = x_sample.shape
    rows = 16
    assert bs < rows
    cvec = jnp.zeros((rows, d), F32).at[:bs].set(c).at[bs].set(c_ctx)
    rope_tabs = _rope_tables(ss)
    xp, xs = x_prompt, x_sample
    new_k, new_v, new_c, new_n, new_m = [], [], [], [], []
    for l in range(depth):
        lam_init = 0.8 - 0.6 * math.exp(-0.3 * l)
        mod = _ada(cvec, w_ada[l], b_ada[l])
        mods = [mod[:, i * d:(i + 1) * d].reshape(rows, 1, d) for i in range(6)]
        wl = w_in[l]
        off_dq = 4 * WIDTH + N_GATES
        w_in_r = jnp.concatenate(
            [wl[:, :WIDTH], wl[:, 2 * WIDTH:4 * WIDTH], wl[:, off_dq:]], axis=1).astype(BF16)
        w_in_t = jnp.concatenate(
            [wl[:, WIDTH:2 * WIDTH], wl[:, off_dq + 2 * WIDTH:], wl[:, 4 * WIDTH:off_dq],
             jnp.zeros((d, LANES - N_GATES), F32)], axis=1).T.astype(BF16)
        bg = b_gate[l].reshape(N_GATES, 1)
        wr = jnp.concatenate([w_route_expert[l], w_route_group[l],
                              jnp.zeros((d, LANES - N_EXPERTS - N_GROUPS), F32)], axis=1)
        wr_hi = wr.astype(BF16)
        wr = jnp.concatenate([wr_hi, (wr - wr_hi.astype(F32)).astype(BF16)], axis=1)
        br = jnp.concatenate([b_route_expert[l], b_route_group[l],
                              jnp.zeros((LANES - N_EXPERTS - N_GROUPS,), F32)]).reshape(1, LANES)
        weights = (norm_mix[l].reshape(1, d), norm_ffn[l].reshape(1, d), w_in_r, w_in_t, bg,
                   mlstm_norm[l].reshape(1, WIDTH), diff_norm[l].reshape(1, WIDTH), diff_lambda[l],
                   w_out[l].astype(BF16), wr, br, w_gate_up[l].astype(BF16), w_down[l].astype(BF16),
                   final_norm.reshape(1, d), lam_init)

        xp, (nk, nv), (cf, nf, mf) = _layer(xp, mods, bs, False, weights, None, None, None, True,
                                            tm=1024, hp=4)
        new_k.append(nk.reshape(bp, sp, N_HEADS, 2, QK_DIM))
        new_v.append(nv.reshape(bp, sp, N_HEADS, HEAD_DIM))
        new_c.append(cf)
        new_n.append(nf)
        new_m.append(mf)

        states = (state_C[:, l], state_n[:, l], state_m[:, l])
        cache = (cache_k[:, l].reshape(bs, -1, WIDTH).astype(BF16),
                 cache_v[:, l].reshape(bs, -1, WIDTH).transpose(0, 2, 1).astype(BF16))
        xs, _, _ = _layer(xs, mods, 0, True, weights, states, cache, rope_tabs, False,
                          tm=1024, hp=4)

    return (xp, xs, jnp.stack(new_k, axis=1), jnp.stack(new_v, axis=1),
            jnp.stack(new_c, axis=1), jnp.stack(new_n, axis=1), jnp.stack(new_m, axis=1))
```
